```python
import math
import jax, jax.numpy as jnp
from jax import lax
import numpy as np

D_MODEL = 1024
BATCH = 8
SEQ = 4096
DEPTH = 2
DEC_BATCH = 32
DEC_SEQ = 8
PAST_LEN = 16384
PAGE_SIZE = 128

N_MIXERS = 2
N_POOL_LAYERS = (DEPTH + 1) // 2
N_NSA_LAYERS = DEPTH // 2
ALPHA = (2.0 * DEPTH) ** 0.25
BETA = (8.0 * DEPTH) ** -0.25
LN_EPS = 1e-5

POOL_GROUPS = 4
POOL_WINDOWS = (2, 4, 8, 16)
POOL_CH = D_MODEL // POOL_GROUPS
POOL_BUF = max(POOL_WINDOWS) - 1

N_HEADS = 16
KV_HEADS = 4
HEAD_DIM = D_MODEL // N_HEADS
Q_PER_KV = N_HEADS // KV_HEADS
ROT_DIM = HEAD_DIM // 4
ROPE_THETA = 500000.0
CMP_BLOCK = 32
CMP_STRIDE = 16
CMP_HID = 128
SEL_BLOCK = 64
N_SEL = 16
WINDOW = 512
Q_BLOCK = 128
KV_W = 2 * KV_HEADS * HEAD_DIM
IN_W = N_HEADS * HEAD_DIM + 3 * KV_W + 3 * N_HEADS

D_FF = 2816
CONV_W = 3

NEG = -1e30
BIG = 1e9

kernel_name = 'hybrid_pool_nsa_convffn_decode_step'


def layer_norm(x, g, b):
    xf = x.astype(jnp.float32)
    mu = jnp.mean(xf, -1, keepdims=True)
    var = jnp.mean(jnp.square(xf - mu), -1, keepdims=True)
    y = (xf - mu) * lax.rsqrt(var + LN_EPS)
    return (y * g.astype(jnp.float32) + b.astype(jnp.float32)).astype(x.dtype)


def masked_softmax(s, mask):
    s = jnp.where(mask, s.astype(jnp.float32), NEG)
    return jnp.where(mask, jax.nn.softmax(s, axis=-1), 0.0)


def partial_rope(x, pos):
    half = ROT_DIM // 2
    inv = jnp.power(ROPE_THETA, -2.0 * jnp.arange(half, dtype=jnp.float32) / ROT_DIM)
    ang = pos.astype(jnp.float32)[:, None] * inv[None, :]
    cos = jnp.cos(ang)[None, :, None, :].astype(x.dtype)
    sin = jnp.sin(ang)[None, :, None, :].astype(x.dtype)
    x1 = x[..., :half]
    x2 = x[..., half:ROT_DIM]
    return jnp.concatenate([x1 * cos - x2 * sin, x2 * cos + x1 * sin, x[..., ROT_DIM:]], -1)


def pool_mixer(x, prefix, pos0, w, scale):
    B, T, _ = x.shape
    xx = jnp.concatenate([prefix, x], 1).astype(jnp.float32)
    csum = jnp.concatenate([jnp.zeros((B, 1, D_MODEL), jnp.float32), jnp.cumsum(xx, 1)], 1)
    end = csum[:, POOL_BUF + 1:]
    pos = pos0 + jnp.arange(T)
    outs = []
    for gi, win in enumerate(POOL_WINDOWS):
        sl = slice(gi * POOL_CH, (gi + 1) * POOL_CH)
        start = csum[:, POOL_BUF + 1 - win:POOL_BUF + 1 - win + T, sl]
        cnt = jnp.minimum(pos + 1, win).astype(jnp.float32)[None, :, None]
        outs.append((end[..., sl] - start) / cnt - xx[:, POOL_BUF:, sl])
    d = jnp.stack(outs, 2).astype(x.dtype)
    y = jnp.einsum('btgc,gce->btge', d, w).reshape(B, T, D_MODEL) * scale
    new_prefix = jnp.concatenate([prefix, x], 1)[:, -POOL_BUF:]
    return y, new_prefix


def conv_ffn(x, prefix, w_up, conv_w, conv_b, w_down):
    T = x.shape[1]
    up = x @ w_up
    up_pad = jnp.concatenate([prefix, up], 1)
    c = conv_b
    for k in range(CONV_W):
        c = c + conv_w[k] * up_pad[:, k:k + T]
    g, u = jnp.split(c, 2, -1)
    y = (jax.nn.silu(g) * u) @ w_down
    return y, up_pad[:, -(CONV_W - 1):]


def nsa_project(x, pos, w_in):
    B, T, _ = x.shape
    p = x @ w_in
    nq = N_HEADS * HEAD_DIM
    q = p[..., :nq].reshape(B, T, N_HEADS, HEAD_DIM)
    kv = p[..., nq:nq + 3 * KV_W].reshape(B, T, 3, 2, KV_HEADS, HEAD_DIM)
    gates = jax.nn.sigmoid(p[..., nq + 3 * KV_W:].astype(jnp.float32)).astype(x.dtype).reshape(B, T, N_HEADS, 3)
    q = partial_rope(q, pos) * (HEAD_DIM ** -0.5)
    k = partial_rope(kv[:, :, :, 0].reshape(B, T, 3 * KV_HEADS, HEAD_DIM), pos).reshape(B, T, 3, KV_HEADS, HEAD_DIM)
    kv = jnp.stack([k, kv[:, :, :, 1]], 3)
    return q, kv[:, :, 0], kv[:, :, 1], kv[:, :, 2], gates


def compress(rows, w1, b1, pos_emb, w2, b2):
    B, T = rows.shape[:2]
    n_half = T // CMP_STRIDE
    r = CMP_BLOCK // CMP_STRIDE
    nc = n_half - r + 1
    halves = rows[:, :n_half * CMP_STRIDE].reshape(B, n_half, CMP_STRIDE, 2, KV_HEADS, HEAD_DIM)
    outs = []
    for kvi in range(2):
        h = b1[kvi] + jnp.einsum('td,tdh->h', pos_emb[kvi], w1[kvi])
        for ri in range(r):
            w1r = w1[kvi, ri * CMP_STRIDE:(ri + 1) * CMP_STRIDE]
            pr = jnp.einsum('bntgd,tdh->bngh', halves[:, :, :, kvi], w1r)
            h = h + pr[:, ri:ri + nc]
        outs.append(jax.nn.gelu(h) @ w2[kvi] + b2[kvi])
    cend = jnp.arange(nc) * CMP_STRIDE + CMP_BLOCK - 1
    return outs[0], outs[1], cend


def nsa_core(q, gates, qpos, ck, cv, cend, gather_fn, n_blocks, wkv, wpos):
    B, Q = q.shape[:2]
    qg = q.reshape(B, Q, KV_HEADS, Q_PER_KV, HEAD_DIM)
    t5 = qpos[None, :, None, None, None]
    s = jnp.einsum('bqgrd,bngd->bqgrn', qg, ck)
    p_cmp = masked_softmax(s, cend[None, None, None, None, :] <= t5)
    o_cmp = jnp.einsum('bqgrn,bngd->bqgrd', p_cmp.astype(cv.dtype), cv)
    imp = jnp.sum(p_cmp, 3)
    nc = imp.shape[-1]
    lead = CMP_BLOCK // CMP_STRIDE - 1
    ratio = SEL_BLOCK // CMP_STRIDE
    n_off = ratio + lead
    imp = jnp.pad(imp, ((0, 0), (0, 0), (0, 0), (lead, ratio * n_blocks + n_off - nc - lead)))
    blk = imp[..., 0:ratio * n_blocks:ratio]
    for o in range(1, n_off):
        blk = blk + imp[..., o:o + ratio * n_blocks:ratio]
    jb = jnp.arange(n_blocks)
    tq = qpos[None, :, None, None]
    cur = tq // SEL_BLOCK
    valid = jb * SEL_BLOCK <= tq
    forced = (jb == 0) | (jb == cur) | (jb == cur - 1)
    score = jnp.where(valid & forced, BIG, jnp.where(valid, blk, -1.0))
    k_sel = min(N_SEL, n_blocks)
    top_val, idx = lax.top_k(score, k_sel)
    sel_ok = top_val >= 0.0
    kv_sel = gather_fn(idx)
    kpos = idx[..., None] * SEL_BLOCK + jnp.arange(SEL_BLOCK)
    m = sel_ok[..., None] & (kpos <= qpos[None, :, None, None, None])
    s = jnp.einsum('bqgrd,bqgksd->bqgrks', qg, kv_sel[..., 0, :])
    s = s.reshape(B, Q, KV_HEADS, Q_PER_KV, k_sel * SEL_BLOCK)
    p = masked_softmax(s, m.reshape(B, Q, KV_HEADS, 1, k_sel * SEL_BLOCK))
    p = p.reshape(B, Q, KV_HEADS, Q_PER_KV, k_sel, SEL_BLOCK).astype(kv_sel.dtype)
    o_slc = jnp.einsum('bqgrks,bqgksd->bqgrd', p, kv_sel[..., 1, :])
    s = jnp.einsum('bqgrd,bmgd->bqgrm', qg, wkv[:, :, 0])
    dist = qpos[:, None] - wpos[None, :]
    wm = (wpos[None, :] >= 0) & (dist >= 0) & (dist < WINDOW)
    p = masked_softmax(s, wm[None, :, None, None, :])
    o_win = jnp.einsum('bqgrm,bmgd->bqgrd', p.astype(wkv.dtype), wkv[:, :, 1])
    g = gates.reshape(B, Q, KV_HEADS, Q_PER_KV, 3)
    o = g[..., 0:1] * o_cmp + g[..., 1:2] * o_slc + g[..., 2:3] * o_win
    return o.reshape(B, Q, N_HEADS * HEAD_DIM)


def nsa_prompt(x, w_in, w_o, cw):
    B, T, _ = x.shape
    pos = jnp.arange(T)
    q, cmp_kv, slc_kv, win_kv, gates = nsa_project(x, pos, w_in)
    ck, cv, cend = compress(cmp_kv, *cw)
    n_blocks = T // SEL_BLOCK
    kb = slc_kv.reshape(B, n_blocks, SEL_BLOCK, 2, KV_HEADS, HEAD_DIM)
    b_idx = jnp.arange(B)[:, None, None, None]
    g_idx = jnp.arange(KV_HEADS)[None, None, :, None]

    def gather(idx):
        return kb[b_idx, idx, :, :, g_idx, :]

    win_pad = jnp.pad(win_kv, ((0, 0), (WINDOW, 0), (0, 0), (0, 0), (0, 0)))

    def q_block(qb):
        s0 = qb * Q_BLOCK
        qpos = s0 + jnp.arange(Q_BLOCK)
        qblk = lax.dynamic_slice_in_dim(q, s0, Q_BLOCK, 1)
        gblk = lax.dynamic_slice_in_dim(gates, s0, Q_BLOCK, 1)
        wblk = lax.dynamic_slice_in_dim(win_pad, s0, Q_BLOCK + WINDOW, 1)
        wpos = s0 - WINDOW + jnp.arange(Q_BLOCK + WINDOW)
        return nsa_core(qblk, gblk, qpos, ck, cv, cend, gather, n_blocks, wblk, wpos)

    o = lax.map(q_block, jnp.arange(T // Q_BLOCK))
    o = jnp.moveaxis(o, 0, 1).reshape(B, T, N_HEADS * HEAD_DIM)
    win_len = min(WINDOW, T)
    return o @ w_o, cmp_kv, slc_kv, win_kv[:, T - win_len:]


def nsa_sample(x, cmp_pool, slc_pool, win_buf, page_table, w_in, w_o, cw):
    B, T_new, _ = x.shape
    pos = PAST_LEN + jnp.arange(T_new)
    q, cmp_kv, slc_kv, win_kv, gates = nsa_project(x, pos, w_in)
    n_pages = PAST_LEN // PAGE_SIZE
    past_cmp = cmp_pool[page_table].reshape(B, n_pages * PAGE_SIZE, 2, KV_HEADS, HEAD_DIM)
    ck, cv, cend = compress(jnp.concatenate([past_cmp, cmp_kv], 1), *cw)
    n_past_blk = PAST_LEN // SEL_BLOCK
    n_tail = -(-T_new // SEL_BLOCK)
    n_blocks = n_past_blk + n_tail
    blk_per_page = PAGE_SIZE // SEL_BLOCK
    pool_b = slc_pool.reshape(-1, blk_per_page, SEL_BLOCK, 2, KV_HEADS, HEAD_DIM)
    tail = jnp.pad(slc_kv, ((0, 0), (0, n_tail * SEL_BLOCK - T_new), (0, 0), (0, 0), (0, 0)))
    tail = tail.reshape(B, n_tail, SEL_BLOCK, 2, KV_HEADS, HEAD_DIM)
    b_idx = jnp.arange(B)[:, None, None, None]
    g_idx = jnp.arange(KV_HEADS)[None, None, :, None]

    def gather(idx):
        jp = jnp.minimum(idx, n_past_blk - 1)
        page = page_table[b_idx, jp // blk_per_page]
        from_pool = pool_b[page, jp % blk_per_page, :, :, g_idx, :]
        jt = jnp.clip(idx - n_past_blk, 0, n_tail - 1)
        from_tail = tail[b_idx, jt, :, :, g_idx, :]
        return jnp.where((idx >= n_past_blk)[..., None, None, None], from_tail, from_pool)

    buf_len = win_buf.shape[1]
    wkv = jnp.concatenate([win_buf, win_kv], 1)
    wpos = PAST_LEN - buf_len + jnp.arange(wkv.shape[1])
    o = nsa_core(q, gates, pos, ck, cv, cend, gather, n_blocks, wkv, wpos)
    return o @ w_o, cmp_kv, slc_kv, wkv[:, -buf_len:]


def setup_inputs(seed: int = 0) -> dict:
    key = jax.random.key(seed)
    ks = iter(jax.random.split(key, 40))

    def nrm(shape, scale):
        return jax.random.normal(next(ks), shape, jnp.float32) * scale

    n_pages = PAST_LEN // PAGE_SIZE
    n_phys = (5 * DEC_BATCH * n_pages + 3) // 4
    page_table = jax.random.permutation(next(ks), n_phys)[:DEC_BATCH * n_pages].reshape(DEC_BATCH, n_pages).astype(jnp.int32)
    win_buf = min(WINDOW, PAST_LEN)
    kv_row = (2, KV_HEADS, HEAD_DIM)
    return {
        'x_prompt': nrm((BATCH, SEQ, D_MODEL), 1.0),
        'x_sample': nrm((DEC_BATCH, DEC_SEQ, D_MODEL), 1.0),
        'state_pool': nrm((N_POOL_LAYERS, DEC_BATCH, POOL_BUF, D_MODEL), 1.0),
        'cache_cmp_kv': nrm((N_NSA_LAYERS, n_phys, PAGE_SIZE) + kv_row, 1.0),
        'cache_slc_kv': nrm((N_NSA_LAYERS, n_phys, PAGE_SIZE) + kv_row, 1.0),
        'state_win_kv': nrm((N_NSA_LAYERS, DEC_BATCH, win_buf) + kv_row, 1.0),
        'state_ffn': nrm((DEPTH, DEC_BATCH, CONV_W - 1, 2 * D_FF), 1.0),
        'page_table': page_table,
        'ln_g': 1.0 + nrm((DEPTH, 2, D_MODEL), 0.05),
        'ln_b': nrm((DEPTH, 2, D_MODEL), 0.01),
        'pool_w': nrm((N_POOL_LAYERS, POOL_GROUPS, POOL_CH, POOL_CH), BETA * POOL_CH ** -0.5),
        'pool_scale': 1.0 + nrm((N_POOL_LAYERS, D_MODEL), 0.1),
        'nsa_w_in': nrm((N_NSA_LAYERS, D_MODEL, IN_W), D_MODEL ** -0.5),
        'nsa_w_o': nrm((N_NSA_LAYERS, N_HEADS * HEAD_DIM, D_MODEL), BETA * (N_HEADS * HEAD_DIM) ** -0.5),
        'cmp_w1': nrm((N_NSA_LAYERS, 2, CMP_BLOCK, HEAD_DIM, CMP_HID), (CMP_BLOCK * HEAD_DIM) ** -0.5),
        'cmp_b1': nrm((N_NSA_LAYERS, 2, CMP_HID), 0.01),
        'cmp_pos': nrm((N_NSA_LAYERS, 2, CMP_BLOCK, HEAD_DIM), 0.02),
        'cmp_w2': nrm((N_NSA_LAYERS, 2, CMP_HID, HEAD_DIM), CMP_HID ** -0.5),
        'cmp_b2': nrm((N_NSA_LAYERS, 2, HEAD_DIM), 0.01),
        'ffn_w_up': nrm((DEPTH, D_MODEL, 2 * D_FF), D_MODEL ** -0.5),
        'ffn_conv_w': nrm((DEPTH, CONV_W, 2 * D_FF), CONV_W ** -0.5),
        'ffn_conv_b': nrm((DEPTH, 2 * D_FF), 0.01),
        'ffn_w_down': nrm((DEPTH, D_FF, D_MODEL), BETA * D_FF ** -0.5),
    }


def reference(x_prompt, x_sample, state_pool, cache_cmp_kv, cache_slc_kv, state_win_kv, state_ffn, page_table,
              ln_g, ln_b, pool_w, pool_scale, nsa_w_in, nsa_w_o, cmp_w1, cmp_b1, cmp_pos, cmp_w2, cmp_b2,
              ffn_w_up, ffn_conv_w, ffn_conv_b, ffn_w_down):
    hp, hs = x_prompt, x_sample
    bp = hp.shape[0]
    pool_p, pool_s, cmp_p, cmp_s, slc_p, slc_s, win_p, win_s, ffn_p, ffn_s = ([] for _ in range(10))
    for i in range(DEPTH):
        li = i // N_MIXERS
        if i % N_MIXERS == 0:
            mp, npp = pool_mixer(hp, jnp.zeros((bp, POOL_BUF, D_MODEL), hp.dtype), 0, pool_w[li], pool_scale[li])
            ms, nps = pool_mixer(hs, state_pool[li], PAST_LEN, pool_w[li], pool_scale[li])
            pool_p.append(npp)
            pool_s.append(nps)
        else:
            cw = (cmp_w1[li], cmp_b1[li], cmp_pos[li], cmp_w2[li], cmp_b2[li])
            mp, c_p, s_p, w_p = nsa_prompt(hp, nsa_w_in[li], nsa_w_o[li], cw)
            ms, c_s, s_s, w_s = nsa_sample(hs, cache_cmp_kv[li], cache_slc_kv[li], state_win_kv[li], page_table,
                                           nsa_w_in[li], nsa_w_o[li], cw)
            cmp_p.append(c_p)
            cmp_s.append(c_s)
            slc_p.append(s_p)
            slc_s.append(s_s)
            win_p.append(w_p)
            win_s.append(w_s)
        hp = layer_norm(ALPHA * hp + mp, ln_g[i, 0], ln_b[i, 0])
        hs = layer_norm(ALPHA * hs + ms, ln_g[i, 0], ln_b[i, 0])
        fp, nfp = conv_ffn(hp, jnp.zeros((bp, CONV_W - 1, 2 * D_FF), hp.dtype),
                           ffn_w_up[i], ffn_conv_w[i], ffn_conv_b[i], ffn_w_down[i])
        fs, nfs = conv_ffn(hs, state_ffn[i], ffn_w_up[i], ffn_conv_w[i], ffn_conv_b[i], ffn_w_down[i])
        ffn_p.append(nfp)
        ffn_s.append(nfs)
        hp = layer_norm(ALPHA * hp + fp, ln_g[i, 1], ln_b[i, 1])
        hs = layer_norm(ALPHA * hs + fs, ln_g[i, 1], ln_b[i, 1])
    return (hp, hs, jnp.stack(pool_p), jnp.stack(pool_s), jnp.stack(cmp_p), jnp.stack(cmp_s),
            jnp.stack(slc_p), jnp.stack(slc_s), jnp.stack(win_p), jnp.stack(win_s),
            jnp.stack(ffn_p), jnp.stack(ffn_s))
```

```python
import functools

import numpy as np
import jax
import jax.numpy as jnp
from jax import lax
from jax.experimental import pallas as pl
from jax.experimental.pallas import tpu as pltpu

D_MODEL = 1024
DEPTH = 2
ALPHA = (2.0 * DEPTH) ** 0.25
LN_EPS = 1e-5
POOL_WINDOWS = (2, 4, 8, 16)
POOL_CH = D_MODEL // len(POOL_WINDOWS)
POOL_BUF = max(POOL_WINDOWS) - 1
N_HEADS = 16
KV_HEADS = 4
HEAD_DIM = 64
Q_PER_KV = N_HEADS // KV_HEADS
ROT_DIM = HEAD_DIM // 4
ROPE_THETA = 500000.0
CMP_BLOCK = 32
CMP_STRIDE = 16
CMP_HID = 128
SEL_BLOCK = 64
N_SEL = 16
WINDOW = 512
KV_W = 2 * KV_HEADS * HEAD_DIM
D_FF = 2816
CONV_W = 3
PAGE_SIZE = 128
NEG = -1e30
BIG = 1e9

LANES = 128
SUBLANES = 8
Q_TILE = 128
SLC_KT = 256
PAGES_PER_STEP = 16
VMEM_LIMIT = 56 * 1024 * 1024

F32 = jnp.float32
BF16 = jnp.bfloat16
NT_DIMS = (((1,), (1,)), ((), ()))
TN_DIMS = (((0,), (0,)), ((), ()))


def _cparams(sem):
    return pltpu.CompilerParams(dimension_semantics=sem, vmem_limit_bytes=VMEM_LIMIT)


def _layer_norm(h, g, b):
    mu = jnp.mean(h, axis=-1, keepdims=True)
    hc = h - mu
    var = jnp.mean(hc * hc, axis=-1, keepdims=True)
    return hc * lax.rsqrt(var + LN_EPS) * g + b


def _const_spec(shape):
    nd = len(shape)
    return pl.BlockSpec(shape, lambda *_: (0,) * nd, pipeline_mode=pl.Buffered(1))


def _pool_ln_kernel(x_ref, halo_ref, pre_ref, w_ref, sc_ref, g_ref, b_ref, o_ref, buf, *, tT, pos0):
    t = pl.program_id(1)
    x = x_ref[0]
    buf[0:16, :] = jnp.where(t == 0, pre_ref[0], halo_ref[0])
    buf[16:16 + tT, :] = x
    pos = pos0 + t * tT + lax.broadcasted_iota(jnp.int32, (tT, 1), 0)
    ys = []
    for gi, win in enumerate(POOL_WINDOWS):
        c0 = gi * POOL_CH
        xg = x[:, c0:c0 + POOL_CH]
        s = xg
        for i in range(1, win):
            s = s + buf[pl.ds(16 - i, tT), c0:c0 + POOL_CH]
        cnt = jnp.minimum(pos + 1, win).astype(F32)
        d = s / cnt - xg
        ys.append(jnp.dot(d.astype(BF16), w_ref[gi], preferred_element_type=F32))
    y = jnp.concatenate(ys, axis=1) * sc_ref[...]
    o_ref[0] = _layer_norm(ALPHA * x + y, g_ref[...], b_ref[...])


def _pool_ln(x, prefix, pos0, w, scale, g, b, tT):
    B, T, D = x.shape
    nT = T // tT
    pre16 = jnp.concatenate([jnp.zeros((B, 1, D), x.dtype), prefix], axis=1)
    if T >= 16:
        halo_src = x
        per = tT // 16
        halo_spec = pl.BlockSpec((1, 16, D), lambda bi, ti: (bi, jnp.maximum(ti * per - 1, 0), 0))
    else:
        halo_src = pre16
        halo_spec = pl.BlockSpec((1, 16, D), lambda bi, ti: (bi, 0, 0))
    return pl.pallas_call(
        functools.partial(_pool_ln_kernel, tT=tT, pos0=pos0),
        grid=(B, nT),
        in_specs=[
            pl.BlockSpec((1, tT, D), lambda bi, ti: (bi, ti, 0)),
            halo_spec,
            pl.BlockSpec((1, 16, D), lambda bi, ti: (bi, 0, 0)),
            _const_spec((len(POOL_WINDOWS), POOL_CH, POOL_CH)),
            _const_spec((1, D)), _const_spec((1, D)), _const_spec((1, D)),
        ],
        out_specs=pl.BlockSpec((1, tT, D), lambda bi, ti: (bi, ti, 0)),
        out_shape=jax.ShapeDtypeStruct((B, T, D), F32),
        scratch_shapes=[pltpu.VMEM((16 + tT, D), F32)],
        compiler_params=_cparams(("parallel", "parallel")),
        name="pool_ln",
    )(x, halo_src, pre16, w.astype(BF16), scale.reshape(1, D), g.reshape(1, D), b.reshape(1, D))


FFN_CHUNK = 256


def _ffn_ln_kernel(h_ref, pre_ref, wup_ref, cw_ref, cb_ref, wdn_ref, g_ref, b_ref, o_ref, st_ref,
                   ubuf, act, *, tT):
    t = pl.program_id(1)

    @pl.when(t == 0)
    def _():
        st_ref[0] = pre_ref[0]

    h = h_ref[0]
    hb = h.astype(BF16)
    fc = FFN_CHUNK
    for j in range(D_FF // fc):
        cs = []
        for half in range(2):
            c0 = half * D_FF + j * fc
            up = jnp.dot(hb, wup_ref[:, c0:c0 + fc], preferred_element_type=F32)
            ubuf[half, 0:8, :] = st_ref[0, :, c0:c0 + fc]
            ubuf[half, 8:8 + tT, :] = up
            st_ref[0, :, c0:c0 + fc] = ubuf[half, tT:tT + 8, :]
            c = (cb_ref[:, c0:c0 + fc]
                 + cw_ref[0:1, c0:c0 + fc] * ubuf[half, pl.ds(6, tT), :]
                 + cw_ref[1:2, c0:c0 + fc] * ubuf[half, pl.ds(7, tT), :]
                 + cw_ref[2:3, c0:c0 + fc] * up)
            cs.append(c)
        a = cs[0] * jax.nn.sigmoid(cs[0]) * cs[1]
        act[:, j * fc:(j + 1) * fc] = a.astype(BF16)
    y = jnp.dot(act[...], wdn_ref[...], preferred_element_type=F32)
    o_ref[0] = _layer_norm(ALPHA * h + y, g_ref[...], b_ref[...])


def _ffn_ln(h, prefix, w_up, conv_w, conv_b, w_down, g, b, tT):
    B, T, D = h.shape
    nT = T // tT
    F2 = 2 * D_FF
    pre8 = jnp.concatenate([jnp.zeros((B, 8 - (CONV_W - 1), F2), F32), prefix], axis=1)
    out, st = pl.pallas_call(
        functools.partial(_ffn_ln_kernel, tT=tT),
        grid=(B, nT),
        in_specs=[
            pl.BlockSpec((1, tT, D), lambda bi, ti: (bi, ti, 0)),
            pl.BlockSpec((1, 8, F2), lambda bi, ti: (bi, 0, 0)),
            _const_spec((D, F2)),
            _const_spec((CONV_W, F2)),
            _const_spec((1, F2)),
            _const_spec((D_FF, D)),
            _const_spec((1, D)), _const_spec((1, D)),
        ],
        out_specs=[pl.BlockSpec((1, tT, D), lambda bi, ti: (bi, ti, 0)),
                   pl.BlockSpec((1, 8, F2), lambda bi, ti: (bi, 0, 0))],
        out_shape=[jax.ShapeDtypeStruct((B, T, D), F32), jax.ShapeDtypeStruct((B, 8, F2), F32)],
        scratch_shapes=[pltpu.VMEM((2, tT + 8, FFN_CHUNK), F32), pltpu.VMEM((tT, D_FF), BF16)],
        compiler_params=_cparams(("parallel", "arbitrary")),
        name="ffn_ln",
    )(h, pre8, w_up.astype(BF16), conv_w, conv_b.reshape(1, F2), w_down.astype(BF16),
      g.reshape(1, D), b.reshape(1, D))
    return out, st[:, 8 - (CONV_W - 1):, :]


def _rope_tables(pos):
    half = ROT_DIM // 2
    inv = jnp.power(ROPE_THETA, -2.0 * jnp.arange(half, dtype=F32) / ROT_DIM)
    ang = pos.astype(F32)[:, None] * inv[None, :]
    cos, sin = jnp.cos(ang), jnp.sin(ang)
    T = pos.shape[0]
    one, zero = jnp.ones((T, HEAD_DIM - ROT_DIM), F32), jnp.zeros((T, HEAD_DIM - half), F32)
    c64 = jnp.concatenate([cos, cos, one], axis=1)
    lo64 = jnp.concatenate([-sin, zero], axis=1)
    hi64 = jnp.concatenate([jnp.zeros((T, half), F32), sin, jnp.zeros((T, HEAD_DIM - ROT_DIM), F32)], axis=1)
    tile = lambda a: jnp.concatenate([a, a], axis=1)
    return tile(c64), tile(lo64), tile(hi64)


def _rope_chunk(x, c, lo, hi):
    return x * c + pltpu.roll(x, LANES - ROT_DIM // 2, axis=1) * lo + pltpu.roll(x, ROT_DIM // 2, axis=1) * hi


def _nsa_proj_kernel(h_ref, wq_ref, wkv_ref, wg_ref, c_ref, lo_ref, hi_ref, *outs, tT, head_major):
    t = pl.program_id(1)
    hb = h_ref[0].astype(BF16)
    c, lo, hi = c_ref[...], lo_ref[...], hi_ref[...]
    q = jnp.dot(hb, wq_ref[...], preferred_element_type=F32)
    kv = jnp.dot(hb, wkv_ref[...], preferred_element_type=F32)
    gl = jnp.dot(hb, wg_ref[...], preferred_element_type=F32)
    gates = jax.nn.sigmoid(gl)
    lane = lax.broadcasted_iota(jnp.int32, (tT, LANES), 1)
    low = lane < HEAD_DIM
    scale = HEAD_DIM ** -0.5
    qc = [_rope_chunk(q[:, i * LANES:(i + 1) * LANES], c, lo, hi) * scale for i in range(D_MODEL // LANES)]
    kvc = []
    for br in range(3):
        for i in range(KV_W // LANES):
            x = kv[:, br * KV_W + i * LANES: br * KV_W + (i + 1) * LANES]
            kvc.append(_rope_chunk(x, c, lo, hi) if i < KV_W // (2 * LANES) else x)
    if head_major:
        cmp_ref, slc_ref, win_ref, qh_ref, skp_ref, sv_ref, wk_ref, wv_ref, g_ref = outs
    else:
        cmp_ref, slc_ref, win_ref, qf_ref, g_ref = outs
    for br, ref in enumerate((cmp_ref, slc_ref, win_ref)):
        ref[0] = jnp.concatenate(kvc[br * 4:(br + 1) * 4], axis=1)
    g_ref[0] = gates
    if not head_major:
        qf_ref[0] = jnp.concatenate(qc, axis=1).astype(BF16)
        return

    def split(x):
        return jnp.where(low, x, 0.0), jnp.where(low, pltpu.roll(x, HEAD_DIM, axis=1), 0.0)

    for i in range(D_MODEL // LANES):
        a, b = split(qc[i])
        qh_ref[0, 2 * i] = a.astype(BF16)
        qh_ref[0, 2 * i + 1] = b.astype(BF16)
    blk = (t * tT + lax.broadcasted_iota(jnp.int32, (tT, LANES), 0)) // SEL_BLOCK
    onehot = jnp.where(lane - HEAD_DIM == blk, 1.0, 0.0)
    for br, (kref, vref) in ((1, (skp_ref, sv_ref)), (2, (wk_ref, wv_ref))):
        for i in range(2):
            ka, kb = split(kvc[br * 4 + i])
            va, vb = split(kvc[br * 4 + 2 + i])
            if br == 1:
                ka, kb = ka + onehot, kb + onehot
            kref[0, 2 * i] = ka.astype(BF16)
            kref[0, 2 * i + 1] = kb.astype(BF16)
            vref[0, 2 * i] = va.astype(BF16)
            vref[0, 2 * i + 1] = vb.astype(BF16)


def _nsa_proj(h, pos, w_in, tT, head_major):
    B, T, D = h.shape
    nT = T // tT
    nq = N_HEADS * HEAD_DIM
    wq = w_in[:, :nq].astype(BF16)
    wkv = w_in[:, nq:nq + 3 * KV_W].astype(BF16)
    wg = w_in[:, nq + 3 * KV_W:]
    if head_major:
        GW = KV_HEADS * LANES
        wg = jnp.pad(wg.reshape(D, KV_HEADS, Q_PER_KV * 3), ((0, 0), (0, 0), (0, LANES - Q_PER_KV * 3))).reshape(D, GW)
    else:
        GW = LANES
        wg = jnp.pad(wg, ((0, 0), (0, LANES - wg.shape[1])))
    wg = wg.astype(BF16)
    c, lo, hi = _rope_tables(pos)
    row = lambda w: pl.BlockSpec((1, tT, w), lambda bi, ti: (bi, ti, 0))
    hm = lambda n: pl.BlockSpec((1, n, tT, LANES), lambda bi, ti: (bi, 0, ti, 0))
    tab = pl.BlockSpec((tT, LANES), lambda bi, ti: (ti, 0))
    out_specs = [row(KV_W), row(KV_W), row(KV_W)]
    out_shape = [jax.ShapeDtypeStruct((B, T, KV_W), F32)] * 3
    if head_major:
        out_specs += [hm(N_HEADS)] + [hm(KV_HEADS)] * 4 + [row(GW)]
        out_shape += ([jax.ShapeDtypeStruct((B, N_HEADS, T, LANES), BF16)]
                      + [jax.ShapeDtypeStruct((B, KV_HEADS, T, LANES), BF16)] * 4
                      + [jax.ShapeDtypeStruct((B, T, GW), F32)])
    else:
        out_specs += [row(D), row(GW)]
        out_shape += [jax.ShapeDtypeStruct((B, T, D), BF16), jax.ShapeDtypeStruct((B, T, GW), F32)]
    return pl.pallas_call(
        functools.partial(_nsa_proj_kernel, tT=tT, head_major=head_major),
        grid=(B, nT),
        in_specs=[row(D), _const_spec((D, nq)), _const_spec((D, 3 * KV_W)), _const_spec((D, GW)), tab, tab, tab],
        out_specs=out_specs,
        out_shape=out_shape,
        compiler_params=_cparams(("parallel", "parallel")),
        name="nsa_proj",
    )(h, wq, wkv, wg, c, lo, hi)


def _gelu_tanh(x):
    return 0.5 * x * (1.0 + jnp.tanh(0.7978845608028654 * (x + 0.044715 * x * x * x)))


def _compress_kernel(*refs, n_x, n_prefetch, head_major):
    refs = refs[n_prefetch:]
    x_refs = refs[:n_x]
    w1_ref, w1f_ref, pos_ref, b1_ref, w2_ref, b2_ref, ck_ref, cv_ref, carry = refs[n_x:]
    pt = pl.program_id(1)

    @pl.when(pt == 0)
    def _():
        carry[...] = jnp.zeros_like(carry)

    rows = [r.shape[1] for r in x_refs]
    M = sum(rows)
    first = lax.broadcasted_iota(jnp.int32, (M, CMP_HID), 0) == 0
    for kvi, out_ref in enumerate((ck_ref, cv_ref)):
        b1 = b1_ref[kvi:kvi + 1, :] + jnp.dot(pos_ref[kvi], w1f_ref[kvi], preferred_element_type=F32)[0:1, :]
        for g in range(KV_HEADS):
            acc = None
            for t in range(CMP_STRIDE):
                off = t * KV_W + kvi * (KV_W // 2) + g * HEAD_DIM
                z = jnp.concatenate([r[0, :, off:off + HEAD_DIM] for r in x_refs], axis=0).astype(BF16)
                d = jnp.dot(z, w1_ref[kvi, t], preferred_element_type=F32)
                acc = d if acc is None else acc + d
            pr0, pr1 = acc[:, :CMP_HID], acc[:, CMP_HID:]
            ci = kvi * KV_HEADS + g
            prev = jnp.where(first, carry[ci, 0:1, :], pltpu.roll(pr0, 1, axis=0))
            carry[ci, 0:1, :] = pr0[M - 1:M, :]
            hid = _gelu_tanh(prev + pr1 + b1)
            o = jnp.dot(hid.astype(BF16), w2_ref[kvi], preferred_element_type=F32) + b2_ref[kvi:kvi + 1, :]
            if head_major:
                out_ref[0, g] = jnp.concatenate([o, jnp.zeros_like(o)], axis=1).astype(BF16)
            else:
                out_ref[0, :, g * HEAD_DIM:(g + 1) * HEAD_DIM] = o.astype(BF16)


def _compress_weights(cw):
    w1, b1, pos_emb, w2, b2 = cw
    r = CMP_BLOCK // CMP_STRIDE
    assert r == 2
    w1cat = jnp.concatenate([w1[:, :CMP_STRIDE], w1[:, CMP_STRIDE:]], axis=-1).astype(BF16)
    w1f = w1.reshape(2, CMP_BLOCK * HEAD_DIM, CMP_HID).astype(BF16)
    posf = jnp.broadcast_to(pos_emb.reshape(2, 1, CMP_BLOCK * HEAD_DIM), (2, SUBLANES, CMP_BLOCK * HEAD_DIM)).astype(BF16)
    return w1cat, w1f, posf, b1, w2.astype(BF16), b2


def _compress_call(x_args, x_specs, grid, n_rows_total, tile_rows, cw, head_major, B, extra_prefetch=None):
    w1cat, w1f, posf, b1, w2, b2 = _compress_weights(cw)
    n_x = len(x_args)
    nsp = 0 if extra_prefetch is None else 1

    def cs(shape):
        nd = len(shape)
        return pl.BlockSpec(shape, lambda *_: (0,) * nd, pipeline_mode=pl.Buffered(1))

    w_specs = [cs(w1cat.shape), cs(w1f.shape), cs(posf.shape), cs(b1.shape), cs(w2.shape), cs(b2.shape)]
    if head_major:
        o_spec = pl.BlockSpec((1, KV_HEADS, tile_rows, LANES), lambda bi, pi, *_: (bi, 0, pi, 0))
        o_shape = jax.ShapeDtypeStruct((B, KV_HEADS, n_rows_total, LANES), BF16)
    else:
        o_spec = pl.BlockSpec((1, tile_rows, KV_W // 2), lambda bi, pi, *_: (bi, pi, 0))
        o_shape = jax.ShapeDtypeStruct((B, n_rows_total, KV_W // 2), BF16)
    gs = pltpu.PrefetchScalarGridSpec(
        num_scalar_prefetch=nsp, grid=grid,
        in_specs=list(x_specs) + w_specs,
        out_specs=[o_spec, o_spec],
        scratch_shapes=[pltpu.VMEM((2 * KV_HEADS, SUBLANES, CMP_HID), F32)],
    )
    args = ([] if extra_prefetch is None else [extra_prefetch]) + list(x_args) + [w1cat, w1f, posf, b1, w2, b2]
    return pl.pallas_call(
        functools.partial(_compress_kernel, n_x=n_x, n_prefetch=nsp, head_major=head_major),
        grid_spec=gs,
        out_shape=[o_shape, o_shape],
        compiler_params=_cparams(("parallel", "arbitrary")),
        name="compress_hm" if head_major else "compress_pg",
    )(*args)


def _block_sum_matrix(n_out, n_in, n_blocks, n_cmp):
    a = np.zeros((n_out, n_in), np.float32)
    for j in range(n_blocks):
        for m in range(4 * j, 4 * j + 5):
            if 1 <= m <= n_cmp:
                a[j, m] = 1.0
    return a


def _attn_prompt_kernel(q_ref, ck_ref, cv_ref, kp_ref, vs_ref, wk_ref, wv_ref, gt_ref, at_ref, o_ref,
                        m_sc, l_sc, acc_sc, *, n_cmp_rows, n_blk_rows):
    qt = pl.program_id(2)
    s0 = qt * Q_TILE
    R = Q_PER_KV * Q_TILE
    Q = q_ref[0, 0].reshape(R, LANES)
    tq = s0 + (lax.broadcasted_iota(jnp.int32, (R, 1), 0) & (Q_TILE - 1))

    sc = lax.dot_general(Q, ck_ref[0, 0], NT_DIMS, preferred_element_type=F32)
    mrow = lax.broadcasted_iota(jnp.int32, (1, n_cmp_rows), 1)
    cmask = (mrow >= 1) & (mrow * CMP_STRIDE + (CMP_BLOCK - CMP_STRIDE - 1) <= tq)
    sc = jnp.where(cmask, sc, NEG)
    e = jnp.exp(sc - jnp.max(sc, axis=1, keepdims=True))
    p = jnp.where(cmask, e / jnp.sum(e, axis=1, keepdims=True), 0.0)
    o_cmp = jnp.dot(p.astype(BF16), cv_ref[0, 0], preferred_element_type=F32)
    imp = p[0:Q_TILE]
    for r in range(1, Q_PER_KV):
        imp = imp + p[r * Q_TILE:(r + 1) * Q_TILE]

    blk = lax.dot_general(at_ref[...], imp, NT_DIMS, precision=lax.Precision.HIGHEST,
                          preferred_element_type=F32)
    j = lax.broadcasted_iota(jnp.int32, (n_blk_rows, Q_TILE), 0)
    tl = s0 + lax.broadcasted_iota(jnp.int32, (n_blk_rows, Q_TILE), 1)
    cur = tl // SEL_BLOCK
    valid = j * SEL_BLOCK <= tl
    forced = (j == 0) | (j == cur) | (j == cur - 1)
    score = jnp.where(valid & forced, BIG, jnp.where(valid, blk, -1.0))
    cnt = jnp.zeros((n_blk_rows, Q_TILE), F32)
    for i in range(n_blk_rows):
        ri = score[i:i + 1, :]
        beats = (ri > score) | ((ri == score) & (j > i))
        cnt = cnt + jnp.where(beats, 1.0, 0.0)
    sel = (cnt < N_SEL) & valid
    pen_t = jnp.where(sel, 0.0, NEG)
    pen_rows = [jnp.zeros((HEAD_DIM, Q_TILE), F32), pen_t]
    if n_blk_rows < HEAD_DIM:
        pen_rows.append(jnp.zeros((HEAD_DIM - n_blk_rows, Q_TILE), F32))
    pen = jnp.concatenate(pen_rows, axis=0).T
    Qa = Q + jnp.concatenate([pen.astype(BF16)] * Q_PER_KV, axis=0)

    m_sc[...] = jnp.full(m_sc.shape, NEG, F32)
    l_sc[...] = jnp.zeros(l_sc.shape, F32)
    acc_sc[...] = jnp.zeros(acc_sc.shape, F32)

    def slc_tile(kt, causal):
        k0 = pl.multiple_of(kt * SLC_KT, SLC_KT)
        s = lax.dot_general(Qa, kp_ref[0, 0, pl.ds(k0, SLC_KT), :], NT_DIMS, preferred_element_type=F32)
        if causal:
            kpos = k0 + lax.broadcasted_iota(jnp.int32, (1, SLC_KT), 1)
            s = jnp.where(kpos <= tq, s, NEG)
        m_old = m_sc[...]
        m_new = jnp.maximum(m_old, jnp.max(s, axis=1, keepdims=True))
        a = jnp.exp(m_old - m_new)
        pe = jnp.exp(s - m_new)
        l_sc[...] = a * l_sc[...] + jnp.sum(pe, axis=1, keepdims=True)
        acc_sc[...] = a * acc_sc[...] + jnp.dot(pe.astype(BF16), vs_ref[0, 0, pl.ds(k0, SLC_KT), :],
                                                preferred_element_type=F32)
        m_sc[...] = m_new

    last = (s0 + Q_TILE - 1) // SLC_KT

    def body(kt, carry):
        slc_tile(kt, False)
        return carry

    lax.fori_loop(0, last, body, 0)
    slc_tile(last, True)
    o_slc = acc_sc[...] / l_sc[...]

    wlen = WINDOW + Q_TILE
    w0 = pl.multiple_of(jnp.maximum(s0 - WINDOW, 0), Q_TILE)
    s = lax.dot_general(Qa, wk_ref[0, 0, pl.ds(w0, wlen), :], NT_DIMS, preferred_element_type=F32)
    kpos = w0 + lax.broadcasted_iota(jnp.int32, (1, wlen), 1)
    wmask = (kpos <= tq) & (tq - kpos < WINDOW)
    s = jnp.where(wmask, s, NEG)
    e = jnp.exp(s - jnp.max(s, axis=1, keepdims=True))
    pw = jnp.where(wmask, e / jnp.sum(e, axis=1, keepdims=True), 0.0)
    o_win = jnp.dot(pw.astype(BF16), wv_ref[0, 0, pl.ds(w0, wlen), :], preferred_element_type=F32)

    gv = gt_ref[0]
    for r in range(Q_PER_KV):
        rs = slice(r * Q_TILE, (r + 1) * Q_TILE)
        o = (gv[:, 3 * r:3 * r + 1] * o_cmp[rs, 0:HEAD_DIM]
             + gv[:, 3 * r + 1:3 * r + 2] * o_slc[rs, 0:HEAD_DIM]
             + gv[:, 3 * r + 2:3 * r + 3] * o_win[rs, 0:HEAD_DIM])
        o_ref[0, :, r * HEAD_DIM:(r + 1) * HEAD_DIM] = o.astype(BF16)


def _attn_prompt(qh, ck, cv, skp, sv, wk, wv, gates):
    B, _, T, _ = qh.shape
    n_cmp_rows = ck.shape[2]
    n_blocks = T // SEL_BLOCK
    n_blk_rows = max(SUBLANES, n_blocks)
    assert n_blocks <= HEAD_DIM and T >= WINDOW + Q_TILE and T % SLC_KT == 0
    n_cmp = T // CMP_STRIDE - CMP_BLOCK // CMP_STRIDE + 1
    at = jnp.asarray(_block_sum_matrix(n_blk_rows, n_cmp_rows, n_blocks, n_cmp))
    q5 = qh.reshape(B, KV_HEADS, Q_PER_KV, T, LANES)
    R = Q_PER_KV * Q_TILE
    full = lambda n: pl.BlockSpec((1, 1, n, LANES), lambda bi, gi, qi: (bi, gi, 0, 0))
    return pl.pallas_call(
        functools.partial(_attn_prompt_kernel, n_cmp_rows=n_cmp_rows, n_blk_rows=n_blk_rows),
        grid=(B, KV_HEADS, T // Q_TILE),
        in_specs=[
            pl.BlockSpec((1, 1, Q_PER_KV, Q_TILE, LANES), lambda bi, gi, qi: (bi, gi, 0, qi, 0)),
            full(n_cmp_rows), full(n_cmp_rows), full(T), full(T), full(T), full(T),
            pl.BlockSpec((1, Q_TILE, LANES), lambda bi, gi, qi: (bi, qi, gi)),
            pl.BlockSpec((n_blk_rows, n_cmp_rows), lambda bi, gi, qi: (0, 0)),
        ],
        out_specs=pl.BlockSpec((1, Q_TILE, Q_PER_KV * HEAD_DIM), lambda bi, gi, qi: (bi, qi, gi)),
        out_shape=jax.ShapeDtypeStruct((B, T, D_MODEL), BF16),
        scratch_shapes=[pltpu.VMEM((R, 1), F32), pltpu.VMEM((R, 1), F32), pltpu.VMEM((R, LANES), F32)],
        compiler_params=_cparams(("parallel", "parallel", "arbitrary")),
        name="attn_prompt",
    )(q5, ck, cv, skp, sv, wk, wv, gates, at)


def _oproj_ln_kernel(h_ref, o_ref, w_ref, g_ref, b_ref, out_ref):
    y = jnp.dot(o_ref[0], w_ref[...], preferred_element_type=F32)
    out_ref[0] = _layer_norm(ALPHA * h_ref[0] + y, g_ref[...], b_ref[...])


def _oproj_ln(h, o, w_o, g, b, tT):
    B, T, D = h.shape
    row = pl.BlockSpec((1, tT, D), lambda bi, ti: (bi, ti, 0))
    return pl.pallas_call(
        _oproj_ln_kernel,
        grid=(B, T // tT),
        in_specs=[row, row, _const_spec((D, D)), _const_spec((1, D)), _const_spec((1, D))],
        out_specs=row,
        out_shape=jax.ShapeDtypeStruct((B, T, D), F32),
        compiler_params=_cparams(("parallel", "parallel")),
        name="oproj_ln",
    )(h, o, w_o.astype(BF16), g.reshape(1, D), b.reshape(1, D))


def _lane_q(shape):
    return lax.broadcasted_iota(jnp.int32, shape, len(shape) - 1) & 7


def _col(v):
    return jnp.broadcast_to(v, (LANES, LANES)).T


def _diag_heads(o_all):
    g_row = (lax.broadcasted_iota(jnp.int32, (LANES, HEAD_DIM), 0) // 8) & (KV_HEADS - 1)
    out = jnp.zeros((LANES, HEAD_DIM), F32)
    for g in range(KV_HEADS):
        out = out + jnp.where(g_row == g, o_all[:, g * HEAD_DIM:(g + 1) * HEAD_DIM], 0.0)
    return out


def _attn_s1_kernel(ck_ref, cv_ref, qbd_ref, at_ref, ocmp_ref, pen_ref, score_sc, *, past, n_blocks):
    n_rows = ck_ref.shape[1]
    nb_rows = at_ref.shape[0]
    qbd = qbd_ref[0]
    st = jnp.dot(ck_ref[0], qbd, preferred_element_type=F32)
    mrow = lax.broadcasted_iota(jnp.int32, (n_rows, 1), 0)
    tq = past + _lane_q((1, LANES))
    cmask = (mrow >= 1) & (mrow * CMP_STRIDE + (CMP_BLOCK - CMP_STRIDE - 1) <= tq)
    st = jnp.where(cmask, st, NEG)
    e = jnp.exp(st - jnp.max(st, axis=0, keepdims=True))
    p = jnp.where(cmask, e / jnp.sum(e, axis=0, keepdims=True), 0.0)
    o_all = lax.dot_general(p.astype(BF16), cv_ref[0], TN_DIMS, preferred_element_type=F32)
    ocmp_ref[0] = _diag_heads(o_all)
    imp = p
    for r in range(1, Q_PER_KV):
        imp = imp + pltpu.roll(p, r * 32, axis=1)
    blk = jnp.dot(at_ref[...], imp, precision=lax.Precision.HIGHEST, preferred_element_type=F32)
    j = lax.broadcasted_iota(jnp.int32, (nb_rows, LANES), 0)
    tl = past + _lane_q((nb_rows, LANES))
    cur = tl // SEL_BLOCK
    valid = (j * SEL_BLOCK <= tl) & (j < n_blocks)
    forced = (j == 0) | (j == cur) | (j == cur - 1)
    score = jnp.where(valid & forced, BIG, jnp.where(valid, blk, -1.0))
    score_sc[...] = score

    def body(i, cnt):
        ri = score_sc[pl.ds(i, 1), :]
        beats = (ri > score) | ((ri == score) & (j > i))
        return cnt + jnp.where(beats, 1.0, 0.0)

    cnt = lax.fori_loop(0, n_blocks, body, jnp.zeros((nb_rows, LANES), F32))
    sel = (cnt < N_SEL) & valid
    pen_ref[0] = jnp.where(sel, 0.0, NEG)


def _attn_s2_kernel(ptab_ref, *refs, n_pg):
    pg_refs = refs[:n_pg]
    qbd_ref, pen_ref, m_ref, l_ref, acc_ref, s_sc = refs[n_pg:]
    pt = pl.program_id(1)
    qbd = qbd_ref[0]
    half = lax.broadcasted_iota(jnp.int32, (PAGE_SIZE, 1), 0) < SEL_BLOCK
    m = jnp.full((1, LANES), NEG, F32)
    for i, r in enumerate(pg_refs):
        k = r[0, :, 0:KV_W // 2].astype(BF16)
        s = jnp.dot(k, qbd, preferred_element_type=F32)
        b0 = (pt * n_pg + i) * (PAGE_SIZE // SEL_BLOCK)
        s = s + jnp.where(half, pen_ref[0, pl.ds(b0, 1), :], pen_ref[0, pl.ds(b0 + 1, 1), :])
        s_sc[i * PAGE_SIZE:(i + 1) * PAGE_SIZE, :] = s
        m = jnp.maximum(m, jnp.max(s, axis=0, keepdims=True))
    l = jnp.zeros((1, LANES), F32)
    acc = jnp.zeros((LANES, KV_W // 2), F32)
    for i, r in enumerate(pg_refs):
        pe = jnp.exp(s_sc[i * PAGE_SIZE:(i + 1) * PAGE_SIZE, :] - m)
        l = l + jnp.sum(pe, axis=0, keepdims=True)
        v = r[0, :, KV_W // 2:KV_W].astype(BF16)
        acc = acc + lax.dot_general(pe.astype(BF16), v, TN_DIMS, preferred_element_type=F32)
    m_ref[0, 0] = jnp.broadcast_to(m, (SUBLANES, LANES))
    l_ref[0, 0] = jnp.broadcast_to(l, (SUBLANES, LANES))
    acc_ref[0, 0] = acc


def _attn_s3_kernel(m_ref, l_ref, acc_ref, pen_ref, ocmp_ref, qbd_ref, snew_ref, wst_ref, wnew_ref, gt_ref,
                    o_ref, *, n_pt, n_new, tail_blk):
    qbd = qbd_ref[0]
    ql = _lane_q((1, LANES))
    nrow = snew_ref.shape[1]
    irow = lax.broadcasted_iota(jnp.int32, (nrow, 1), 0)
    kw = KV_W // 2

    xs = snew_ref[0]
    st = jnp.dot(xs[:, 0:kw].astype(BF16), qbd, preferred_element_type=F32) + pen_ref[0, tail_blk:tail_blk + 1, :]
    tmask = (irow <= ql) & (irow < n_new)
    st = jnp.where(tmask, st, NEG)
    m_tot = jnp.max(st, axis=0, keepdims=True)
    for s in range(n_pt):
        m_tot = jnp.maximum(m_tot, m_ref[0, s, 0:1, :])
    pt_ = jnp.exp(st - m_tot)
    l_tot = jnp.sum(pt_, axis=0, keepdims=True)
    acc = lax.dot_general(pt_.astype(BF16), xs[:, kw:KV_W].astype(BF16), TN_DIMS, preferred_element_type=F32)
    for s in range(n_pt):
        a = jnp.exp(m_ref[0, s, 0:1, :] - m_tot)
        l_tot = l_tot + a * l_ref[0, s, 0:1, :]
        ac = _col(a)
        acc = acc + jnp.concatenate([ac, ac], axis=1) * acc_ref[0, s]
    lc = _col(l_tot)
    o_slc = _diag_heads(acc / jnp.concatenate([lc, lc], axis=1))

    ws, wn = wst_ref[0], wnew_ref[0]
    nw = ws.shape[0]
    s1 = jnp.dot(ws[:, 0:kw].astype(BF16), qbd, preferred_element_type=F32)
    s2 = jnp.dot(wn[:, 0:kw].astype(BF16), qbd, preferred_element_type=F32)
    i1 = lax.broadcasted_iota(jnp.int32, (nw, 1), 0)
    dist1 = nw + ql - i1
    mask1 = (dist1 >= 0) & (dist1 < WINDOW)
    mask2 = (irow <= ql) & (irow < n_new) & (ql - irow < WINDOW)
    s1 = jnp.where(mask1, s1, NEG)
    s2 = jnp.where(mask2, s2, NEG)
    mw = jnp.maximum(jnp.max(s1, axis=0, keepdims=True), jnp.max(s2, axis=0, keepdims=True))
    p1 = jnp.where(mask1, jnp.exp(s1 - mw), 0.0)
    p2 = jnp.where(mask2, jnp.exp(s2 - mw), 0.0)
    lw = jnp.sum(p1, axis=0, keepdims=True) + jnp.sum(p2, axis=0, keepdims=True)
    inv = 1.0 / lw
    p1 = p1 * inv
    p2 = p2 * inv
    ow = (lax.dot_general(p1.astype(BF16), ws[:, kw:KV_W].astype(BF16), TN_DIMS, preferred_element_type=F32)
          + lax.dot_general(p2.astype(BF16), wn[:, kw:KV_W].astype(BF16), TN_DIMS, preferred_element_type=F32))
    o_win = _diag_heads(ow)

    gv = gt_ref[0]
    o_ref[0] = gv[:, 0:1] * ocmp_ref[0] + gv[:, 1:2] * o_slc + gv[:, 2:3] * o_win


def _attn_sample(q_flat, gates, ck, cv, slc_pool, page_table, slc_new, win_state, win_new, past):
    B, Tn, _ = q_flat.shape
    assert Tn == 8 and past % (PAGE_SIZE * PAGES_PER_STEP) == 0
    n_pages = past // PAGE_SIZE
    n_pt = n_pages // PAGES_PER_STEP
    n_past_blk = past // SEL_BLOCK
    n_blocks = n_past_blk + 1
    nb_rows = -(-n_blocks // SUBLANES) * SUBLANES
    n_cmp = past // CMP_STRIDE - 1
    Mc = ck.shape[1]
    at = jnp.asarray(_block_sum_matrix(nb_rows, Mc, n_blocks, n_cmp))
    kw = KV_W // 2

    q5 = q_flat.reshape(B, Tn, KV_HEADS, Q_PER_KV, HEAD_DIM)
    base = jnp.transpose(q5, (0, 2, 4, 3, 1))
    eye = jnp.eye(KV_HEADS, dtype=q_flat.dtype)
    qbd = (base[:, :, :, :, None, :] * eye[None, :, None, None, :, None]).reshape(B, kw, LANES)
    g5 = gates[:, :, :N_HEADS * 3].reshape(B, Tn, KV_HEADS, Q_PER_KV, 3)
    gt = jnp.pad(jnp.transpose(g5, (0, 3, 2, 1, 4)).reshape(B, LANES, 3), ((0, 0), (0, 0), (0, LANES - 3)))

    per_b = lambda *shape: pl.BlockSpec((1,) + shape, lambda bi, *_: (bi,) + (0,) * len(shape))
    ocmp, pen = pl.pallas_call(
        functools.partial(_attn_s1_kernel, past=past, n_blocks=n_blocks),
        grid=(B,),
        in_specs=[per_b(Mc, kw), per_b(Mc, kw), per_b(kw, LANES),
                  pl.BlockSpec((nb_rows, Mc), lambda bi: (0, 0))],
        out_specs=[per_b(LANES, HEAD_DIM), per_b(nb_rows, LANES)],
        out_shape=[jax.ShapeDtypeStruct((B, LANES, HEAD_DIM), F32), jax.ShapeDtypeStruct((B, nb_rows, LANES), F32)],
        scratch_shapes=[pltpu.VMEM((nb_rows, LANES), F32)],
        compiler_params=_cparams(("parallel",)),
        name="attn_s1",
    )(ck, cv, qbd, at)

    n_pg = PAGES_PER_STEP
    pool3 = slc_pool.reshape(slc_pool.shape[0], PAGE_SIZE, KV_W)
    pg_specs = [pl.BlockSpec((1, PAGE_SIZE, KV_W),
                             functools.partial(lambda bi, pi, ptab, i: (ptab[bi, pi * n_pg + i], 0, 0), i=i))
                for i in range(n_pg)]
    part = lambda *shape: pl.BlockSpec((1, 1) + shape, lambda bi, pi, ptab: (bi, pi) + (0,) * len(shape))
    m_p, l_p, acc_p = pl.pallas_call(
        functools.partial(_attn_s2_kernel, n_pg=n_pg),
        grid_spec=pltpu.PrefetchScalarGridSpec(
            num_scalar_prefetch=1, grid=(B, n_pt),
            in_specs=pg_specs + [pl.BlockSpec((1, kw, LANES), lambda bi, pi, ptab: (bi, 0, 0)),
                                 pl.BlockSpec((1, nb_rows, LANES), lambda bi, pi, ptab: (bi, 0, 0))],
            out_specs=[part(SUBLANES, LANES), part(SUBLANES, LANES), part(LANES, kw)],
            scratch_shapes=[pltpu.VMEM((n_pg * PAGE_SIZE, LANES), F32)],
        ),
        out_shape=[jax.ShapeDtypeStruct((B, n_pt, SUBLANES, LANES), F32)] * 2
        + [jax.ShapeDtypeStruct((B, n_pt, LANES, kw), F32)],
        compiler_params=_cparams(("parallel", "parallel")),
        name="attn_s2",
    )(page_table, *([pool3] * n_pg), qbd, pen)

    pad_rows = 16 - Tn
    snew = jnp.pad(slc_new, ((0, 0), (0, pad_rows), (0, 0)))
    wnew = jnp.pad(win_new, ((0, 0), (0, pad_rows), (0, 0)))
    nw = win_state.shape[1]
    o = pl.pallas_call(
        functools.partial(_attn_s3_kernel, n_pt=n_pt, n_new=Tn, tail_blk=n_past_blk),
        grid=(B,),
        in_specs=[per_b(n_pt, SUBLANES, LANES), per_b(n_pt, SUBLANES, LANES), per_b(n_pt, LANES, kw),
                  per_b(nb_rows, LANES), per_b(LANES, HEAD_DIM), per_b(kw, LANES),
                  per_b(16, KV_W), per_b(nw, KV_W), per_b(16, KV_W), per_b(LANES, LANES)],
        out_specs=per_b(LANES, HEAD_DIM),
        out_shape=jax.ShapeDtypeStruct((B, LANES, HEAD_DIM), F32),
        compiler_params=_cparams(("parallel",)),
        name="attn_s3",
    )(m_p, l_p, acc_p, pen, ocmp, qbd, snew, win_state, wnew, gt)
    o = jnp.transpose(o.reshape(B, Q_PER_KV, KV_HEADS, Tn, HEAD_DIM), (0, 3, 2, 1, 4))
    return o.reshape(B, Tn, D_MODEL).astype(BF16)


def _nsa_prompt_layer(h, w_in, w_o, cw, g, b, tT):
    B, T, _ = h.shape
    cmp_kv, slc_kv, win_kv, qh, skp, sv, wk, wv, gates = _nsa_proj(h, jnp.arange(T), w_in, tT, True)
    n_half = T // CMP_STRIDE
    x = cmp_kv.reshape(B, n_half, CMP_STRIDE * KV_W)
    ck, cv = _compress_call(
        [x], [pl.BlockSpec((1, n_half, CMP_STRIDE * KV_W), lambda bi, pi: (bi, 0, 0))],
        (B, 1), n_half, n_half, cw, True, B)
    o = _attn_prompt(qh, ck, cv, skp, sv, wk, wv, gates)
    h2 = _oproj_ln(h, o, w_o, g, b, tT)
    win_len = min(WINDOW, T)
    return h2, cmp_kv, slc_kv, win_kv[:, T - win_len:]


def _nsa_sample_layer(h, cmp_pool, slc_pool, win_buf, page_table, w_in, w_o, cw, g, b):
    B, Tn, _ = h.shape
    n_pages = page_table.shape[1]
    past = n_pages * PAGE_SIZE
    cmp_kv, slc_kv, win_kv, q_flat, gates = _nsa_proj(h, past + jnp.arange(Tn), w_in, Tn, False)
    assert Tn < CMP_STRIDE
    hp = PAGE_SIZE // CMP_STRIDE
    pool = cmp_pool.reshape(cmp_pool.shape[0], hp, CMP_STRIDE * KV_W)
    n_pg = PAGES_PER_STEP
    specs = [pl.BlockSpec((1, hp, CMP_STRIDE * KV_W),
                          functools.partial(lambda bi, pi, ptab, i: (ptab[bi, pi * n_pg + i], 0, 0), i=i))
             for i in range(n_pg)]
    ck, cv = _compress_call([pool] * n_pg, specs, (B, n_pages // n_pg), n_pages * hp, n_pg * hp, cw, False, B,
                            extra_prefetch=page_table)
    win_state = win_buf.reshape(B, win_buf.shape[1], KV_W)
    o = _attn_sample(q_flat, gates, ck, cv, slc_pool, page_table, slc_kv, win_state, win_kv, past)
    h2 = _oproj_ln(h, o, w_o, g, b, Tn)
    buf_len = win_buf.shape[1]
    new_win = jnp.concatenate([win_state, win_kv], axis=1)[:, -buf_len:]
    return h2, cmp_kv, slc_kv, new_win


def _row_tile(T):
    return 512 if T % 512 == 0 else T


def kernel(x_prompt, x_sample, state_pool, cache_cmp_kv, cache_slc_kv, state_win_kv, state_ffn, page_table, ln_g, ln_b, pool_w, pool_scale, nsa_w_in, nsa_w_o, cmp_w1, cmp_b1, cmp_pos, cmp_w2, cmp_b2, ffn_w_up, ffn_conv_w, ffn_conv_b, ffn_w_down):
    Bp, T, D = x_prompt.shape
    Bs, Tn, _ = x_sample.shape
    past = page_table.shape[1] * PAGE_SIZE
    kv_shape = (2, KV_HEADS, HEAD_DIM)
    tp, ts = _row_tile(T), Tn

    hp = _pool_ln(x_prompt, jnp.zeros((Bp, POOL_BUF, D), F32), 0, pool_w[0], pool_scale[0], ln_g[0, 0], ln_b[0, 0], tp)
    hs = _pool_ln(x_sample, state_pool[0], past, pool_w[0], pool_scale[0], ln_g[0, 0], ln_b[0, 0], ts)
    pool_p = jnp.concatenate([jnp.zeros((Bp, POOL_BUF, D), F32), x_prompt], axis=1)[:, -POOL_BUF:][None]
    pool_s = jnp.concatenate([state_pool[0], x_sample], axis=1)[:, -POOL_BUF:][None]
    ffn = lambda h, pre, i, t: _ffn_ln(h, pre, ffn_w_up[i], ffn_conv_w[i], ffn_conv_b[i], ffn_w_down[i],
                                       ln_g[i, 1], ln_b[i, 1], t)
    zero_pre = jnp.zeros((Bp, CONV_W - 1, 2 * D_FF), F32)
    hp, ffn_p0 = ffn(hp, zero_pre, 0, tp)
    hs, ffn_s0 = ffn(hs, state_ffn[0], 0, ts)

    cw = (cmp_w1[0], cmp_b1[0], cmp_pos[0], cmp_w2[0], cmp_b2[0])
    hp, c_p, s_p, w_p = _nsa_prompt_layer(hp, nsa_w_in[0], nsa_w_o[0], cw, ln_g[1, 0], ln_b[1, 0], tp)
    hs, c_s, s_s, w_s = _nsa_sample_layer(hs, cache_cmp_kv[0], cache_slc_kv[0], state_win_kv[0], page_table,
                                          nsa_w_in[0], nsa_w_o[0], cw, ln_g[1, 0], ln_b[1, 0])
    hp, ffn_p1 = ffn(hp, zero_pre, 1, tp)
    hs, ffn_s1 = ffn(hs, state_ffn[1], 1, ts)

    kv5 = lambda a: a.reshape(a.shape[0], a.shape[1], *kv_shape)[None]
    return (hp, hs, pool_p, pool_s, kv5(c_p), kv5(c_s), kv5(s_p), kv5(s_s), kv5(w_p), kv5(w_s),
            jnp.stack([ffn_p0, ffn_p1]), jnp.stack([ffn_s0, ffn_s1]))
```

```python
import functools

import numpy as np
import jax
import jax.numpy as jnp
from jax import lax
from jax.experimental import pallas as pl
from jax.experimental.pallas import tpu as pltpu

D_MODEL = 1024
DEPTH = 2
ALPHA = (2.0 * DEPTH) ** 0.25
LN_EPS = 1e-5
POOL_WINDOWS = (2, 4, 8, 16)
POOL_CH = D_MODEL // len(POOL_WINDOWS)
POOL_BUF = max(POOL_WINDOWS) - 1
N_HEADS = 16
KV_HEADS = 4
HEAD_DIM = 64
Q_PER_KV = N_HEADS // KV_HEADS
ROT_DIM = HEAD_DIM // 4
ROPE_THETA = 500000.0
CMP_BLOCK = 32
CMP_STRIDE = 16
CMP_HID = 128
SEL_BLOCK = 64
N_SEL = 16
WINDOW = 512
KV_W = 2 * KV_HEADS * HEAD_DIM
D_FF = 2816
CONV_W = 3
PAGE_SIZE = 128
NEG = -1e30
BIG = 1e9
LOG2E = 1.4426950408889634

LANES = 128
SUBLANES = 8
Q_TILE = 128
SLC_KT = 512
PAGES_PER_STEP = 16
VMEM_LIMIT = 56 * 1024 * 1024

F32 = jnp.float32
BF16 = jnp.bfloat16
NT_DIMS = (((1,), (1,)), ((), ()))


def _cparams(sem):
    return pltpu.CompilerParams(dimension_semantics=sem, vmem_limit_bytes=VMEM_LIMIT)


def _layer_norm(h, g, b):
    mu = jnp.mean(h, axis=-1, keepdims=True)
    hc = h - mu
    var = jnp.mean(hc * hc, axis=-1, keepdims=True)
    return hc * lax.rsqrt(var + LN_EPS) * g + b


def _const_spec(shape):
    nd = len(shape)
    return pl.BlockSpec(shape, lambda *_: (0,) * nd, pipeline_mode=pl.Buffered(1))


def _pool_ln_kernel(x_ref, halo_ref, pre_ref, w_ref, sc_ref, g_ref, b_ref, o_ref, buf, *, tT, pos0):
    t = pl.program_id(1)
    x = x_ref[0]
    buf[0:16, :] = jnp.where(t == 0, pre_ref[0], halo_ref[0])
    buf[16:16 + tT, :] = x
    pos = pos0 + t * tT + lax.broadcasted_iota(jnp.int32, (tT, 1), 0)
    ys = []
    for gi, win in enumerate(POOL_WINDOWS):
        c0 = gi * POOL_CH
        xg = x[:, c0:c0 + POOL_CH]
        s = xg
        for i in range(1, win):
            s = s + buf[pl.ds(16 - i, tT), c0:c0 + POOL_CH]
        cnt = jnp.minimum(pos + 1, win).astype(F32)
        d = s / cnt - xg
        ys.append(jnp.dot(d.astype(BF16), w_ref[gi], preferred_element_type=F32))
    y = jnp.concatenate(ys, axis=1) * sc_ref[...]
    o_ref[0] = _layer_norm(ALPHA * x + y, g_ref[...], b_ref[...])


def _pool_ln(x, prefix, pos0, w, scale, g, b, tT):
    B, T, D = x.shape
    nT = T // tT
    pre16 = jnp.concatenate([jnp.zeros((B, 1, D), x.dtype), prefix], axis=1)
    if T >= 16:
        halo_src = x
        per = tT // 16
        halo_spec = pl.BlockSpec((1, 16, D), lambda bi, ti: (bi, jnp.maximum(ti * per - 1, 0), 0))
    else:
        halo_src = pre16
        halo_spec = pl.BlockSpec((1, 16, D), lambda bi, ti: (bi, 0, 0))
    return pl.pallas_call(
        functools.partial(_pool_ln_kernel, tT=tT, pos0=pos0),
        grid=(B, nT),
        in_specs=[
            pl.BlockSpec((1, tT, D), lambda bi, ti: (bi, ti, 0)),
            halo_spec,
            pl.BlockSpec((1, 16, D), lambda bi, ti: (bi, 0, 0)),
            _const_spec((len(POOL_WINDOWS), POOL_CH, POOL_CH)),
            _const_spec((1, D)), _const_spec((1, D)), _const_spec((1, D)),
        ],
        out_specs=pl.BlockSpec((1, tT, D), lambda bi, ti: (bi, ti, 0)),
        out_shape=jax.ShapeDtypeStruct((B, T, D), F32),
        scratch_shapes=[pltpu.VMEM((16 + tT, D), F32)],
        compiler_params=_cparams(("parallel", "parallel")),
        name="pool_ln",
    )(x, halo_src, pre16, w.astype(BF16), scale.reshape(1, D), g.reshape(1, D), b.reshape(1, D))


FFN_CHUNK = 256


def _ffn_ln_kernel(h_ref, pre_ref, wup_ref, cw_ref, cb_ref, wdn_ref, g_ref, b_ref, o_ref, st_ref,
                   ubuf, act, *, tT):
    t = pl.program_id(1)

    @pl.when(t == 0)
    def _():
        st_ref[0] = pre_ref[0]

    h = h_ref[0]
    hb = h.astype(BF16)
    fc = FFN_CHUNK
    for j in range(D_FF // fc):
        cs = []
        for half in range(2):
            c0 = half * D_FF + j * fc
            up = jnp.dot(hb, wup_ref[:, c0:c0 + fc], preferred_element_type=F32)
            ubuf[half, 0:8, :] = st_ref[0, :, c0:c0 + fc]
            ubuf[half, 8:8 + tT, :] = up
            st_ref[0, :, c0:c0 + fc] = ubuf[half, tT:tT + 8, :]
            c = (cb_ref[:, c0:c0 + fc]
                 + cw_ref[0:1, c0:c0 + fc] * ubuf[half, pl.ds(6, tT), :]
                 + cw_ref[1:2, c0:c0 + fc] * ubuf[half, pl.ds(7, tT), :]
                 + cw_ref[2:3, c0:c0 + fc] * up)
            cs.append(c)
        a = cs[0] * jax.nn.sigmoid(cs[0]) * cs[1]
        act[:, j * fc:(j + 1) * fc] = a.astype(BF16)
    y = jnp.dot(act[...], wdn_ref[...], preferred_element_type=F32)
    o_ref[0] = _layer_norm(ALPHA * h + y, g_ref[...], b_ref[...])


def _ffn_ln(h, prefix, w_up, conv_w, conv_b, w_down, g, b, tT):
    B, T, D = h.shape
    nT = T // tT
    F2 = 2 * D_FF
    pre8 = jnp.concatenate([jnp.zeros((B, 8 - (CONV_W - 1), F2), F32), prefix], axis=1)
    out, st = pl.pallas_call(
        functools.partial(_ffn_ln_kernel, tT=tT),
        grid=(B, nT),
        in_specs=[
            pl.BlockSpec((1, tT, D), lambda bi, ti: (bi, ti, 0)),
            pl.BlockSpec((1, 8, F2), lambda bi, ti: (bi, 0, 0)),
            _const_spec((D, F2)),
            _const_spec((CONV_W, F2)),
            _const_spec((1, F2)),
            _const_spec((D_FF, D)),
            _const_spec((1, D)), _const_spec((1, D)),
        ],
        out_specs=[pl.BlockSpec((1, tT, D), lambda bi, ti: (bi, ti, 0)),
                   pl.BlockSpec((1, 8, F2), lambda bi, ti: (bi, 0, 0))],
        out_shape=[jax.ShapeDtypeStruct((B, T, D), F32), jax.ShapeDtypeStruct((B, 8, F2), F32)],
        scratch_shapes=[pltpu.VMEM((2, tT + 8, FFN_CHUNK), F32), pltpu.VMEM((tT, D_FF), BF16)],
        compiler_params=_cparams(("parallel", "arbitrary")),
        name="ffn_ln",
    )(h, pre8, w_up.astype(BF16), conv_w, conv_b.reshape(1, F2), w_down.astype(BF16),
      g.reshape(1, D), b.reshape(1, D))
    return out, st[:, 8 - (CONV_W - 1):, :]


def _rope_tables(pos):
    half = ROT_DIM // 2
    inv = jnp.power(ROPE_THETA, -2.0 * jnp.arange(half, dtype=F32) / ROT_DIM)
    ang = pos.astype(F32)[:, None] * inv[None, :]
    cos, sin = jnp.cos(ang), jnp.sin(ang)
    T = pos.shape[0]
    one, zero = jnp.ones((T, HEAD_DIM - ROT_DIM), F32), jnp.zeros((T, HEAD_DIM - half), F32)
    c64 = jnp.concatenate([cos, cos, one], axis=1)
    lo64 = jnp.concatenate([-sin, zero], axis=1)
    hi64 = jnp.concatenate([jnp.zeros((T, half), F32), sin, jnp.zeros((T, HEAD_DIM - ROT_DIM), F32)], axis=1)
    tile = lambda a: jnp.concatenate([a, a], axis=1)
    return tile(c64), tile(lo64), tile(hi64)


def _rope_chunk(x, c, lo, hi):
    return x * c + pltpu.roll(x, LANES - ROT_DIM // 2, axis=1) * lo + pltpu.roll(x, ROT_DIM // 2, axis=1) * hi


def _nsa_proj_kernel(h_ref, wq_ref, wkv_ref, wg_ref, c_ref, lo_ref, hi_ref, *outs, tT, head_major):
    t = pl.program_id(1)
    hb = h_ref[0].astype(BF16)
    c, lo, hi = c_ref[...], lo_ref[...], hi_ref[...]
    q = jnp.dot(hb, wq_ref[...], preferred_element_type=F32)
    kv = jnp.dot(hb, wkv_ref[...], preferred_element_type=F32)
    gl = jnp.dot(hb, wg_ref[...], preferred_element_type=F32)
    gates = jax.nn.sigmoid(gl)
    lane = lax.broadcasted_iota(jnp.int32, (tT, LANES), 1)
    low = lane < HEAD_DIM
    scale = HEAD_DIM ** -0.5 * (LOG2E if head_major else 1.0)
    qc = [_rope_chunk(q[:, i * LANES:(i + 1) * LANES], c, lo, hi) * scale for i in range(D_MODEL // LANES)]
    kvc = []
    for br in range(3):
        for i in range(KV_W // LANES):
            x = kv[:, br * KV_W + i * LANES: br * KV_W + (i + 1) * LANES]
            kvc.append(_rope_chunk(x, c, lo, hi) if i < KV_W // (2 * LANES) else x)
    if head_major:
        cmp_ref, slc_ref, win_ref, qh_ref, skp_ref, sv_ref, wk_ref, wv_ref, g_ref = outs
    else:
        cmp_ref, slc_ref, win_ref, qf_ref, g_ref = outs
    for br, ref in enumerate((cmp_ref, slc_ref, win_ref)):
        ref[0] = jnp.concatenate(kvc[br * 4:(br + 1) * 4], axis=1)
    g_ref[0] = gates
    if not head_major:
        qf_ref[0] = jnp.concatenate(qc, axis=1).astype(BF16)
        return

    def split(x):
        return jnp.where(low, x, 0.0), jnp.where(low, pltpu.roll(x, HEAD_DIM, axis=1), 0.0)

    for i in range(D_MODEL // LANES):
        a, b = split(qc[i])
        qh_ref[0, 2 * i] = a.astype(BF16)
        qh_ref[0, 2 * i + 1] = b.astype(BF16)
    blk = (t * tT + lax.broadcasted_iota(jnp.int32, (tT, LANES), 0)) // SEL_BLOCK
    onehot = jnp.where(lane - HEAD_DIM == blk, 1.0, 0.0)
    ones_col = jnp.where(lane == HEAD_DIM, 1.0, 0.0)
    for br, (kref, vref) in ((1, (skp_ref, sv_ref)), (2, (wk_ref, wv_ref))):
        for i in range(2):
            ka, kb = split(kvc[br * 4 + i])
            va, vb = split(kvc[br * 4 + 2 + i])
            if br == 1:
                ka, kb = ka + onehot, kb + onehot
            va, vb = va + ones_col, vb + ones_col
            kref[0, 2 * i] = ka.astype(BF16)
            kref[0, 2 * i + 1] = kb.astype(BF16)
            vref[0, 2 * i] = va.astype(BF16)
            vref[0, 2 * i + 1] = vb.astype(BF16)


def _nsa_proj(h, pos, w_in, tT, head_major):
    B, T, D = h.shape
    nT = T // tT
    nq = N_HEADS * HEAD_DIM
    wq = w_in[:, :nq].astype(BF16)
    wkv = w_in[:, nq:nq + 3 * KV_W].astype(BF16)
    wg = w_in[:, nq + 3 * KV_W:]
    if head_major:
        GW = KV_HEADS * LANES
        wg = jnp.pad(wg.reshape(D, KV_HEADS, Q_PER_KV * 3), ((0, 0), (0, 0), (0, LANES - Q_PER_KV * 3))).reshape(D, GW)
    else:
        GW = LANES
        wg = jnp.pad(wg, ((0, 0), (0, LANES - wg.shape[1])))
    wg = wg.astype(BF16)
    c, lo, hi = _rope_tables(pos)
    row = lambda w: pl.BlockSpec((1, tT, w), lambda bi, ti: (bi, ti, 0))
    hm = lambda n: pl.BlockSpec((1, n, tT, LANES), lambda bi, ti: (bi, 0, ti, 0))
    tab = pl.BlockSpec((tT, LANES), lambda bi, ti: (ti, 0))
    out_specs = [row(KV_W), row(KV_W), row(KV_W)]
    out_shape = [jax.ShapeDtypeStruct((B, T, KV_W), F32)] * 3
    if head_major:
        out_specs += [hm(N_HEADS)] + [hm(KV_HEADS)] * 4 + [row(GW)]
        out_shape += ([jax.ShapeDtypeStruct((B, N_HEADS, T, LANES), BF16)]
                      + [jax.ShapeDtypeStruct((B, KV_HEADS, T, LANES), BF16)] * 4
                      + [jax.ShapeDtypeStruct((B, T, GW), F32)])
    else:
        out_specs += [row(D), row(GW)]
        out_shape += [jax.ShapeDtypeStruct((B, T, D), BF16), jax.ShapeDtypeStruct((B, T, GW), F32)]
    return pl.pallas_call(
        functools.partial(_nsa_proj_kernel, tT=tT, head_major=head_major),
        grid=(B, nT),
        in_specs=[row(D), _const_spec((D, nq)), _const_spec((D, 3 * KV_W)), _const_spec((D, GW)), tab, tab, tab],
        out_specs=out_specs,
        out_shape=out_shape,
        compiler_params=_cparams(("parallel", "parallel")),
        name="nsa_proj",
    )(h, wq, wkv, wg, c, lo, hi)


def _gelu_tanh(x):
    return 0.5 * x * (1.0 + jnp.tanh(0.7978845608028654 * (x + 0.044715 * x * x * x)))


def _compress_kernel(x_ref, w1_ref, w1f_ref, pos_ref, b1_ref, w2_ref, b2_ref, ck_ref, cv_ref):
    for kvi, out_ref in enumerate((ck_ref, cv_ref)):
        b1 = b1_ref[kvi:kvi + 1, :] + jnp.dot(pos_ref[kvi], w1f_ref[kvi], preferred_element_type=F32)[0:1, :]
        for g in range(KV_HEADS):
            acc = None
            for t in range(CMP_STRIDE):
                off = t * KV_W + kvi * (KV_W // 2) + g * HEAD_DIM
                d = jnp.dot(x_ref[0, :, off:off + HEAD_DIM].astype(BF16), w1_ref[kvi, t],
                            preferred_element_type=F32)
                acc = d if acc is None else acc + d
            pr0, pr1 = acc[:, :CMP_HID], acc[:, CMP_HID:]
            hid = _gelu_tanh(pltpu.roll(pr0, 1, axis=0) + pr1 + b1)
            o = jnp.dot(hid.astype(BF16), w2_ref[kvi], preferred_element_type=F32) + b2_ref[kvi:kvi + 1, :]
            out_ref[0, g] = jnp.concatenate([o, jnp.zeros_like(o)], axis=1).astype(BF16)


def _compress_weights(cw):
    w1, b1, pos_emb, w2, b2 = cw
    assert CMP_BLOCK == 2 * CMP_STRIDE
    w1cat = jnp.concatenate([w1[:, :CMP_STRIDE], w1[:, CMP_STRIDE:]], axis=-1).astype(BF16)
    w1f = w1.reshape(2, CMP_BLOCK * HEAD_DIM, CMP_HID).astype(BF16)
    posf = jnp.broadcast_to(pos_emb.reshape(2, 1, CMP_BLOCK * HEAD_DIM), (2, SUBLANES, CMP_BLOCK * HEAD_DIM)).astype(BF16)
    return w1cat, w1f, posf, b1, w2.astype(BF16), b2


def _compress_prompt(cmp_kv, cw):
    B, T, _ = cmp_kv.shape
    n_half = T // CMP_STRIDE
    x = cmp_kv.reshape(B, n_half, CMP_STRIDE * KV_W)
    ws = _compress_weights(cw)
    o_spec = pl.BlockSpec((1, KV_HEADS, n_half, LANES), lambda bi: (bi, 0, 0, 0))
    o_shape = jax.ShapeDtypeStruct((B, KV_HEADS, n_half, LANES), BF16)
    return pl.pallas_call(
        _compress_kernel,
        grid=(B,),
        in_specs=[pl.BlockSpec((1, n_half, CMP_STRIDE * KV_W), lambda bi: (bi, 0, 0))]
        + [_const_spec(w.shape) for w in ws],
        out_specs=[o_spec, o_spec],
        out_shape=[o_shape, o_shape],
        compiler_params=_cparams(("parallel",)),
        name="compress_hm",
    )(x, *ws)


def _block_sum_matrix(n_out, n_in, n_blocks, n_cmp):
    a = np.zeros((n_out, n_in), np.float32)
    for j in range(n_blocks):
        for m in range(4 * j, 4 * j + 5):
            if 1 <= m <= n_cmp:
                a[j, m] = 1.0
    return a


def _attn_prompt_kernel(q_ref, ck_ref, cv_ref, kp_ref, vs_ref, wk_ref, wv_ref, gt_ref, at_ref, o_ref,
                        s_buf, m_run, acc_sc, pen_sc, *, n_cmp_rows, n_blk_rows):
    qt = pl.program_id(2)
    s0 = qt * Q_TILE
    R = Q_PER_KV * Q_TILE
    Q = q_ref[0, 0].reshape(R, LANES)
    tq = s0 + (lax.broadcasted_iota(jnp.int32, (R, 1), 0) & (Q_TILE - 1))

    sc = lax.dot_general(Q, ck_ref[0, 0], NT_DIMS, preferred_element_type=F32)
    mrow = lax.broadcasted_iota(jnp.int32, (1, n_cmp_rows), 1)
    cend = jnp.where(mrow >= 1, mrow * CMP_STRIDE + (CMP_BLOCK - CMP_STRIDE - 1), jnp.int32(2 ** 30))
    sc = jnp.where(cend <= tq, sc, NEG)
    e = jnp.exp2(sc - jnp.max(sc, axis=1, keepdims=True))
    inv = jnp.where(tq >= CMP_BLOCK - 1, 1.0 / jnp.sum(e, axis=1, keepdims=True), 0.0)
    p = e * inv
    o_cmp = jnp.dot(p.astype(BF16), cv_ref[0, 0], preferred_element_type=F32)
    imp = p[0:Q_TILE]
    for r in range(1, Q_PER_KV):
        imp = imp + p[r * Q_TILE:(r + 1) * Q_TILE]

    blk = lax.dot_general(at_ref[...], imp, NT_DIMS, precision=lax.Precision.HIGHEST,
                          preferred_element_type=F32)
    j = lax.broadcasted_iota(jnp.int32, (n_blk_rows, Q_TILE), 0)
    tl = s0 + lax.broadcasted_iota(jnp.int32, (n_blk_rows, Q_TILE), 1)
    cur = tl // SEL_BLOCK
    valid = j * SEL_BLOCK <= tl
    forced = (j == 0) | (j == cur) | (j == cur - 1)
    score = jnp.where(valid & forced, BIG, jnp.where(valid, blk, -1.0))

    all_fit = qt < (N_SEL * SEL_BLOCK) // Q_TILE

    @pl.when(all_fit)
    def _():
        pen_sc[...] = jnp.where(valid, 0.0, NEG)

    @pl.when(jnp.logical_not(all_fit))
    def _():
        nv = n_blk_rows // SUBLANES
        sv = [score[v * SUBLANES:(v + 1) * SUBLANES, :] for v in range(nv)]
        jv = lax.broadcasted_iota(jnp.int32, (SUBLANES, Q_TILE), 0)
        cnt = [jnp.zeros((SUBLANES, Q_TILE), F32) for _ in range(nv)]
        for i in range(n_blk_rows):
            vi, si = divmod(i, SUBLANES)
            ri = sv[vi][si:si + 1, :]
            for v in range(nv):
                if v < vi:
                    beats = ri > sv[v]
                elif v > vi:
                    beats = ri >= sv[v]
                else:
                    beats = (ri > sv[v]) | ((jv > si) & (ri >= sv[v]))
                cnt[v] = cnt[v] + jnp.where(beats, 1.0, 0.0)
        sel = (jnp.concatenate(cnt, axis=0) < N_SEL) & valid
        pen_sc[...] = jnp.where(sel, 0.0, NEG)

    pen_rows = [jnp.zeros((HEAD_DIM, Q_TILE), F32), pen_sc[...]]
    if n_blk_rows < HEAD_DIM:
        pen_rows.append(jnp.zeros((HEAD_DIM - n_blk_rows, Q_TILE), F32))
    pen = jnp.concatenate(pen_rows, axis=0).T
    Qa = Q + jnp.concatenate([pen.astype(BF16)] * Q_PER_KV, axis=0)

    n_full = (s0 // SLC_KT)
    m_run[...] = jnp.full(m_run.shape, NEG, F32)

    def scores(kt, causal):
        k0 = pl.multiple_of(kt * SLC_KT, SLC_KT)
        s = lax.dot_general(Qa, kp_ref[0, 0, pl.ds(k0, SLC_KT), :], NT_DIMS, preferred_element_type=F32)
        if causal:
            kpos = k0 + lax.broadcasted_iota(jnp.int32, (1, SLC_KT), 1)
            s = jnp.where(kpos <= tq, s, NEG)
        s_buf[kt] = s
        mx = m_run[...]
        for c in range(SLC_KT // LANES):
            mx = jnp.maximum(mx, s[:, c * LANES:(c + 1) * LANES])
        m_run[...] = mx

    def p1(kt, carry):
        scores(kt, False)
        return carry

    lax.fori_loop(0, n_full, p1, 0)
    scores(n_full, True)
    m = jnp.max(m_run[...], axis=1, keepdims=True)
    acc_sc[...] = jnp.zeros(acc_sc.shape, F32)

    def p2(kt, carry):
        k0 = pl.multiple_of(kt * SLC_KT, SLC_KT)
        pe = jnp.exp2(s_buf[kt] - m).astype(BF16)
        acc_sc[...] += jnp.dot(pe, vs_ref[0, 0, pl.ds(k0, SLC_KT), :], preferred_element_type=F32)
        return carry

    lax.fori_loop(0, n_full + 1, p2, 0)
    acc_s = acc_sc[...]

    wlen = WINDOW + Q_TILE
    w0 = pl.multiple_of(jnp.maximum(s0 - WINDOW, 0), Q_TILE)
    s = lax.dot_general(Qa, wk_ref[0, 0, pl.ds(w0, wlen), :], NT_DIMS, preferred_element_type=F32)
    kpos = w0 + lax.broadcasted_iota(jnp.int32, (1, wlen), 1)
    s = jnp.where((kpos <= tq) & (kpos > tq - WINDOW), s, NEG)
    pw = jnp.exp2(s - jnp.max(s, axis=1, keepdims=True)).astype(BF16)
    acc_w = jnp.dot(pw, wv_ref[0, 0, pl.ds(w0, wlen), :], preferred_element_type=F32)

    gv = gt_ref[0]
    for r in range(Q_PER_KV):
        rs = slice(r * Q_TILE, (r + 1) * Q_TILE)
        g_s = gv[:, 3 * r + 1:3 * r + 2] / acc_s[rs, HEAD_DIM:HEAD_DIM + 1]
        g_w = gv[:, 3 * r + 2:3 * r + 3] / acc_w[rs, HEAD_DIM:HEAD_DIM + 1]
        o = (gv[:, 3 * r:3 * r + 1] * o_cmp[rs, 0:HEAD_DIM] + g_s * acc_s[rs, 0:HEAD_DIM]
             + g_w * acc_w[rs, 0:HEAD_DIM])
        o_ref[0, :, r * HEAD_DIM:(r + 1) * HEAD_DIM] = o.astype(BF16)


def _attn_prompt(qh, ck, cv, skp, sv, wk, wv, gates):
    B, _, T, _ = qh.shape
    n_cmp_rows = ck.shape[2]
    n_blocks = T // SEL_BLOCK
    n_blk_rows = max(SUBLANES, n_blocks)
    assert n_blocks <= HEAD_DIM and T >= WINDOW + Q_TILE and T % SLC_KT == 0
    n_cmp = T // CMP_STRIDE - CMP_BLOCK // CMP_STRIDE + 1
    at = jnp.asarray(_block_sum_matrix(n_blk_rows, n_cmp_rows, n_blocks, n_cmp))
    q5 = qh.reshape(B, KV_HEADS, Q_PER_KV, T, LANES)
    R = Q_PER_KV * Q_TILE
    full = lambda n: pl.BlockSpec((1, 1, n, LANES), lambda bi, gi, qi: (bi, gi, 0, 0))
    return pl.pallas_call(
        functools.partial(_attn_prompt_kernel, n_cmp_rows=n_cmp_rows, n_blk_rows=n_blk_rows),
        grid=(B, KV_HEADS, T // Q_TILE),
        in_specs=[
            pl.BlockSpec((1, 1, Q_PER_KV, Q_TILE, LANES), lambda bi, gi, qi: (bi, gi, 0, qi, 0)),
            full(n_cmp_rows), full(n_cmp_rows), full(T), full(T), full(T), full(T),
            pl.BlockSpec((1, Q_TILE, LANES), lambda bi, gi, qi: (bi, qi, gi)),
            pl.BlockSpec((n_blk_rows, n_cmp_rows), lambda bi, gi, qi: (0, 0)),
        ],
        out_specs=pl.BlockSpec((1, Q_TILE, Q_PER_KV * HEAD_DIM), lambda bi, gi, qi: (bi, qi, gi)),
        out_shape=jax.ShapeDtypeStruct((B, T, D_MODEL), BF16),
        scratch_shapes=[pltpu.VMEM((T // SLC_KT, R, SLC_KT), F32), pltpu.VMEM((R, LANES), F32),
                        pltpu.VMEM((R, LANES), F32), pltpu.VMEM((n_blk_rows, Q_TILE), F32)],
        compiler_params=_cparams(("parallel", "parallel", "arbitrary")),
        name="attn_prompt",
    )(q5, ck, cv, skp, sv, wk, wv, gates, at)


def _oproj_ln_kernel(h_ref, o_ref, w_ref, g_ref, b_ref, out_ref):
    y = jnp.dot(o_ref[0], w_ref[...], preferred_element_type=F32)
    out_ref[0] = _layer_norm(ALPHA * h_ref[0] + y, g_ref[...], b_ref[...])


def _oproj_ln(h, o, w_o, g, b, tT):
    B, T, D = h.shape
    row = pl.BlockSpec((1, tT, D), lambda bi, ti: (bi, ti, 0))
    return pl.pallas_call(
        _oproj_ln_kernel,
        grid=(B, T // tT),
        in_specs=[row, row, _const_spec((D, D)), _const_spec((1, D)), _const_spec((1, D))],
        out_specs=row,
        out_shape=jax.ShapeDtypeStruct((B, T, D), F32),
        compiler_params=_cparams(("parallel", "parallel")),
        name="oproj_ln",
    )(h, o, w_o.astype(BF16), g.reshape(1, D), b.reshape(1, D))


PEN_TILE = PAGES_PER_STEP * (PAGE_SIZE // SEL_BLOCK)


def _compress_pg_kernel(ptab_ref, *refs, n_pg):
    pg_refs = refs[:n_pg]
    wbd_ref, w1f_ref, pos_ref, b1_ref, w2_ref, b2_ref, ck_ref, cv_ref, xs, carry = refs[n_pg:]
    pt = pl.program_id(1)

    @pl.when(pt == 0)
    def _():
        carry[...] = jnp.zeros_like(carry)

    hp = PAGE_SIZE // CMP_STRIDE
    M = n_pg * hp
    for i, r in enumerate(pg_refs):
        for c in range(KV_W // LANES):
            xs[c, i * PAGE_SIZE:(i + 1) * PAGE_SIZE, :] = r[0, c * LANES:(c + 1) * LANES, :].T
    first = lax.broadcasted_iota(jnp.int32, (M, CMP_HID), 0) == 0
    for kvi, out_ref in enumerate((ck_ref, cv_ref)):
        b1 = b1_ref[kvi:kvi + 1, :] + jnp.dot(pos_ref[kvi], w1f_ref[kvi], preferred_element_type=F32)[0:1, :]
        for pair in range(KV_HEADS // 2):
            c = kvi * (KV_HEADS // 2) + pair
            acc = None
            for t in range(CMP_STRIDE):
                z = xs[c, pl.ds(t, M, stride=CMP_STRIDE), :].astype(BF16)
                d = jnp.dot(z, wbd_ref[kvi, t], preferred_element_type=F32)
                acc = d if acc is None else acc + d
            for gl in range(2):
                g = pair * 2 + gl
                pr0 = acc[:, gl * 2 * CMP_HID:gl * 2 * CMP_HID + CMP_HID]
                pr1 = acc[:, gl * 2 * CMP_HID + CMP_HID:(gl + 1) * 2 * CMP_HID]
                ci = kvi * KV_HEADS + g
                prev = jnp.where(first, carry[ci, 0:1, :], pltpu.roll(pr0, 1, axis=0))
                carry[ci, 0:1, :] = pr0[M - 1:M, :]
                hid = _gelu_tanh(prev + pr1 + b1)
                o = jnp.dot(hid.astype(BF16), w2_ref[kvi], preferred_element_type=F32) + b2_ref[kvi:kvi + 1, :]
                out_ref[0, :, g * HEAD_DIM:(g + 1) * HEAD_DIM] = o.astype(BF16)


def _page_specs(n_pg):
    return [pl.BlockSpec((1, KV_W, PAGE_SIZE),
                         functools.partial(lambda bi, pi, ptab, i: (ptab[bi, pi * n_pg + i], 0, 0), i=i))
            for i in range(n_pg)]


def _compress_pages(pool_t, page_table, cw):
    B, n_pages = page_table.shape
    n_pg = PAGES_PER_STEP
    hp = PAGE_SIZE // CMP_STRIDE
    w1cat, w1f, posf, b1, w2, b2 = _compress_weights(cw)
    z = jnp.zeros_like(w1cat)
    wbd = jnp.concatenate([jnp.concatenate([w1cat, z], axis=-1), jnp.concatenate([z, w1cat], axis=-1)], axis=-2)
    o_spec = pl.BlockSpec((1, n_pg * hp, KV_W // 2), lambda bi, pi, ptab: (bi, pi, 0))
    o_shape = jax.ShapeDtypeStruct((B, n_pages * hp, KV_W // 2), BF16)
    return pl.pallas_call(
        functools.partial(_compress_pg_kernel, n_pg=n_pg),
        grid_spec=pltpu.PrefetchScalarGridSpec(
            num_scalar_prefetch=1, grid=(B, n_pages // n_pg),
            in_specs=_page_specs(n_pg) + [_const_spec(w.shape) for w in (wbd, w1f, posf, b1, w2, b2)],
            out_specs=[o_spec, o_spec],
            scratch_shapes=[pltpu.VMEM((KV_W // LANES, n_pg * PAGE_SIZE, LANES), F32),
                            pltpu.VMEM((2 * KV_HEADS, SUBLANES, CMP_HID), F32)],
        ),
        out_shape=[o_shape, o_shape],
        compiler_params=_cparams(("parallel", "arbitrary")),
        name="compress_pg",
    )(page_table, *([pool_t] * n_pg), wbd, w1f, posf, b1, w2, b2)


def _row_q(shape):
    return lax.broadcasted_iota(jnp.int32, shape, 0) & 7


def _diag_heads(o_all):
    g_row = (lax.broadcasted_iota(jnp.int32, (LANES, HEAD_DIM), 0) // 8) & (KV_HEADS - 1)
    out = jnp.zeros((LANES, HEAD_DIM), F32)
    for g in range(KV_HEADS):
        out = out + jnp.where(g_row == g, o_all[:, g * HEAD_DIM:(g + 1) * HEAD_DIM], 0.0)
    return out


def _attn_s1_kernel(ck_ref, cv_ref, qbr_ref, at_ref, ocmp_ref, pen_ref, score_sc, *, past, n_blocks, n_tiles):
    n_rows = ck_ref.shape[1]
    nb_rows = at_ref.shape[0]
    qbr = qbr_ref[0]
    s = lax.dot_general(qbr, ck_ref[0], NT_DIMS, preferred_element_type=F32)
    mrow = lax.broadcasted_iota(jnp.int32, (1, n_rows), 1)
    cend = jnp.where(mrow >= 1, mrow * CMP_STRIDE + (CMP_BLOCK - CMP_STRIDE - 1), jnp.int32(2 ** 30))
    tq = past + _row_q((LANES, 1))
    s = jnp.where(cend <= tq, s, NEG)
    e = jnp.exp(s - jnp.max(s, axis=1, keepdims=True))
    inv = jnp.where(tq >= CMP_BLOCK - 1, 1.0 / jnp.sum(e, axis=1, keepdims=True), 0.0)
    p = e * inv
    ocmp_ref[0] = _diag_heads(jnp.dot(p.astype(BF16), cv_ref[0], preferred_element_type=F32))
    rows = LANES // Q_PER_KV
    imp = p[0:rows]
    for r in range(1, Q_PER_KV):
        imp = imp + p[r * rows:(r + 1) * rows]
    imp = jnp.concatenate([imp] * Q_PER_KV, axis=0)
    blk = lax.dot_general(at_ref[...], imp, NT_DIMS, precision=lax.Precision.HIGHEST,
                          preferred_element_type=F32)
    j = lax.broadcasted_iota(jnp.int32, (nb_rows, LANES), 0)
    tl = past + (lax.broadcasted_iota(jnp.int32, (nb_rows, LANES), 1) & 7)
    cur = tl // SEL_BLOCK
    valid = (j * SEL_BLOCK <= tl) & (j < n_blocks)
    forced = (j == 0) | (j == cur) | (j == cur - 1)
    score = jnp.where(valid & forced, BIG, jnp.where(valid, blk, -1.0))
    score_sc[...] = score

    def body(i, cnt):
        ri = score_sc[pl.ds(i, 1), :]
        beats = (ri > score) | ((ri == score) & (j > i))
        return cnt + jnp.where(beats, 1.0, 0.0)

    cnt = lax.fori_loop(0, n_blocks, body, jnp.zeros((nb_rows, LANES), F32))
    pen_t = jnp.where((cnt < N_SEL) & valid, 0.0, NEG)
    n_chunks = -(-n_tiles * PEN_TILE // LANES)
    pen_t = jnp.concatenate([pen_t, jnp.full((n_chunks * LANES - nb_rows, LANES), NEG, F32)], axis=0)
    pen_r = jnp.concatenate([pen_t[c * LANES:(c + 1) * LANES, :].T for c in range(n_chunks)], axis=1)
    zeros = jnp.zeros((LANES, LANES - PEN_TILE), F32)
    for k in range(n_tiles):
        pen_ref[0, k] = jnp.concatenate([pen_r[:, k * PEN_TILE:(k + 1) * PEN_TILE], zeros], axis=1)


def _attn_s2_kernel(ptab_ref, *refs, n_pg):
    pg_refs = refs[:n_pg]
    qbr_ref, pen_ref, e_ref, m_ref, l_ref, acc_ref = refs[n_pg:]
    kw = KV_W // 2
    kt = jnp.concatenate([r[0, 0:kw, :].astype(BF16) for r in pg_refs], axis=1)
    s = jnp.dot(qbr_ref[0], kt, preferred_element_type=F32)
    s = s + jnp.dot(pen_ref[0, 0].astype(BF16), e_ref[...], preferred_element_type=F32)
    m = jnp.max(s, axis=1, keepdims=True)
    pe = jnp.exp(s - m)
    l = jnp.sum(pe, axis=1, keepdims=True)
    vt = jnp.concatenate([r[0, kw:KV_W, :].astype(BF16) for r in pg_refs], axis=1)
    acc_ref[0, 0] = lax.dot_general(pe.astype(BF16), vt, NT_DIMS, preferred_element_type=F32)
    m_ref[0, 0] = jnp.broadcast_to(m, (LANES, LANES))
    l_ref[0, 0] = jnp.broadcast_to(l, (LANES, LANES))


def _attn_s3_kernel(m_ref, l_ref, acc_ref, pen_ref, ocmp_ref, qbr_ref, snew_ref, wst_ref, wnew_ref, gt_ref,
                    o_ref, *, n_pt, n_new):
    qbr = qbr_ref[0]
    kw = KV_W // 2
    nrow = snew_ref.shape[1]
    qr = _row_q((LANES, 1))
    icol = lax.broadcasted_iota(jnp.int32, (1, nrow), 1)
    new_ok = (icol <= qr) & (icol < n_new)

    xs = snew_ref[0]
    st = lax.dot_general(qbr, xs[:, 0:kw].astype(BF16), NT_DIMS, preferred_element_type=F32)
    st = jnp.where(new_ok, st + pen_ref[0, 0][:, 0:1], NEG)
    m_tot = jnp.max(st, axis=1, keepdims=True)
    for s in range(n_pt):
        m_tot = jnp.maximum(m_tot, m_ref[0, s][:, 0:1])
    pt_ = jnp.exp(st - m_tot)
    l_tot = jnp.sum(pt_, axis=1, keepdims=True)
    acc = jnp.dot(pt_.astype(BF16), xs[:, kw:KV_W].astype(BF16), preferred_element_type=F32)
    for s in range(n_pt):
        a = jnp.exp(m_ref[0, s][:, 0:1] - m_tot)
        l_tot = l_tot + a * l_ref[0, s][:, 0:1]
        acc = acc + a * acc_ref[0, s]
    o_slc = _diag_heads(acc / l_tot)

    wn = wnew_ref[0]
    nw = wst_ref.shape[2]
    s1 = jnp.dot(qbr, wst_ref[0, 0:kw, :].astype(BF16), preferred_element_type=F32)
    s2 = lax.dot_general(qbr, wn[:, 0:kw].astype(BF16), NT_DIMS, preferred_element_type=F32)
    dist1 = nw + qr - lax.broadcasted_iota(jnp.int32, (1, nw), 1)
    mask1 = (dist1 >= 0) & (dist1 < WINDOW)
    mask2 = new_ok & (qr - icol < WINDOW)
    s1 = jnp.where(mask1, s1, NEG)
    s2 = jnp.where(mask2, s2, NEG)
    mw = jnp.maximum(jnp.max(s1, axis=1, keepdims=True), jnp.max(s2, axis=1, keepdims=True))
    p1 = jnp.where(mask1, jnp.exp(s1 - mw), 0.0)
    p2 = jnp.where(mask2, jnp.exp(s2 - mw), 0.0)
    inv = 1.0 / (jnp.sum(p1, axis=1, keepdims=True) + jnp.sum(p2, axis=1, keepdims=True))
    ow = (lax.dot_general((p1 * inv).astype(BF16), wst_ref[0, kw:KV_W, :].astype(BF16), NT_DIMS,
                          preferred_element_type=F32)
          + jnp.dot((p2 * inv).astype(BF16), wn[:, kw:KV_W].astype(BF16), preferred_element_type=F32))
    o_win = _diag_heads(ow)

    gv = gt_ref[0]
    o_ref[0] = gv[:, 0:1] * ocmp_ref[0] + gv[:, 1:2] * o_slc + gv[:, 2:3] * o_win


def _attn_sample(q_flat, gates, ck, cv, slc_pool_t, page_table, slc_new, win_state_t, win_new, past):
    B, Tn, _ = q_flat.shape
    n_pg = PAGES_PER_STEP
    assert Tn == 8 and past % (PAGE_SIZE * n_pg) == 0
    n_pages = past // PAGE_SIZE
    n_pt = n_pages // n_pg
    n_past_blk = past // SEL_BLOCK
    n_blocks = n_past_blk + 1
    nb_rows = -(-n_blocks // SUBLANES) * SUBLANES
    n_cmp = past // CMP_STRIDE - 1
    Mc = ck.shape[1]
    at = jnp.asarray(_block_sum_matrix(nb_rows, Mc, n_blocks, n_cmp))
    kw = KV_W // 2

    q5 = q_flat.reshape(B, Tn, KV_HEADS, Q_PER_KV, HEAD_DIM)
    base = jnp.transpose(q5, (0, 3, 2, 1, 4))
    eye = jnp.eye(KV_HEADS, dtype=q_flat.dtype)
    qbr = (base[:, :, :, :, None, :] * eye[None, None, :, None, :, None]).reshape(B, LANES, kw)
    g5 = gates[:, :, :N_HEADS * 3].reshape(B, Tn, KV_HEADS, Q_PER_KV, 3)
    gt = jnp.pad(jnp.transpose(g5, (0, 3, 2, 1, 4)).reshape(B, LANES, 3), ((0, 0), (0, 0), (0, LANES - 3)))

    per_b = lambda *shape: pl.BlockSpec((1,) + shape, lambda bi, *_: (bi,) + (0,) * len(shape))
    ocmp, pen = pl.pallas_call(
        functools.partial(_attn_s1_kernel, past=past, n_blocks=n_blocks, n_tiles=n_pt + 1),
        grid=(B,),
        in_specs=[per_b(Mc, kw), per_b(Mc, kw), per_b(LANES, kw),
                  pl.BlockSpec((nb_rows, Mc), lambda bi: (0, 0))],
        out_specs=[per_b(LANES, HEAD_DIM), per_b(n_pt + 1, LANES, LANES)],
        out_shape=[jax.ShapeDtypeStruct((B, LANES, HEAD_DIM), F32),
                   jax.ShapeDtypeStruct((B, n_pt + 1, LANES, LANES), F32)],
        scratch_shapes=[pltpu.VMEM((nb_rows, LANES), F32)],
        compiler_params=_cparams(("parallel",)),
        name="attn_s1",
    )(ck, cv, qbr, at)

    expand = np.zeros((LANES, n_pg * PAGE_SIZE), np.float32)
    for jb in range(PEN_TILE):
        expand[jb, jb * SEL_BLOCK:(jb + 1) * SEL_BLOCK] = 1.0
    part = lambda *shape: pl.BlockSpec((1, 1) + shape, lambda bi, pi, ptab: (bi, pi) + (0,) * len(shape))
    m_p, l_p, acc_p = pl.pallas_call(
        functools.partial(_attn_s2_kernel, n_pg=n_pg),
        grid_spec=pltpu.PrefetchScalarGridSpec(
            num_scalar_prefetch=1, grid=(B, n_pt),
            in_specs=_page_specs(n_pg) + [pl.BlockSpec((1, LANES, kw), lambda bi, pi, ptab: (bi, 0, 0)),
                                          part(LANES, LANES),
                                          pl.BlockSpec(expand.shape, lambda bi, pi, ptab: (0, 0))],
            out_specs=[part(LANES, LANES), part(LANES, LANES), part(LANES, kw)],
        ),
        out_shape=[jax.ShapeDtypeStruct((B, n_pt, LANES, LANES), F32)] * 2
        + [jax.ShapeDtypeStruct((B, n_pt, LANES, kw), F32)],
        compiler_params=_cparams(("parallel", "parallel")),
        name="attn_s2",
    )(page_table, *([slc_pool_t] * n_pg), qbr, pen, jnp.asarray(expand, BF16))

    pad_rows = 16 - Tn
    snew = jnp.pad(slc_new, ((0, 0), (0, pad_rows), (0, 0)))
    wnew = jnp.pad(win_new, ((0, 0), (0, pad_rows), (0, 0)))
    nw = win_state_t.shape[2]
    o = pl.pallas_call(
        functools.partial(_attn_s3_kernel, n_pt=n_pt, n_new=Tn),
        grid=(B,),
        in_specs=[per_b(n_pt, LANES, LANES), per_b(n_pt, LANES, LANES), per_b(n_pt, LANES, kw),
                  pl.BlockSpec((1, 1, LANES, LANES), lambda bi: (bi, n_pt, 0, 0)),
                  per_b(LANES, HEAD_DIM), per_b(LANES, kw),
                  per_b(16, KV_W), per_b(KV_W, nw), per_b(16, KV_W), per_b(LANES, LANES)],
        out_specs=per_b(LANES, HEAD_DIM),
        out_shape=jax.ShapeDtypeStruct((B, LANES, HEAD_DIM), F32),
        compiler_params=_cparams(("parallel",)),
        name="attn_s3",
    )(m_p, l_p, acc_p, pen, ocmp, qbr, snew, win_state_t, wnew, gt)
    o = jnp.transpose(o.reshape(B, Q_PER_KV, KV_HEADS, Tn, HEAD_DIM), (0, 3, 2, 1, 4))
    return o.reshape(B, Tn, D_MODEL).astype(BF16)


def _nsa_prompt_layer(h, w_in, w_o, cw, g, b, tT):
    B, T, _ = h.shape
    cmp_kv, slc_kv, win_kv, qh, skp, sv, wk, wv, gates = _nsa_proj(h, jnp.arange(T), w_in, tT, True)
    ck, cv = _compress_prompt(cmp_kv, cw)
    o = _attn_prompt(qh, ck, cv, skp, sv, wk, wv, gates)
    h2 = _oproj_ln(h, o, w_o, g, b, tT)
    win_len = min(WINDOW, T)
    return h2, cmp_kv, slc_kv, win_kv[:, T - win_len:]


def _pages_device_layout(pool):
    return jnp.transpose(pool, (0, 2, 3, 4, 1)).reshape(pool.shape[0], KV_W, pool.shape[1])


def _nsa_sample_layer(h, cmp_pool, slc_pool, win_buf, page_table, w_in, w_o, cw, g, b):
    B, Tn, _ = h.shape
    n_pages = page_table.shape[1]
    past = n_pages * PAGE_SIZE
    cmp_kv, slc_kv, win_kv, q_flat, gates = _nsa_proj(h, past + jnp.arange(Tn), w_in, Tn, False)
    assert Tn < CMP_STRIDE
    ck, cv = _compress_pages(_pages_device_layout(cmp_pool), page_table, cw)
    o = _attn_sample(q_flat, gates, ck, cv, _pages_device_layout(slc_pool), page_table, slc_kv,
                     _pages_device_layout(win_buf), win_kv, past)
    h2 = _oproj_ln(h, o, w_o, g, b, Tn)
    buf_len = win_buf.shape[1]
    new_win = jnp.concatenate([win_buf, win_kv.reshape(B, Tn, 2, KV_HEADS, HEAD_DIM)], axis=1)[:, -buf_len:]
    return h2, cmp_kv, slc_kv, new_win.reshape(B, buf_len, KV_W)


def _row_tile(T):
    return 512 if T % 512 == 0 else T


def kernel(x_prompt, x_sample, state_pool, cache_cmp_kv, cache_slc_kv, state_win_kv, state_ffn, page_table, ln_g, ln_b, pool_w, pool_scale, nsa_w_in, nsa_w_o, cmp_w1, cmp_b1, cmp_pos, cmp_w2, cmp_b2, ffn_w_up, ffn_conv_w, ffn_conv_b, ffn_w_down):
    Bp, T, D = x_prompt.shape
    Bs, Tn, _ = x_sample.shape
    past = page_table.shape[1] * PAGE_SIZE
    kv_shape = (2, KV_HEADS, HEAD_DIM)
    tp, ts = _row_tile(T), Tn

    hp = _pool_ln(x_prompt, jnp.zeros((Bp, POOL_BUF, D), F32), 0, pool_w[0], pool_scale[0], ln_g[0, 0], ln_b[0, 0], tp)
    hs = _pool_ln(x_sample, state_pool[0], past, pool_w[0], pool_scale[0], ln_g[0, 0], ln_b[0, 0], ts)
    pool_p = jnp.concatenate([jnp.zeros((Bp, POOL_BUF, D), F32), x_prompt], axis=1)[:, -POOL_BUF:][None]
    pool_s = jnp.concatenate([state_pool[0], x_sample], axis=1)[:, -POOL_BUF:][None]
    ffn = lambda h, pre, i, t: _ffn_ln(h, pre, ffn_w_up[i], ffn_conv_w[i], ffn_conv_b[i], ffn_w_down[i],
                                       ln_g[i, 1], ln_b[i, 1], t)
    zero_pre = jnp.zeros((Bp, CONV_W - 1, 2 * D_FF), F32)
    hp, ffn_p0 = ffn(hp, zero_pre, 0, tp)
    hs, ffn_s0 = ffn(hs, state_ffn[0], 0, ts)

    cw = (cmp_w1[0], cmp_b1[0], cmp_pos[0], cmp_w2[0], cmp_b2[0])
    hp, c_p, s_p, w_p = _nsa_prompt_layer(hp, nsa_w_in[0], nsa_w_o[0], cw, ln_g[1, 0], ln_b[1, 0], tp)
    hs, c_s, s_s, w_s = _nsa_sample_layer(hs, cache_cmp_kv[0], cache_slc_kv[0], state_win_kv[0], page_table,
                                          nsa_w_in[0], nsa_w_o[0], cw, ln_g[1, 0], ln_b[1, 0])
    hp, ffn_p1 = ffn(hp, zero_pre, 1, tp)
    hs, ffn_s1 = ffn(hs, state_ffn[1], 1, ts)

    kv5 = lambda a: a.reshape(a.shape[0], a.shape[1], *kv_shape)[None]
    return (hp, hs, pool_p, pool_s, kv5(c_p), kv5(c_s), kv5(s_p), kv5(s_s), kv5(w_p), kv5(w_s),
            jnp.stack([ffn_p0, ffn_p1]), jnp.stack([ffn_s0, ffn_s1]))
```

```python
import functools

import numpy as np
import jax
import jax.numpy as jnp
from jax import lax
from jax.experimental import pallas as pl
from jax.experimental.pallas import tpu as pltpu

D_MODEL = 1024
DEPTH = 2
ALPHA = (2.0 * DEPTH) ** 0.25
LN_EPS = 1e-5
POOL_WINDOWS = (2, 4, 8, 16)
POOL_CH = D_MODEL // len(POOL_WINDOWS)
POOL_BUF = max(POOL_WINDOWS) - 1
N_HEADS = 16
KV_HEADS = 4
HEAD_DIM = 64
Q_PER_KV = N_HEADS // KV_HEADS
ROT_DIM = HEAD_DIM // 4
ROPE_THETA = 500000.0
CMP_BLOCK = 32
CMP_STRIDE = 16
CMP_HID = 128
SEL_BLOCK = 64
N_SEL = 16
WINDOW = 512
KV_W = 2 * KV_HEADS * HEAD_DIM
D_FF = 2816
CONV_W = 3
PAGE_SIZE = 128
NEG = -1e30
BIG = 1e9
LOG2E = 1.4426950408889634

LANES = 128
SUBLANES = 8
Q_TILE = 128
SLC_KT = 512
ATTN_GROUPS = 2
PAGES_PER_STEP = 16
VMEM_LIMIT = 56 * 1024 * 1024

F32 = jnp.float32
BF16 = jnp.bfloat16
NT_DIMS = (((1,), (1,)), ((), ()))


def _cparams(sem):
    return pltpu.CompilerParams(dimension_semantics=sem, vmem_limit_bytes=VMEM_LIMIT)


def _layer_norm(h, g, b):
    mu = jnp.mean(h, axis=-1, keepdims=True)
    hc = h - mu
    var = jnp.mean(hc * hc, axis=-1, keepdims=True)
    return hc * lax.rsqrt(var + LN_EPS) * g + b


def _const_spec(shape):
    nd = len(shape)
    return pl.BlockSpec(shape, lambda *_: (0,) * nd, pipeline_mode=pl.Buffered(1))


def _pool_ln_kernel(x_ref, halo_ref, pre_ref, w_ref, sc_ref, g_ref, b_ref, o_ref, buf, *, tT, pos0):
    t = pl.program_id(1)
    x = x_ref[0]
    buf[0:16, :] = jnp.where(t == 0, pre_ref[0], halo_ref[0])
    buf[16:16 + tT, :] = x
    pos = pos0 + t * tT + lax.broadcasted_iota(jnp.int32, (tT, 1), 0)
    ys = []
    for gi, win in enumerate(POOL_WINDOWS):
        c0 = gi * POOL_CH
        xg = x[:, c0:c0 + POOL_CH]
        s = xg
        for i in range(1, win):
            s = s + buf[pl.ds(16 - i, tT), c0:c0 + POOL_CH]
        cnt = jnp.minimum(pos + 1, win).astype(F32)
        d = s / cnt - xg
        ys.append(jnp.dot(d.astype(BF16), w_ref[gi], preferred_element_type=F32))
    y = jnp.concatenate(ys, axis=1) * sc_ref[...]
    o_ref[0] = _layer_norm(ALPHA * x + y, g_ref[...], b_ref[...])


def _pool_ln(x, prefix, pos0, w, scale, g, b, tT):
    B, T, D = x.shape
    nT = T // tT
    pre16 = jnp.concatenate([jnp.zeros((B, 1, D), x.dtype), prefix], axis=1)
    if T >= 16:
        halo_src = x
        per = tT // 16
        halo_spec = pl.BlockSpec((1, 16, D), lambda bi, ti: (bi, jnp.maximum(ti * per - 1, 0), 0))
    else:
        halo_src = pre16
        halo_spec = pl.BlockSpec((1, 16, D), lambda bi, ti: (bi, 0, 0))
    return pl.pallas_call(
        functools.partial(_pool_ln_kernel, tT=tT, pos0=pos0),
        grid=(B, nT),
        in_specs=[
            pl.BlockSpec((1, tT, D), lambda bi, ti: (bi, ti, 0)),
            halo_spec,
            pl.BlockSpec((1, 16, D), lambda bi, ti: (bi, 0, 0)),
            _const_spec((len(POOL_WINDOWS), POOL_CH, POOL_CH)),
            _const_spec((1, D)), _const_spec((1, D)), _const_spec((1, D)),
        ],
        out_specs=pl.BlockSpec((1, tT, D), lambda bi, ti: (bi, ti, 0)),
        out_shape=jax.ShapeDtypeStruct((B, T, D), F32),
        scratch_shapes=[pltpu.VMEM((16 + tT, D), F32)],
        compiler_params=_cparams(("parallel", "parallel")),
        name="pool_ln",
    )(x, halo_src, pre16, w.astype(BF16), scale.reshape(1, D), g.reshape(1, D), b.reshape(1, D))


FFN_CHUNK = 256


def _ffn_ln_kernel(h_ref, pre_ref, wup_ref, cw_ref, cb_ref, wdn_ref, g_ref, b_ref, o_ref, st_ref,
                   ubuf, act, *, tT):
    t = pl.program_id(1)

    @pl.when(t == 0)
    def _():
        st_ref[0] = pre_ref[0]

    h = h_ref[0]
    hb = h.astype(BF16)
    fc = FFN_CHUNK
    for j in range(D_FF // fc):
        cs = []
        for half in range(2):
            c0 = half * D_FF + j * fc
            up = jnp.dot(hb, wup_ref[:, c0:c0 + fc], preferred_element_type=F32)
            ubuf[half, 0:8, :] = st_ref[0, :, c0:c0 + fc]
            ubuf[half, 8:8 + tT, :] = up
            st_ref[0, :, c0:c0 + fc] = ubuf[half, tT:tT + 8, :]
            c = (cb_ref[:, c0:c0 + fc]
                 + cw_ref[0:1, c0:c0 + fc] * ubuf[half, pl.ds(6, tT), :]
                 + cw_ref[1:2, c0:c0 + fc] * ubuf[half, pl.ds(7, tT), :]
                 + cw_ref[2:3, c0:c0 + fc] * up)
            cs.append(c)
        a = cs[0] * jax.nn.sigmoid(cs[0]) * cs[1]
        act[:, j * fc:(j + 1) * fc] = a.astype(BF16)
    y = jnp.dot(act[...], wdn_ref[...], preferred_element_type=F32)
    o_ref[0] = _layer_norm(ALPHA * h + y, g_ref[...], b_ref[...])


def _ffn_ln(h, prefix, w_up, conv_w, conv_b, w_down, g, b, tT):
    B, T, D = h.shape
    nT = T // tT
    F2 = 2 * D_FF
    pre8 = jnp.concatenate([jnp.zeros((B, 8 - (CONV_W - 1), F2), F32), prefix], axis=1)
    out, st = pl.pallas_call(
        functools.partial(_ffn_ln_kernel, tT=tT),
        grid=(B, nT),
        in_specs=[
            pl.BlockSpec((1, tT, D), lambda bi, ti: (bi, ti, 0)),
            pl.BlockSpec((1, 8, F2), lambda bi, ti: (bi, 0, 0)),
            _const_spec((D, F2)),
            _const_spec((CONV_W, F2)),
            _const_spec((1, F2)),
            _const_spec((D_FF, D)),
            _const_spec((1, D)), _const_spec((1, D)),
        ],
        out_specs=[pl.BlockSpec((1, tT, D), lambda bi, ti: (bi, ti, 0)),
                   pl.BlockSpec((1, 8, F2), lambda bi, ti: (bi, 0, 0))],
        out_shape=[jax.ShapeDtypeStruct((B, T, D), F32), jax.ShapeDtypeStruct((B, 8, F2), F32)],
        scratch_shapes=[pltpu.VMEM((2, tT + 8, FFN_CHUNK), F32), pltpu.VMEM((tT, D_FF), BF16)],
        compiler_params=_cparams(("parallel", "arbitrary")),
        name="ffn_ln",
    )(h, pre8, w_up.astype(BF16), conv_w, conv_b.reshape(1, F2), w_down.astype(BF16),
      g.reshape(1, D), b.reshape(1, D))
    return out, st[:, 8 - (CONV_W - 1):, :]


def _rope_tables(pos):
    half = ROT_DIM // 2
    inv = jnp.power(ROPE_THETA, -2.0 * jnp.arange(half, dtype=F32) / ROT_DIM)
    ang = pos.astype(F32)[:, None] * inv[None, :]
    cos, sin = jnp.cos(ang), jnp.sin(ang)
    T = pos.shape[0]
    one, zero = jnp.ones((T, HEAD_DIM - ROT_DIM), F32), jnp.zeros((T, HEAD_DIM - half), F32)
    c64 = jnp.concatenate([cos, cos, one], axis=1)
    lo64 = jnp.concatenate([-sin, zero], axis=1)
    hi64 = jnp.concatenate([jnp.zeros((T, half), F32), sin, jnp.zeros((T, HEAD_DIM - ROT_DIM), F32)], axis=1)
    tile = lambda a: jnp.concatenate([a, a], axis=1)
    return tile(c64), tile(lo64), tile(hi64)


def _rope_chunk(x, c, lo, hi):
    return x * c + pltpu.roll(x, LANES - ROT_DIM // 2, axis=1) * lo + pltpu.roll(x, ROT_DIM // 2, axis=1) * hi


def _nsa_proj_kernel(h_ref, wq_ref, wkv_ref, wg_ref, c_ref, lo_ref, hi_ref, *outs, tT, head_major):
    t = pl.program_id(1)
    hb = h_ref[0].astype(BF16)
    c, lo, hi = c_ref[...], lo_ref[...], hi_ref[...]
    q = jnp.dot(hb, wq_ref[...], preferred_element_type=F32)
    kv = jnp.dot(hb, wkv_ref[...], preferred_element_type=F32)
    gl = jnp.dot(hb, wg_ref[...], preferred_element_type=F32)
    gates = jax.nn.sigmoid(gl)
    lane = lax.broadcasted_iota(jnp.int32, (tT, LANES), 1)
    low = lane < HEAD_DIM
    scale = HEAD_DIM ** -0.5 * (LOG2E if head_major else 1.0)
    qc = [_rope_chunk(q[:, i * LANES:(i + 1) * LANES], c, lo, hi) * scale for i in range(D_MODEL // LANES)]
    kvc = []
    for br in range(3):
        for i in range(KV_W // LANES):
            x = kv[:, br * KV_W + i * LANES: br * KV_W + (i + 1) * LANES]
            kvc.append(_rope_chunk(x, c, lo, hi) if i < KV_W // (2 * LANES) else x)
    if head_major:
        cmp_ref, slc_ref, win_ref, qh_ref, skp_ref, sv_ref, wk_ref, wv_ref, g_ref = outs
    else:
        cmp_ref, slc_ref, win_ref, qf_ref, g_ref = outs
    for br, ref in enumerate((cmp_ref, slc_ref, win_ref)):
        ref[0] = jnp.concatenate(kvc[br * 4:(br + 1) * 4], axis=1)
    g_ref[0] = gates
    if not head_major:
        qf_ref[0] = jnp.concatenate(qc, axis=1).astype(BF16)
        return

    def split(x):
        return jnp.where(low, x, 0.0), jnp.where(low, pltpu.roll(x, HEAD_DIM, axis=1), 0.0)

    for i in range(D_MODEL // LANES):
        a, b = split(qc[i])
        qh_ref[0, 2 * i] = a.astype(BF16)
        qh_ref[0, 2 * i + 1] = b.astype(BF16)
    blk = (t * tT + lax.broadcasted_iota(jnp.int32, (tT, LANES), 0)) // SEL_BLOCK
    onehot = jnp.where(lane - HEAD_DIM == blk, 1.0, 0.0)
    ones_col = jnp.where(lane == HEAD_DIM, 1.0, 0.0)
    for br, (kref, vref) in ((1, (skp_ref, sv_ref)), (2, (wk_ref, wv_ref))):
        for i in range(2):
            ka, kb = split(kvc[br * 4 + i])
            va, vb = split(kvc[br * 4 + 2 + i])
            if br == 1:
                ka, kb = ka + onehot, kb + onehot
            va, vb = va + ones_col, vb + ones_col
            kref[0, 2 * i] = ka.astype(BF16)
            kref[0, 2 * i + 1] = kb.astype(BF16)
            vref[0, 2 * i] = va.astype(BF16)
            vref[0, 2 * i + 1] = vb.astype(BF16)


def _nsa_proj(h, pos, w_in, tT, head_major):
    B, T, D = h.shape
    nT = T // tT
    nq = N_HEADS * HEAD_DIM
    wq = w_in[:, :nq].astype(BF16)
    wkv = w_in[:, nq:nq + 3 * KV_W].astype(BF16)
    wg = w_in[:, nq + 3 * KV_W:]
    if head_major:
        GW = KV_HEADS * LANES
        wg = jnp.pad(wg.reshape(D, KV_HEADS, Q_PER_KV * 3), ((0, 0), (0, 0), (0, LANES - Q_PER_KV * 3))).reshape(D, GW)
    else:
        GW = LANES
        wg = jnp.pad(wg, ((0, 0), (0, LANES - wg.shape[1])))
    wg = wg.astype(BF16)
    c, lo, hi = _rope_tables(pos)
    row = lambda w: pl.BlockSpec((1, tT, w), lambda bi, ti: (bi, ti, 0))
    hm = lambda n: pl.BlockSpec((1, n, tT, LANES), lambda bi, ti: (bi, 0, ti, 0))
    tab = pl.BlockSpec((tT, LANES), lambda bi, ti: (ti, 0))
    out_specs = [row(KV_W), row(KV_W), row(KV_W)]
    out_shape = [jax.ShapeDtypeStruct((B, T, KV_W), F32)] * 3
    if head_major:
        out_specs += [hm(N_HEADS)] + [hm(KV_HEADS)] * 4 + [row(GW)]
        out_shape += ([jax.ShapeDtypeStruct((B, N_HEADS, T, LANES), BF16)]
                      + [jax.ShapeDtypeStruct((B, KV_HEADS, T, LANES), BF16)] * 4
                      + [jax.ShapeDtypeStruct((B, T, GW), F32)])
    else:
        out_specs += [row(D), row(GW)]
        out_shape += [jax.ShapeDtypeStruct((B, T, D), BF16), jax.ShapeDtypeStruct((B, T, GW), F32)]
    return pl.pallas_call(
        functools.partial(_nsa_proj_kernel, tT=tT, head_major=head_major),
        grid=(B, nT),
        in_specs=[row(D), _const_spec((D, nq)), _const_spec((D, 3 * KV_W)), _const_spec((D, GW)), tab, tab, tab],
        out_specs=out_specs,
        out_shape=out_shape,
        compiler_params=_cparams(("parallel", "parallel")),
        name="nsa_proj",
    )(h, wq, wkv, wg, c, lo, hi)


def _gelu_tanh(x):
    return 0.5 * x * (1.0 + jnp.tanh(0.7978845608028654 * (x + 0.044715 * x * x * x)))


def _compress_kernel(x_ref, w1_ref, w1f_ref, pos_ref, b1_ref, w2_ref, b2_ref, ck_ref, cv_ref):
    for kvi, out_ref in enumerate((ck_ref, cv_ref)):
        b1 = b1_ref[kvi:kvi + 1, :] + jnp.dot(pos_ref[kvi], w1f_ref[kvi], preferred_element_type=F32)[0:1, :]
        for g in range(KV_HEADS):
            acc = None
            for t in range(CMP_STRIDE):
                off = t * KV_W + kvi * (KV_W // 2) + g * HEAD_DIM
                d = jnp.dot(x_ref[0, :, off:off + HEAD_DIM].astype(BF16), w1_ref[kvi, t],
                            preferred_element_type=F32)
                acc = d if acc is None else acc + d
            pr0, pr1 = acc[:, :CMP_HID], acc[:, CMP_HID:]
            hid = _gelu_tanh(pltpu.roll(pr0, 1, axis=0) + pr1 + b1)
            o = jnp.dot(hid.astype(BF16), w2_ref[kvi], preferred_element_type=F32) + b2_ref[kvi:kvi + 1, :]
            out_ref[0, g] = jnp.concatenate([o, jnp.zeros_like(o)], axis=1).astype(BF16)


def _compress_weights(cw):
    w1, b1, pos_emb, w2, b2 = cw
    assert CMP_BLOCK == 2 * CMP_STRIDE
    w1cat = jnp.concatenate([w1[:, :CMP_STRIDE], w1[:, CMP_STRIDE:]], axis=-1).astype(BF16)
    w1f = w1.reshape(2, CMP_BLOCK * HEAD_DIM, CMP_HID).astype(BF16)
    posf = jnp.broadcast_to(pos_emb.reshape(2, 1, CMP_BLOCK * HEAD_DIM), (2, SUBLANES, CMP_BLOCK * HEAD_DIM)).astype(BF16)
    return w1cat, w1f, posf, b1, w2.astype(BF16), b2


def _compress_prompt(cmp_kv, cw):
    B, T, _ = cmp_kv.shape
    n_half = T // CMP_STRIDE
    x = cmp_kv.reshape(B, n_half, CMP_STRIDE * KV_W)
    ws = _compress_weights(cw)
    o_spec = pl.BlockSpec((1, KV_HEADS, n_half, LANES), lambda bi: (bi, 0, 0, 0))
    o_shape = jax.ShapeDtypeStruct((B, KV_HEADS, n_half, LANES), BF16)
    return pl.pallas_call(
        _compress_kernel,
        grid=(B,),
        in_specs=[pl.BlockSpec((1, n_half, CMP_STRIDE * KV_W), lambda bi: (bi, 0, 0))]
        + [_const_spec(w.shape) for w in ws],
        out_specs=[o_spec, o_spec],
        out_shape=[o_shape, o_shape],
        compiler_params=_cparams(("parallel",)),
        name="compress_hm",
    )(x, *ws)


def _block_sum_matrix(n_out, n_in, n_blocks, n_cmp):
    a = np.zeros((n_out, n_in), np.float32)
    for j in range(n_blocks):
        for m in range(4 * j, 4 * j + 5):
            if 1 <= m <= n_cmp:
                a[j, m] = 1.0
    return a


def _attn_prompt_kernel(q_ref, ck_ref, cv_ref, kp_ref, vs_ref, wk_ref, wv_ref, gt_ref, at_ref, o_ref,
                        s_buf, m_run, acc_sc, oc_sc, *, n_cmp_rows, n_blk_rows):
    qt = pl.program_id(2)
    s0 = qt * Q_TILE
    R = Q_PER_KV * Q_TILE
    GS = ATTN_GROUPS
    Qs = [q_ref[0, gi].reshape(R, LANES) for gi in range(GS)]
    tq = s0 + (lax.broadcasted_iota(jnp.int32, (R, 1), 0) & (Q_TILE - 1))

    mrow = lax.broadcasted_iota(jnp.int32, (1, n_cmp_rows), 1)
    cend = jnp.where(mrow >= 1, mrow * CMP_STRIDE + (CMP_BLOCK - CMP_STRIDE - 1), jnp.int32(2 ** 30))
    j = lax.broadcasted_iota(jnp.int32, (n_blk_rows, Q_TILE), 0)
    tl = s0 + lax.broadcasted_iota(jnp.int32, (n_blk_rows, Q_TILE), 1)
    cur = tl // SEL_BLOCK
    valid = j * SEL_BLOCK <= tl
    forced = valid & ((j == 0) | (j == cur) | (j == cur - 1))
    scores = []
    for gi in range(GS):
        sc = lax.dot_general(Qs[gi], ck_ref[0, gi], NT_DIMS, preferred_element_type=F32)
        sc = jnp.where(cend <= tq, sc, NEG)
        e = jnp.exp2(sc - jnp.max(sc, axis=1, keepdims=True))
        inv = jnp.where(tq >= CMP_BLOCK - 1, 1.0 / jnp.sum(e, axis=1, keepdims=True), 0.0)
        p = e * inv
        oc_sc[gi] = jnp.dot(p.astype(BF16), cv_ref[0, gi], preferred_element_type=F32)
        imp = p[0:Q_TILE]
        for r in range(1, Q_PER_KV):
            imp = imp + p[r * Q_TILE:(r + 1) * Q_TILE]
        blk = lax.dot_general(at_ref[...], imp, NT_DIMS, precision=lax.Precision.HIGHEST,
                              preferred_element_type=F32)
        scores.append(jnp.where(forced, BIG, jnp.where(valid, blk, -1.0)))

    wlen = WINDOW + Q_TILE
    n_sub = wlen // Q_TILE
    w0 = pl.multiple_of(jnp.maximum(s0 - WINDOW, 0), Q_TILE)
    tri = (lax.broadcasted_iota(jnp.int32, (Q_TILE, Q_TILE), 1)
           <= lax.broadcasted_iota(jnp.int32, (Q_TILE, Q_TILE), 0))
    steady = s0 >= WINDOW
    wbias = []
    for i in range(n_sub):
        lower_ok = jnp.where(steady, i > 0, i <= qt)
        upper_ok = jnp.where(steady, i < n_sub - 1, i < qt)
        b = jnp.where(tri, jnp.where(lower_ok, 0.0, NEG), jnp.where(upper_ok, 0.0, NEG))
        wbias.append(jnp.concatenate([b] * Q_PER_KV, axis=0))
    wbias = jnp.concatenate(wbias, axis=1)
    gv = gt_ref[0]
    gcol = lambda gi, jb: jnp.concatenate(
        [gv[:, gi * LANES + 3 * r + jb:gi * LANES + 3 * r + jb + 1] for r in range(Q_PER_KV)], axis=0)
    for gi in range(GS):
        s = lax.dot_general(Qs[gi], wk_ref[0, gi, pl.ds(w0, wlen), :], NT_DIMS, preferred_element_type=F32)
        s = s + wbias
        pw = jnp.exp2(s - jnp.max(s, axis=1, keepdims=True)).astype(BF16)
        acc_w = jnp.dot(pw, wv_ref[0, gi, pl.ds(w0, wlen), :], preferred_element_type=F32)
        oc_sc[gi] = gcol(gi, 0) * oc_sc[gi] + (gcol(gi, 2) / acc_w[:, HEAD_DIM:HEAD_DIM + 1]) * acc_w

    nv = n_blk_rows // SUBLANES
    jv = lax.broadcasted_iota(jnp.int32, (SUBLANES, Q_TILE), 0)
    svs = [[sc_[v * SUBLANES:(v + 1) * SUBLANES, :] for v in range(nv)] for sc_ in scores]
    cnts = [[jnp.zeros((SUBLANES, Q_TILE), F32) for _ in range(nv)] for _ in range(GS)]
    for i in range(n_blk_rows):
        vi, si = divmod(i, SUBLANES)
        for gi in range(GS):
            sv, cnt = svs[gi], cnts[gi]
            ri = sv[vi][si:si + 1, :]
            for v in range(nv):
                if v < vi:
                    beats = ri > sv[v]
                elif v > vi:
                    beats = ri >= sv[v]
                else:
                    beats = (ri > sv[v]) | ((jv > si) & (ri >= sv[v]))
                cnt[v] = cnt[v] + jnp.where(beats, 1.0, 0.0)
    Qa = []
    for gi in range(GS):
        sel = (jnp.concatenate(cnts[gi], axis=0) < N_SEL) & valid
        pen_rows = [jnp.zeros((HEAD_DIM, Q_TILE), F32), jnp.where(sel, 0.0, NEG)]
        if n_blk_rows < HEAD_DIM:
            pen_rows.append(jnp.zeros((HEAD_DIM - n_blk_rows, Q_TILE), F32))
        pen = jnp.concatenate(pen_rows, axis=0).T
        Qa.append(Qs[gi] + jnp.concatenate([pen.astype(BF16)] * Q_PER_KV, axis=0))

    n_full = s0 // SLC_KT
    m_run[...] = jnp.full(m_run.shape, NEG, F32)
    acc_sc[...] = jnp.zeros(acc_sc.shape, F32)

    def scores_tile(kt, causal):
        k0 = pl.multiple_of(kt * SLC_KT, SLC_KT)
        for gi in range(GS):
            s = lax.dot_general(Qa[gi], kp_ref[0, gi, pl.ds(k0, SLC_KT), :], NT_DIMS, preferred_element_type=F32)
            if causal:
                kp_ = k0 + lax.broadcasted_iota(jnp.int32, (1, SLC_KT), 1)
                s = jnp.where(kp_ <= tq, s, NEG)
            s_buf[gi, kt] = s
            mx = m_run[gi]
            for c in range(SLC_KT // LANES):
                mx = jnp.maximum(mx, s[:, c * LANES:(c + 1) * LANES])
            m_run[gi] = mx

    def p1(kt, carry):
        scores_tile(kt, False)
        return carry

    lax.fori_loop(0, n_full, p1, 0)
    scores_tile(n_full, True)
    ms = [jnp.max(m_run[gi], axis=1, keepdims=True) for gi in range(GS)]

    def p2(kt, carry):
        k0 = pl.multiple_of(kt * SLC_KT, SLC_KT)
        for gi in range(GS):
            pe = jnp.exp2(s_buf[gi, kt] - ms[gi]).astype(BF16)
            acc_sc[gi] += jnp.dot(pe, vs_ref[0, gi, pl.ds(k0, SLC_KT), :], preferred_element_type=F32)
        return carry

    lax.fori_loop(0, n_full + 1, p2, 0)

    first_head = lax.broadcasted_iota(jnp.int32, (Q_TILE, LANES), 1) < HEAD_DIM
    for gi in range(GS):
        acc_s = acc_sc[gi]
        o = oc_sc[gi] + (gcol(gi, 1) / acc_s[:, HEAD_DIM:HEAD_DIM + 1]) * acc_s
        for pr in range(Q_PER_KV // 2):
            a = o[(2 * pr) * Q_TILE:(2 * pr + 1) * Q_TILE]
            b = pltpu.roll(o[(2 * pr + 1) * Q_TILE:(2 * pr + 2) * Q_TILE], HEAD_DIM, axis=1)
            c0 = (gi * Q_PER_KV // 2 + pr) * LANES
            o_ref[0, :, c0:c0 + LANES] = jnp.where(first_head, a, b).astype(BF16)


def _attn_prompt(qh, ck, cv, skp, sv, wk, wv, gates):
    B, _, T, _ = qh.shape
    GS = ATTN_GROUPS
    n_cmp_rows = ck.shape[2]
    n_blocks = T // SEL_BLOCK
    n_blk_rows = max(SUBLANES, n_blocks)
    assert n_blocks <= HEAD_DIM and T >= WINDOW + Q_TILE and T % SLC_KT == 0 and KV_HEADS % GS == 0
    n_cmp = T // CMP_STRIDE - CMP_BLOCK // CMP_STRIDE + 1
    at = jnp.asarray(_block_sum_matrix(n_blk_rows, n_cmp_rows, n_blocks, n_cmp))
    q5 = qh.reshape(B, KV_HEADS, Q_PER_KV, T, LANES)
    R = Q_PER_KV * Q_TILE
    full = lambda n: pl.BlockSpec((1, GS, n, LANES), lambda bi, gi, qi: (bi, gi, 0, 0),
                                  pipeline_mode=pl.Buffered(1))
    return pl.pallas_call(
        functools.partial(_attn_prompt_kernel, n_cmp_rows=n_cmp_rows, n_blk_rows=n_blk_rows),
        grid=(B, KV_HEADS // GS, T // Q_TILE),
        in_specs=[
            pl.BlockSpec((1, GS, Q_PER_KV, Q_TILE, LANES), lambda bi, gi, qi: (bi, gi, 0, qi, 0)),
            full(n_cmp_rows), full(n_cmp_rows), full(T), full(T), full(T), full(T),
            pl.BlockSpec((1, Q_TILE, GS * LANES), lambda bi, gi, qi: (bi, qi, gi)),
            pl.BlockSpec((n_blk_rows, n_cmp_rows), lambda bi, gi, qi: (0, 0)),
        ],
        out_specs=pl.BlockSpec((1, Q_TILE, GS * Q_PER_KV * HEAD_DIM), lambda bi, gi, qi: (bi, qi, gi)),
        out_shape=jax.ShapeDtypeStruct((B, T, D_MODEL), BF16),
        scratch_shapes=[pltpu.VMEM((GS, T // SLC_KT, R, SLC_KT), F32), pltpu.VMEM((GS, R, LANES), F32),
                        pltpu.VMEM((GS, R, LANES), F32), pltpu.VMEM((GS, R, LANES), F32)],
        compiler_params=_cparams(("parallel", "parallel", "arbitrary")),
        name="attn_prompt",
    )(q5, ck, cv, skp, sv, wk, wv, gates, at)


def _oproj_ln_kernel(h_ref, o_ref, w_ref, g_ref, b_ref, out_ref):
    y = jnp.dot(o_ref[0], w_ref[...], preferred_element_type=F32)
    out_ref[0] = _layer_norm(ALPHA * h_ref[0] + y, g_ref[...], b_ref[...])


def _oproj_ln(h, o, w_o, g, b, tT):
    B, T, D = h.shape
    row = pl.BlockSpec((1, tT, D), lambda bi, ti: (bi, ti, 0))
    return pl.pallas_call(
        _oproj_ln_kernel,
        grid=(B, T // tT),
        in_specs=[row, row, _const_spec((D, D)), _const_spec((1, D)), _const_spec((1, D))],
        out_specs=row,
        out_shape=jax.ShapeDtypeStruct((B, T, D), F32),
        compiler_params=_cparams(("parallel", "parallel")),
        name="oproj_ln",
    )(h, o, w_o.astype(BF16), g.reshape(1, D), b.reshape(1, D))


PEN_TILE = PAGES_PER_STEP * (PAGE_SIZE // SEL_BLOCK)
S1_BATCH = 4


def _compress_pg_kernel(ptab_ref, *refs, n_pg):
    pg_refs = refs[:n_pg]
    wbd_ref, w1f_ref, pos_ref, b1_ref, w2_ref, b2_ref, ck_ref, cv_ref, xs, carry = refs[n_pg:]
    pt = pl.program_id(1)

    @pl.when(pt == 0)
    def _():
        carry[...] = jnp.zeros_like(carry)

    hp = PAGE_SIZE // CMP_STRIDE
    M = n_pg * hp
    for i, r in enumerate(pg_refs):
        for c in range(KV_W // LANES):
            xs[c, i * PAGE_SIZE:(i + 1) * PAGE_SIZE, :] = r[0, c * LANES:(c + 1) * LANES, :].T
    first = lax.broadcasted_iota(jnp.int32, (M, CMP_HID), 0) == 0
    for kvi, out_ref in enumerate((ck_ref, cv_ref)):
        b1 = b1_ref[kvi:kvi + 1, :] + jnp.dot(pos_ref[kvi], w1f_ref[kvi], preferred_element_type=F32)[0:1, :]
        for pair in range(KV_HEADS // 2):
            c = kvi * (KV_HEADS // 2) + pair
            acc = None
            for t in range(CMP_STRIDE):
                z = xs[c, pl.ds(t, M, stride=CMP_STRIDE), :].astype(BF16)
                d = jnp.dot(z, wbd_ref[kvi, t], preferred_element_type=F32)
                acc = d if acc is None else acc + d
            for gl in range(2):
                g = pair * 2 + gl
                pr0 = acc[:, gl * 2 * CMP_HID:gl * 2 * CMP_HID + CMP_HID]
                pr1 = acc[:, gl * 2 * CMP_HID + CMP_HID:(gl + 1) * 2 * CMP_HID]
                ci = kvi * KV_HEADS + g
                prev = jnp.where(first, carry[ci, 0:1, :], pltpu.roll(pr0, 1, axis=0))
                carry[ci, 0:1, :] = pr0[M - 1:M, :]
                hid = _gelu_tanh(prev + pr1 + b1)
                o = jnp.dot(hid.astype(BF16), w2_ref[kvi], preferred_element_type=F32) + b2_ref[kvi:kvi + 1, :]
                out_ref[0, :, g * HEAD_DIM:(g + 1) * HEAD_DIM] = o.astype(BF16)


def _page_specs(n_pg):
    return [pl.BlockSpec((1, KV_W, PAGE_SIZE),
                         functools.partial(lambda bi, pi, ptab, i: (ptab[bi, pi * n_pg + i], 0, 0), i=i))
            for i in range(n_pg)]


def _compress_pages(pool_t, page_table, cw):
    B, n_pages = page_table.shape
    n_pg = PAGES_PER_STEP
    hp = PAGE_SIZE // CMP_STRIDE
    w1cat, w1f, posf, b1, w2, b2 = _compress_weights(cw)
    z = jnp.zeros_like(w1cat)
    wbd = jnp.concatenate([jnp.concatenate([w1cat, z], axis=-1), jnp.concatenate([z, w1cat], axis=-1)], axis=-2)
    o_spec = pl.BlockSpec((1, n_pg * hp, KV_W // 2), lambda bi, pi, ptab: (bi, pi, 0))
    o_shape = jax.ShapeDtypeStruct((B, n_pages * hp, KV_W // 2), BF16)
    return pl.pallas_call(
        functools.partial(_compress_pg_kernel, n_pg=n_pg),
        grid_spec=pltpu.PrefetchScalarGridSpec(
            num_scalar_prefetch=1, grid=(B, n_pages // n_pg),
            in_specs=_page_specs(n_pg) + [_const_spec(w.shape) for w in (wbd, w1f, posf, b1, w2, b2)],
            out_specs=[o_spec, o_spec],
            scratch_shapes=[pltpu.VMEM((KV_W // LANES, n_pg * PAGE_SIZE, LANES), F32),
                            pltpu.VMEM((2 * KV_HEADS, SUBLANES, CMP_HID), F32)],
        ),
        out_shape=[o_shape, o_shape],
        compiler_params=_cparams(("parallel", "arbitrary")),
        name="compress_pg",
    )(page_table, *([pool_t] * n_pg), wbd, w1f, posf, b1, w2, b2)


def _row_q(shape):
    return lax.broadcasted_iota(jnp.int32, shape, 0) & 7


def _diag_heads(o_all):
    g_row = (lax.broadcasted_iota(jnp.int32, (LANES, HEAD_DIM), 0) // 8) & (KV_HEADS - 1)
    out = jnp.zeros((LANES, HEAD_DIM), F32)
    for g in range(KV_HEADS):
        out = out + jnp.where(g_row == g, o_all[:, g * HEAD_DIM:(g + 1) * HEAD_DIM], 0.0)
    return out


def _attn_s1_kernel(ck_ref, cv_ref, qbr_ref, at_ref, ocmp_ref, pen_ref, score_sc, *, past, n_blocks, n_tiles):
    n_rows = ck_ref.shape[1]
    nb_rows = at_ref.shape[0]
    mrow = lax.broadcasted_iota(jnp.int32, (1, n_rows), 1)
    cend = jnp.where(mrow >= 1, mrow * CMP_STRIDE + (CMP_BLOCK - CMP_STRIDE - 1), jnp.int32(2 ** 30))
    tq = past + _row_q((LANES, 1))
    j = lax.broadcasted_iota(jnp.int32, (nb_rows, LANES), 0)
    lane = lax.broadcasted_iota(jnp.int32, (nb_rows, LANES), 1)
    grp = lane // (LANES // Q_PER_KV)
    tl = past + (lane & 7)
    cur = tl // SEL_BLOCK
    valid = (j * SEL_BLOCK <= tl) & (j < n_blocks)
    forced = valid & ((j == 0) | (j == cur) | (j == cur - 1))
    rows = LANES // Q_PER_KV
    score = None
    for k in range(S1_BATCH):
        s = lax.dot_general(qbr_ref[k], ck_ref[k], NT_DIMS, preferred_element_type=F32)
        s = jnp.where(cend <= tq, s, NEG)
        e = jnp.exp(s - jnp.max(s, axis=1, keepdims=True))
        inv = jnp.where(tq >= CMP_BLOCK - 1, 1.0 / jnp.sum(e, axis=1, keepdims=True), 0.0)
        p = e * inv
        ocmp_ref[k] = _diag_heads(jnp.dot(p.astype(BF16), cv_ref[k], preferred_element_type=F32))
        imp = p[0:rows]
        for r in range(1, Q_PER_KV):
            imp = imp + p[r * rows:(r + 1) * rows]
        imp = jnp.concatenate([imp] * Q_PER_KV, axis=0)
        blk = lax.dot_general(at_ref[...], imp, NT_DIMS, precision=lax.Precision.HIGHEST,
                              preferred_element_type=F32)
        sk = jnp.where(forced, BIG, jnp.where(valid, blk, -1.0))
        score = sk if score is None else jnp.where(grp == k, sk, score)
    score_sc[...] = score

    def body(i, cnt):
        ri = score_sc[pl.ds(i, 1), :]
        beats = (ri > score) | ((ri == score) & (j > i))
        return cnt + jnp.where(beats, 1.0, 0.0)

    cnt = lax.fori_loop(0, n_blocks, body, jnp.zeros((nb_rows, LANES), F32))
    pen4 = jnp.where((cnt < N_SEL) & valid, 0.0, NEG)
    rolled = [pen4] + [pltpu.roll(pen4, sft * rows, axis=1) for sft in range(1, S1_BATCH)]
    n_chunks = -(-n_tiles * PEN_TILE // LANES)
    zeros = jnp.zeros((LANES, LANES - PEN_TILE), F32)
    for k in range(S1_BATCH):
        pen_t = rolled[(0 - k) % S1_BATCH]
        for m in range(1, S1_BATCH):
            pen_t = jnp.where(grp == m, rolled[(m - k) % S1_BATCH], pen_t)
        pen_t = jnp.concatenate([pen_t, jnp.full((n_chunks * LANES - nb_rows, LANES), NEG, F32)], axis=0)
        pen_r = jnp.concatenate([pen_t[c * LANES:(c + 1) * LANES, :].T for c in range(n_chunks)], axis=1)
        for t in range(n_tiles):
            pen_ref[k, t] = jnp.concatenate([pen_r[:, t * PEN_TILE:(t + 1) * PEN_TILE], zeros], axis=1)


def _attn_s2_kernel(ptab_ref, *refs, n_pg):
    pg_refs = refs[:n_pg]
    qbr_ref, pen_ref, e_ref, m_ref, l_ref, acc_ref = refs[n_pg:]
    kw = KV_W // 2
    kt = jnp.concatenate([r[0, 0:kw, :].astype(BF16) for r in pg_refs], axis=1)
    s = jnp.dot(qbr_ref[0], kt, preferred_element_type=F32)
    s = s + jnp.dot(pen_ref[0, 0].astype(BF16), e_ref[...], preferred_element_type=F32)
    m = jnp.max(s, axis=1, keepdims=True)
    pe = jnp.exp(s - m)
    l = jnp.sum(pe, axis=1, keepdims=True)
    vt = jnp.concatenate([r[0, kw:KV_W, :].astype(BF16) for r in pg_refs], axis=1)
    acc_ref[0, 0] = lax.dot_general(pe.astype(BF16), vt, NT_DIMS, preferred_element_type=F32)
    m_ref[0, 0] = jnp.broadcast_to(m, (LANES, LANES))
    l_ref[0, 0] = jnp.broadcast_to(l, (LANES, LANES))


def _attn_s3_kernel(m_ref, l_ref, acc_ref, pen_ref, ocmp_ref, qbr_ref, snew_ref, wst_ref, wnew_ref, gt_ref,
                    o_ref, *, n_pt, n_new):
    qbr = qbr_ref[0]
    kw = KV_W // 2
    nrow = snew_ref.shape[1]
    qr = _row_q((LANES, 1))
    icol = lax.broadcasted_iota(jnp.int32, (1, nrow), 1)
    new_ok = (icol <= qr) & (icol < n_new)

    xs = snew_ref[0]
    st = lax.dot_general(qbr, xs[:, 0:kw].astype(BF16), NT_DIMS, preferred_element_type=F32)
    st = jnp.where(new_ok, st + pen_ref[0, 0][:, 0:1], NEG)
    m_tot = jnp.max(st, axis=1, keepdims=True)
    for s in range(n_pt):
        m_tot = jnp.maximum(m_tot, m_ref[0, s][:, 0:1])
    pt_ = jnp.exp(st - m_tot)
    l_tot = jnp.sum(pt_, axis=1, keepdims=True)
    acc = jnp.dot(pt_.astype(BF16), xs[:, kw:KV_W].astype(BF16), preferred_element_type=F32)
    for s in range(n_pt):
        a = jnp.exp(m_ref[0, s][:, 0:1] - m_tot)
        l_tot = l_tot + a * l_ref[0, s][:, 0:1]
        acc = acc + a * acc_ref[0, s]
    o_slc = _diag_heads(acc / l_tot)

    wn = wnew_ref[0]
    nw = wst_ref.shape[2]
    s1 = jnp.dot(qbr, wst_ref[0, 0:kw, :].astype(BF16), preferred_element_type=F32)
    s2 = lax.dot_general(qbr, wn[:, 0:kw].astype(BF16), NT_DIMS, preferred_element_type=F32)
    dist1 = nw + qr - lax.broadcasted_iota(jnp.int32, (1, nw), 1)
    mask1 = (dist1 >= 0) & (dist1 < WINDOW)
    mask2 = new_ok & (qr - icol < WINDOW)
    s1 = jnp.where(mask1, s1, NEG)
    s2 = jnp.where(mask2, s2, NEG)
    mw = jnp.maximum(jnp.max(s1, axis=1, keepdims=True), jnp.max(s2, axis=1, keepdims=True))
    p1 = jnp.where(mask1, jnp.exp(s1 - mw), 0.0)
    p2 = jnp.where(mask2, jnp.exp(s2 - mw), 0.0)
    inv = 1.0 / (jnp.sum(p1, axis=1, keepdims=True) + jnp.sum(p2, axis=1, keepdims=True))
    ow = (lax.dot_general((p1 * inv).astype(BF16), wst_ref[0, kw:KV_W, :].astype(BF16), NT_DIMS,
                          preferred_element_type=F32)
          + jnp.dot((p2 * inv).astype(BF16), wn[:, kw:KV_W].astype(BF16), preferred_element_type=F32))
    o_win = _diag_heads(ow)

    gv = gt_ref[0]
    o_ref[0] = gv[:, 0:1] * ocmp_ref[0] + gv[:, 1:2] * o_slc + gv[:, 2:3] * o_win


def _attn_sample(q_flat, gates, ck, cv, slc_pool_t, page_table, slc_new, win_state_t, win_new, past):
    B, Tn, _ = q_flat.shape
    n_pg = PAGES_PER_STEP
    assert Tn == 8 and past % (PAGE_SIZE * n_pg) == 0
    n_pages = past // PAGE_SIZE
    n_pt = n_pages // n_pg
    n_past_blk = past // SEL_BLOCK
    n_blocks = n_past_blk + 1
    nb_rows = -(-n_blocks // SUBLANES) * SUBLANES
    n_cmp = past // CMP_STRIDE - 1
    Mc = ck.shape[1]
    at = jnp.asarray(_block_sum_matrix(nb_rows, Mc, n_blocks, n_cmp))
    kw = KV_W // 2

    q5 = q_flat.reshape(B, Tn, KV_HEADS, Q_PER_KV, HEAD_DIM)
    base = jnp.transpose(q5, (0, 3, 2, 1, 4))
    eye = jnp.eye(KV_HEADS, dtype=q_flat.dtype)
    qbr = (base[:, :, :, :, None, :] * eye[None, None, :, None, :, None]).reshape(B, LANES, kw)
    g5 = gates[:, :, :N_HEADS * 3].reshape(B, Tn, KV_HEADS, Q_PER_KV, 3)
    gt = jnp.pad(jnp.transpose(g5, (0, 3, 2, 1, 4)).reshape(B, LANES, 3), ((0, 0), (0, 0), (0, LANES - 3)))

    per_b = lambda *shape: pl.BlockSpec((1,) + shape, lambda bi, *_: (bi,) + (0,) * len(shape))
    assert B % S1_BATCH == 0 and S1_BATCH == Q_PER_KV
    per_sb = lambda *shape: pl.BlockSpec((S1_BATCH,) + shape, lambda bi: (bi,) + (0,) * len(shape))
    ocmp, pen = pl.pallas_call(
        functools.partial(_attn_s1_kernel, past=past, n_blocks=n_blocks, n_tiles=n_pt + 1),
        grid=(B // S1_BATCH,),
        in_specs=[per_sb(Mc, kw), per_sb(Mc, kw), per_sb(LANES, kw),
                  pl.BlockSpec((nb_rows, Mc), lambda bi: (0, 0))],
        out_specs=[per_sb(LANES, HEAD_DIM), per_sb(n_pt + 1, LANES, LANES)],
        out_shape=[jax.ShapeDtypeStruct((B, LANES, HEAD_DIM), F32),
                   jax.ShapeDtypeStruct((B, n_pt + 1, LANES, LANES), F32)],
        scratch_shapes=[pltpu.VMEM((nb_rows, LANES), F32)],
        compiler_params=_cparams(("parallel",)),
        name="attn_s1",
    )(ck, cv, qbr, at)

    expand = np.zeros((LANES, n_pg * PAGE_SIZE), np.float32)
    for jb in range(PEN_TILE):
        expand[jb, jb * SEL_BLOCK:(jb + 1) * SEL_BLOCK] = 1.0
    part = lambda *shape: pl.BlockSpec((1, 1) + shape, lambda bi, pi, ptab: (bi, pi) + (0,) * len(shape))
    m_p, l_p, acc_p = pl.pallas_call(
        functools.partial(_attn_s2_kernel, n_pg=n_pg),
        grid_spec=pltpu.PrefetchScalarGridSpec(
            num_scalar_prefetch=1, grid=(B, n_pt),
            in_specs=_page_specs(n_pg) + [pl.BlockSpec((1, LANES, kw), lambda bi, pi, ptab: (bi, 0, 0)),
                                          part(LANES, LANES),
                                          pl.BlockSpec(expand.shape, lambda bi, pi, ptab: (0, 0))],
            out_specs=[part(LANES, LANES), part(LANES, LANES), part(LANES, kw)],
        ),
        out_shape=[jax.ShapeDtypeStruct((B, n_pt, LANES, LANES), F32)] * 2
        + [jax.ShapeDtypeStruct((B, n_pt, LANES, kw), F32)],
        compiler_params=_cparams(("parallel", "parallel")),
        name="attn_s2",
    )(page_table, *([slc_pool_t] * n_pg), qbr, pen, jnp.asarray(expand, BF16))

    pad_rows = 16 - Tn
    snew = jnp.pad(slc_new, ((0, 0), (0, pad_rows), (0, 0)))
    wnew = jnp.pad(win_new, ((0, 0), (0, pad_rows), (0, 0)))
    nw = win_state_t.shape[2]
    o = pl.pallas_call(
        functools.partial(_attn_s3_kernel, n_pt=n_pt, n_new=Tn),
        grid=(B,),
        in_specs=[per_b(n_pt, LANES, LANES), per_b(n_pt, LANES, LANES), per_b(n_pt, LANES, kw),
                  pl.BlockSpec((1, 1, LANES, LANES), lambda bi: (bi, n_pt, 0, 0)),
                  per_b(LANES, HEAD_DIM), per_b(LANES, kw),
                  per_b(16, KV_W), per_b(KV_W, nw), per_b(16, KV_W), per_b(LANES, LANES)],
        out_specs=per_b(LANES, HEAD_DIM),
        out_shape=jax.ShapeDtypeStruct((B, LANES, HEAD_DIM), F32),
        compiler_params=_cparams(("parallel",)),
        name="attn_s3",
    )(m_p, l_p, acc_p, pen, ocmp, qbr, snew, win_state_t, wnew, gt)
    o = jnp.transpose(o.reshape(B, Q_PER_KV, KV_HEADS, Tn, HEAD_DIM), (0, 3, 2, 1, 4))
    return o.reshape(B, Tn, D_MODEL).astype(BF16)


def _nsa_prompt_layer(h, w_in, w_o, cw, g, b, tT):
    B, T, _ = h.shape
    cmp_kv, slc_kv, win_kv, qh, skp, sv, wk, wv, gates = _nsa_proj(h, jnp.arange(T), w_in, tT, True)
    ck, cv = _compress_prompt(cmp_kv, cw)
    o = _attn_prompt(qh, ck, cv, skp, sv, wk, wv, gates)
    h2 = _oproj_ln(h, o, w_o, g, b, tT)
    win_len = min(WINDOW, T)
    return h2, cmp_kv, slc_kv, win_kv[:, T - win_len:]


def _pages_device_layout(pool):
    return jnp.transpose(pool, (0, 2, 3, 4, 1)).reshape(pool.shape[0], KV_W, pool.shape[1])


def _nsa_sample_layer(h, cmp_pool, slc_pool, win_buf, page_table, w_in, w_o, cw, g, b):
    B, Tn, _ = h.shape
    n_pages = page_table.shape[1]
    past = n_pages * PAGE_SIZE
    cmp_kv, slc_kv, win_kv, q_flat, gates = _nsa_proj(h, past + jnp.arange(Tn), w_in, Tn, False)
    assert Tn < CMP_STRIDE
    ck, cv = _compress_pages(_pages_device_layout(cmp_pool), page_table, cw)
    o = _attn_sample(q_flat, gates, ck, cv, _pages_device_layout(slc_pool), page_table, slc_kv,
                     _pages_device_layout(win_buf), win_kv, past)
    h2 = _oproj_ln(h, o, w_o, g, b, Tn)
    buf_len = win_buf.shape[1]
    new_win = jnp.concatenate([win_buf, win_kv.reshape(B, Tn, 2, KV_HEADS, HEAD_DIM)], axis=1)[:, -buf_len:]
    return h2, cmp_kv, slc_kv, new_win.reshape(B, buf_len, KV_W)


def _row_tile(T):
    return 512 if T % 512 == 0 else T


def kernel(x_prompt, x_sample, state_pool, cache_cmp_kv, cache_slc_kv, state_win_kv, state_ffn, page_table, ln_g, ln_b, pool_w, pool_scale, nsa_w_in, nsa_w_o, cmp_w1, cmp_b1, cmp_pos, cmp_w2, cmp_b2, ffn_w_up, ffn_conv_w, ffn_conv_b, ffn_w_down):
    Bp, T, D = x_prompt.shape
    Bs, Tn, _ = x_sample.shape
    past = page_table.shape[1] * PAGE_SIZE
    kv_shape = (2, KV_HEADS, HEAD_DIM)
    tp, ts = _row_tile(T), Tn

    hp = _pool_ln(x_prompt, jnp.zeros((Bp, POOL_BUF, D), F32), 0, pool_w[0], pool_scale[0], ln_g[0, 0], ln_b[0, 0], tp)
    hs = _pool_ln(x_sample, state_pool[0], past, pool_w[0], pool_scale[0], ln_g[0, 0], ln_b[0, 0], ts)
    pool_p = jnp.concatenate([jnp.zeros((Bp, POOL_BUF, D), F32), x_prompt], axis=1)[:, -POOL_BUF:][None]
    pool_s = jnp.concatenate([state_pool[0], x_sample], axis=1)[:, -POOL_BUF:][None]
    ffn = lambda h, pre, i, t: _ffn_ln(h, pre, ffn_w_up[i], ffn_conv_w[i], ffn_conv_b[i], ffn_w_down[i],
                                       ln_g[i, 1], ln_b[i, 1], t)
    zero_pre = jnp.zeros((Bp, CONV_W - 1, 2 * D_FF), F32)
    hp, ffn_p0 = ffn(hp, zero_pre, 0, tp)
    hs, ffn_s0 = ffn(hs, state_ffn[0], 0, ts)

    cw = (cmp_w1[0], cmp_b1[0], cmp_pos[0], cmp_w2[0], cmp_b2[0])
    hp, c_p, s_p, w_p = _nsa_prompt_layer(hp, nsa_w_in[0], nsa_w_o[0], cw, ln_g[1, 0], ln_b[1, 0], tp)
    hs, c_s, s_s, w_s = _nsa_sample_layer(hs, cache_cmp_kv[0], cache_slc_kv[0], state_win_kv[0], page_table,
                                          nsa_w_in[0], nsa_w_o[0], cw, ln_g[1, 0], ln_b[1, 0])
    hp, ffn_p1 = ffn(hp, zero_pre, 1, tp)
    hs, ffn_s1 = ffn(hs, state_ffn[1], 1, ts)

    kv5 = lambda a: a.reshape(a.shape[0], a.shape[1], *kv_shape)[None]
    return (hp, hs, pool_p, pool_s, kv5(c_p), kv5(c_s), kv5(s_p), kv5(s_s), kv5(w_p), kv5(w_s),
            jnp.stack([ffn_p0, ffn_p1]), jnp.stack([ffn_s0, ffn_s1]))
```

```python
import functools

import numpy as np
import jax
import jax.numpy as jnp
from jax import lax
from jax.experimental import pallas as pl
from jax.experimental.pallas import tpu as pltpu

D_MODEL = 1024
DEPTH = 2
ALPHA = (2.0 * DEPTH) ** 0.25
LN_EPS = 1e-5
POOL_WINDOWS = (2, 4, 8, 16)
POOL_CH = D_MODEL // len(POOL_WINDOWS)
POOL_BUF = max(POOL_WINDOWS) - 1
N_HEADS = 16
KV_HEADS = 4
HEAD_DIM = 64
Q_PER_KV = N_HEADS // KV_HEADS
ROT_DIM = HEAD_DIM // 4
ROPE_THETA = 500000.0
CMP_BLOCK = 32
CMP_STRIDE = 16
CMP_HID = 128
SEL_BLOCK = 64
N_SEL = 16
WINDOW = 512
KV_W = 2 * KV_HEADS * HEAD_DIM
D_FF = 2816
CONV_W = 3
PAGE_SIZE = 128
NEG = -1e30
BIG = 1e9
LOG2E = 1.4426950408889634

LANES = 128
SUBLANES = 8
Q_TILE = 128
SLC_KT = 512
ATTN_GROUPS = 2
PAGES_PER_STEP = 16
VMEM_LIMIT = 56 * 1024 * 1024

F32 = jnp.float32
BF16 = jnp.bfloat16
NT_DIMS = (((1,), (1,)), ((), ()))


def _cparams(sem):
    return pltpu.CompilerParams(dimension_semantics=sem, vmem_limit_bytes=VMEM_LIMIT)


def _layer_norm(h, g, b):
    mu = jnp.mean(h, axis=-1, keepdims=True)
    hc = h - mu
    var = jnp.mean(hc * hc, axis=-1, keepdims=True)
    return hc * lax.rsqrt(var + LN_EPS) * g + b


def _const_spec(shape):
    nd = len(shape)
    return pl.BlockSpec(shape, lambda *_: (0,) * nd, pipeline_mode=pl.Buffered(1))


def _pool_ln_kernel(x_ref, halo_ref, pre_ref, w_ref, sc_ref, g_ref, b_ref, o_ref, buf, *, tT, pos0):
    t = pl.program_id(1)
    x = x_ref[0]
    buf[0:16, :] = jnp.where(t == 0, pre_ref[0], halo_ref[0])
    buf[16:16 + tT, :] = x
    pos = pos0 + t * tT + lax.broadcasted_iota(jnp.int32, (tT, 1), 0)
    ys = []
    for gi, win in enumerate(POOL_WINDOWS):
        c0 = gi * POOL_CH
        xg = x[:, c0:c0 + POOL_CH]
        s = xg
        for i in range(1, win):
            s = s + buf[pl.ds(16 - i, tT), c0:c0 + POOL_CH]
        cnt = jnp.minimum(pos + 1, win).astype(F32)
        d = s / cnt - xg
        ys.append(jnp.dot(d.astype(BF16), w_ref[gi], preferred_element_type=F32))
    y = jnp.concatenate(ys, axis=1) * sc_ref[...]
    o_ref[0] = _layer_norm(ALPHA * x + y, g_ref[...], b_ref[...])


def _pool_ln(x, prefix, pos0, w, scale, g, b, tT):
    B, T, D = x.shape
    nT = T // tT
    pre16 = jnp.concatenate([jnp.zeros((B, 1, D), x.dtype), prefix], axis=1)
    if T >= 16:
        halo_src = x
        per = tT // 16
        halo_spec = pl.BlockSpec((1, 16, D), lambda bi, ti: (bi, jnp.maximum(ti * per - 1, 0), 0))
    else:
        halo_src = pre16
        halo_spec = pl.BlockSpec((1, 16, D), lambda bi, ti: (bi, 0, 0))
    return pl.pallas_call(
        functools.partial(_pool_ln_kernel, tT=tT, pos0=pos0),
        grid=(B, nT),
        in_specs=[
            pl.BlockSpec((1, tT, D), lambda bi, ti: (bi, ti, 0)),
            halo_spec,
            pl.BlockSpec((1, 16, D), lambda bi, ti: (bi, 0, 0)),
            _const_spec((len(POOL_WINDOWS), POOL_CH, POOL_CH)),
            _const_spec((1, D)), _const_spec((1, D)), _const_spec((1, D)),
        ],
        out_specs=pl.BlockSpec((1, tT, D), lambda bi, ti: (bi, ti, 0)),
        out_shape=jax.ShapeDtypeStruct((B, T, D), F32),
        scratch_shapes=[pltpu.VMEM((16 + tT, D), F32)],
        compiler_params=_cparams(("parallel", "parallel")),
        name="pool_ln",
    )(x, halo_src, pre16, w.astype(BF16), scale.reshape(1, D), g.reshape(1, D), b.reshape(1, D))


FFN_CHUNK = 256


def _ffn_ln_kernel(h_ref, pre_ref, wup_ref, cw_ref, cb_ref, wdn_ref, g_ref, b_ref, o_ref, st_ref,
                   ubuf, act, *, tT):
    t = pl.program_id(1)

    @pl.when(t == 0)
    def _():
        st_ref[0] = pre_ref[0]

    h = h_ref[0]
    hb = h.astype(BF16)
    fc = FFN_CHUNK
    for j in range(D_FF // fc):
        cs = []
        for half in range(2):
            c0 = half * D_FF + j * fc
            up = jnp.dot(hb, wup_ref[:, c0:c0 + fc], preferred_element_type=F32)
            ubuf[half, 0:8, :] = st_ref[0, :, c0:c0 + fc]
            ubuf[half, 8:8 + tT, :] = up
            st_ref[0, :, c0:c0 + fc] = ubuf[half, tT:tT + 8, :]
            c = (cb_ref[:, c0:c0 + fc]
                 + cw_ref[0:1, c0:c0 + fc] * ubuf[half, pl.ds(6, tT), :]
                 + cw_ref[1:2, c0:c0 + fc] * ubuf[half, pl.ds(7, tT), :]
                 + cw_ref[2:3, c0:c0 + fc] * up)
            cs.append(c)
        a = cs[0] * jax.nn.sigmoid(cs[0]) * cs[1]
        act[:, j * fc:(j + 1) * fc] = a.astype(BF16)
    y = jnp.dot(act[...], wdn_ref[...], preferred_element_type=F32)
    o_ref[0] = _layer_norm(ALPHA * h + y, g_ref[...], b_ref[...])


def _ffn_ln(h, prefix, w_up, conv_w, conv_b, w_down, g, b, tT):
    B, T, D = h.shape
    nT = T // tT
    F2 = 2 * D_FF
    pre8 = jnp.concatenate([jnp.zeros((B, 8 - (CONV_W - 1), F2), F32), prefix], axis=1)
    out, st = pl.pallas_call(
        functools.partial(_ffn_ln_kernel, tT=tT),
        grid=(B, nT),
        in_specs=[
            pl.BlockSpec((1, tT, D), lambda bi, ti: (bi, ti, 0)),
            pl.BlockSpec((1, 8, F2), lambda bi, ti: (bi, 0, 0)),
            _const_spec((D, F2)),
            _const_spec((CONV_W, F2)),
            _const_spec((1, F2)),
            _const_spec((D_FF, D)),
            _const_spec((1, D)), _const_spec((1, D)),
        ],
        out_specs=[pl.BlockSpec((1, tT, D), lambda bi, ti: (bi, ti, 0)),
                   pl.BlockSpec((1, 8, F2), lambda bi, ti: (bi, 0, 0))],
        out_shape=[jax.ShapeDtypeStruct((B, T, D), F32), jax.ShapeDtypeStruct((B, 8, F2), F32)],
        scratch_shapes=[pltpu.VMEM((2, tT + 8, FFN_CHUNK), F32), pltpu.VMEM((tT, D_FF), BF16)],
        compiler_params=_cparams(("parallel", "arbitrary")),
        name="ffn_ln",
    )(h, pre8, w_up.astype(BF16), conv_w, conv_b.reshape(1, F2), w_down.astype(BF16),
      g.reshape(1, D), b.reshape(1, D))
    return out, st[:, 8 - (CONV_W - 1):, :]


def _ffn_ln_short_kernel(h_ref, pre_ref, wup_ref, cw_ref, cb_ref, wdn_ref, g_ref, b_ref, o_ref, st_ref, act,
                         *, seq):
    h = h_ref[...]
    hb = h.astype(BF16)
    rows = h.shape[0]
    fc = FFN_CHUNK
    t = lax.broadcasted_iota(jnp.int32, (rows, fc), 0) & (seq - 1)
    for j in range(D_FF // fc):
        cs = []
        for half in range(2):
            c0 = half * D_FF + j * fc
            up = jnp.dot(hb, wup_ref[:, c0:c0 + fc], preferred_element_type=F32)
            st_ref[:, c0:c0 + fc] = up
            pre = pre_ref[:, c0:c0 + fc]
            s1 = jnp.where(t >= 1, pltpu.roll(up, 1, axis=0), pltpu.roll(pre, 1, axis=0))
            s2 = jnp.where(t >= 2, pltpu.roll(up, 2, axis=0), pltpu.roll(pre, 2, axis=0))
            cs.append(cb_ref[:, c0:c0 + fc] + cw_ref[0:1, c0:c0 + fc] * s2 + cw_ref[1:2, c0:c0 + fc] * s1
                      + cw_ref[2:3, c0:c0 + fc] * up)
        a = cs[0] * jax.nn.sigmoid(cs[0]) * cs[1]
        act[:, j * fc:(j + 1) * fc] = a.astype(BF16)
    y = jnp.dot(act[...], wdn_ref[...], preferred_element_type=F32)
    o_ref[...] = _layer_norm(ALPHA * h + y, g_ref[...], b_ref[...])


def _ffn_ln_short(h, prefix, w_up, conv_w, conv_b, w_down, g, b):
    B, T, D = h.shape
    assert T == SUBLANES and CONV_W - 1 <= T
    F2 = 2 * D_FF
    rows = B * T
    pre8 = jnp.concatenate([jnp.zeros((B, T - (CONV_W - 1), F2), F32), prefix], axis=1)
    pre_rows = jnp.roll(pre8, -1, axis=0).reshape(rows, F2)
    full = lambda *shape: pl.BlockSpec(shape, lambda i: (0,) * len(shape))
    out, st = pl.pallas_call(
        functools.partial(_ffn_ln_short_kernel, seq=T),
        grid=(1,),
        in_specs=[_const_spec(sh) for sh in ((rows, D), (rows, F2), (D, F2), (CONV_W, F2), (1, F2), (D_FF, D),
                                             (1, D), (1, D))],
        out_specs=[full(rows, D), full(rows, F2)],
        out_shape=[jax.ShapeDtypeStruct((rows, D), F32), jax.ShapeDtypeStruct((rows, F2), F32)],
        scratch_shapes=[pltpu.VMEM((rows, D_FF), BF16)],
        compiler_params=_cparams(("arbitrary",)),
        name="ffn_ln_short",
    )(h.reshape(rows, D), pre_rows, w_up.astype(BF16), conv_w, conv_b.reshape(1, F2), w_down.astype(BF16),
      g.reshape(1, D), b.reshape(1, D))
    return out.reshape(B, T, D), st.reshape(B, T, F2)[:, T - (CONV_W - 1):, :]


def _rope_tables(pos):
    half = ROT_DIM // 2
    inv = jnp.power(ROPE_THETA, -2.0 * jnp.arange(half, dtype=F32) / ROT_DIM)
    ang = pos.astype(F32)[:, None] * inv[None, :]
    cos, sin = jnp.cos(ang), jnp.sin(ang)
    T = pos.shape[0]
    one, zero = jnp.ones((T, HEAD_DIM - ROT_DIM), F32), jnp.zeros((T, HEAD_DIM - half), F32)
    c64 = jnp.concatenate([cos, cos, one], axis=1)
    lo64 = jnp.concatenate([-sin, zero], axis=1)
    hi64 = jnp.concatenate([jnp.zeros((T, half), F32), sin, jnp.zeros((T, HEAD_DIM - ROT_DIM), F32)], axis=1)
    tile = lambda a: jnp.concatenate([a, a], axis=1)
    return tile(c64), tile(lo64), tile(hi64)


def _rope_chunk(x, c, lo, hi):
    return x * c + pltpu.roll(x, LANES - ROT_DIM // 2, axis=1) * lo + pltpu.roll(x, ROT_DIM // 2, axis=1) * hi


def _nsa_proj_kernel(h_ref, wq_ref, wkv_ref, wg_ref, c_ref, lo_ref, hi_ref, *outs, tT, head_major):
    t = pl.program_id(1)
    hb = h_ref[0].astype(BF16)
    c, lo, hi = c_ref[...], lo_ref[...], hi_ref[...]
    q = jnp.dot(hb, wq_ref[...], preferred_element_type=F32)
    kv = jnp.dot(hb, wkv_ref[...], preferred_element_type=F32)
    gl = jnp.dot(hb, wg_ref[...], preferred_element_type=F32)
    gates = jax.nn.sigmoid(gl)
    lane = lax.broadcasted_iota(jnp.int32, (tT, LANES), 1)
    low = lane < HEAD_DIM
    scale = HEAD_DIM ** -0.5 * (LOG2E if head_major else 1.0)
    qc = [_rope_chunk(q[:, i * LANES:(i + 1) * LANES], c, lo, hi) * scale for i in range(D_MODEL // LANES)]
    kvc = []
    for br in range(3):
        for i in range(KV_W // LANES):
            x = kv[:, br * KV_W + i * LANES: br * KV_W + (i + 1) * LANES]
            kvc.append(_rope_chunk(x, c, lo, hi) if i < KV_W // (2 * LANES) else x)
    if head_major:
        cmp_ref, slc_ref, win_ref, qh_ref, skp_ref, sv_ref, wk_ref, wv_ref, g_ref = outs
    else:
        cmp_ref, slc_ref, win_ref, qf_ref, g_ref = outs
    for br, ref in enumerate((cmp_ref, slc_ref, win_ref)):
        ref[0] = jnp.concatenate(kvc[br * 4:(br + 1) * 4], axis=1)
    g_ref[0] = gates
    if not head_major:
        qf_ref[0] = jnp.concatenate(qc, axis=1).astype(BF16)
        return

    def split(x):
        return jnp.where(low, x, 0.0), jnp.where(low, pltpu.roll(x, HEAD_DIM, axis=1), 0.0)

    for i in range(D_MODEL // LANES):
        a, b = split(qc[i])
        qh_ref[0, 2 * i] = a.astype(BF16)
        qh_ref[0, 2 * i + 1] = b.astype(BF16)
    blk = (t * tT + lax.broadcasted_iota(jnp.int32, (tT, LANES), 0)) // SEL_BLOCK
    onehot = jnp.where(lane - HEAD_DIM == blk, 1.0, 0.0)
    ones_col = jnp.where(lane == HEAD_DIM, 1.0, 0.0)
    for br, (kref, vref) in ((1, (skp_ref, sv_ref)), (2, (wk_ref, wv_ref))):
        for i in range(2):
            ka, kb = split(kvc[br * 4 + i])
            va, vb = split(kvc[br * 4 + 2 + i])
            if br == 1:
                ka, kb = ka + onehot, kb + onehot
            va, vb = va + ones_col, vb + ones_col
            kref[0, 2 * i] = ka.astype(BF16)
            kref[0, 2 * i + 1] = kb.astype(BF16)
            vref[0, 2 * i] = va.astype(BF16)
            vref[0, 2 * i + 1] = vb.astype(BF16)


def _nsa_proj(h, pos, w_in, tT, head_major):
    B, T, D = h.shape
    nT = T // tT
    nq = N_HEADS * HEAD_DIM
    wq = w_in[:, :nq].astype(BF16)
    wkv = w_in[:, nq:nq + 3 * KV_W].astype(BF16)
    wg = w_in[:, nq + 3 * KV_W:]
    if head_major:
        GW = KV_HEADS * LANES
        wg = jnp.pad(wg.reshape(D, KV_HEADS, Q_PER_KV * 3), ((0, 0), (0, 0), (0, LANES - Q_PER_KV * 3))).reshape(D, GW)
    else:
        GW = LANES
        wg = jnp.pad(wg, ((0, 0), (0, LANES - wg.shape[1])))
    wg = wg.astype(BF16)
    c, lo, hi = _rope_tables(pos)
    row = lambda w: pl.BlockSpec((1, tT, w), lambda bi, ti: (bi, ti, 0))
    hm = lambda n: pl.BlockSpec((1, n, tT, LANES), lambda bi, ti: (bi, 0, ti, 0))
    tab = pl.BlockSpec((tT, LANES), lambda bi, ti: (ti, 0))
    out_specs = [row(KV_W), row(KV_W), row(KV_W)]
    out_shape = [jax.ShapeDtypeStruct((B, T, KV_W), F32)] * 3
    if head_major:
        out_specs += [hm(N_HEADS)] + [hm(KV_HEADS)] * 4 + [row(GW)]
        out_shape += ([jax.ShapeDtypeStruct((B, N_HEADS, T, LANES), BF16)]
                      + [jax.ShapeDtypeStruct((B, KV_HEADS, T, LANES), BF16)] * 4
                      + [jax.ShapeDtypeStruct((B, T, GW), F32)])
    else:
        out_specs += [row(D), row(GW)]
        out_shape += [jax.ShapeDtypeStruct((B, T, D), BF16), jax.ShapeDtypeStruct((B, T, GW), F32)]
    return pl.pallas_call(
        functools.partial(_nsa_proj_kernel, tT=tT, head_major=head_major),
        grid=(B, nT),
        in_specs=[row(D), _const_spec((D, nq)), _const_spec((D, 3 * KV_W)), _const_spec((D, GW)), tab, tab, tab],
        out_specs=out_specs,
        out_shape=out_shape,
        compiler_params=_cparams(("parallel", "parallel")),
        name="nsa_proj",
    )(h, wq, wkv, wg, c, lo, hi)


def _gelu_tanh(x):
    return 0.5 * x * (1.0 + jnp.tanh(0.7978845608028654 * (x + 0.044715 * x * x * x)))


def _compress_kernel(x_ref, w1_ref, w1f_ref, pos_ref, b1_ref, w2_ref, b2_ref, ck_ref, cv_ref):
    for kvi, out_ref in enumerate((ck_ref, cv_ref)):
        b1 = b1_ref[kvi:kvi + 1, :] + jnp.dot(pos_ref[kvi], w1f_ref[kvi], preferred_element_type=F32)[0:1, :]
        for g in range(KV_HEADS):
            acc = None
            for t in range(CMP_STRIDE):
                off = t * KV_W + kvi * (KV_W // 2) + g * HEAD_DIM
                d = jnp.dot(x_ref[0, :, off:off + HEAD_DIM].astype(BF16), w1_ref[kvi, t],
                            preferred_element_type=F32)
                acc = d if acc is None else acc + d
            pr0, pr1 = acc[:, :CMP_HID], acc[:, CMP_HID:]
            hid = _gelu_tanh(pltpu.roll(pr0, 1, axis=0) + pr1 + b1)
            o = jnp.dot(hid.astype(BF16), w2_ref[kvi], preferred_element_type=F32) + b2_ref[kvi:kvi + 1, :]
            out_ref[0, g] = jnp.concatenate([o, jnp.zeros_like(o)], axis=1).astype(BF16)


def _compress_weights(cw):
    w1, b1, pos_emb, w2, b2 = cw
    assert CMP_BLOCK == 2 * CMP_STRIDE
    w1cat = jnp.concatenate([w1[:, :CMP_STRIDE], w1[:, CMP_STRIDE:]], axis=-1).astype(BF16)
    w1f = w1.reshape(2, CMP_BLOCK * HEAD_DIM, CMP_HID).astype(BF16)
    posf = jnp.broadcast_to(pos_emb.reshape(2, 1, CMP_BLOCK * HEAD_DIM), (2, SUBLANES, CMP_BLOCK * HEAD_DIM)).astype(BF16)
    return w1cat, w1f, posf, b1, w2.astype(BF16), b2


def _compress_prompt(cmp_kv, cw):
    B, T, _ = cmp_kv.shape
    n_half = T // CMP_STRIDE
    x = cmp_kv.reshape(B, n_half, CMP_STRIDE * KV_W)
    ws = _compress_weights(cw)
    o_spec = pl.BlockSpec((1, KV_HEADS, n_half, LANES), lambda bi: (bi, 0, 0, 0))
    o_shape = jax.ShapeDtypeStruct((B, KV_HEADS, n_half, LANES), BF16)
    return pl.pallas_call(
        _compress_kernel,
        grid=(B,),
        in_specs=[pl.BlockSpec((1, n_half, CMP_STRIDE * KV_W), lambda bi: (bi, 0, 0))]
        + [_const_spec(w.shape) for w in ws],
        out_specs=[o_spec, o_spec],
        out_shape=[o_shape, o_shape],
        compiler_params=_cparams(("parallel",)),
        name="compress_hm",
    )(x, *ws)


def _block_sum_matrix(n_out, n_in, n_blocks, n_cmp):
    a = np.zeros((n_out, n_in), np.float32)
    for j in range(n_blocks):
        for m in range(4 * j, 4 * j + 5):
            if 1 <= m <= n_cmp:
                a[j, m] = 1.0
    return a


def _attn_prompt_kernel(q_ref, ck_ref, cv_ref, kp_ref, vs_ref, wk_ref, wv_ref, gt_ref, at_ref, o_ref,
                        s_buf, m_run, acc_sc, oc_sc, pen_sc, *, n_cmp_rows, n_blk_rows):
    qt = pl.program_id(2)
    s0 = qt * Q_TILE
    R = Q_PER_KV * Q_TILE
    GS = ATTN_GROUPS
    Qs = [q_ref[0, gi].reshape(R, LANES) for gi in range(GS)]
    tq = s0 + (lax.broadcasted_iota(jnp.int32, (R, 1), 0) & (Q_TILE - 1))

    mrow = lax.broadcasted_iota(jnp.int32, (1, n_cmp_rows), 1)
    cend = jnp.where(mrow >= 1, mrow * CMP_STRIDE + (CMP_BLOCK - CMP_STRIDE - 1), jnp.int32(2 ** 30))
    j = lax.broadcasted_iota(jnp.int32, (n_blk_rows, Q_TILE), 0)
    tl = s0 + lax.broadcasted_iota(jnp.int32, (n_blk_rows, Q_TILE), 1)
    cur = tl // SEL_BLOCK
    valid = j * SEL_BLOCK <= tl
    forced = valid & ((j == 0) | (j == cur) | (j == cur - 1))
    scores = []
    for gi in range(GS):
        sc = lax.dot_general(Qs[gi], ck_ref[0, gi], NT_DIMS, preferred_element_type=F32)
        sc = jnp.where(cend <= tq, sc, NEG)
        e = jnp.exp2(sc - jnp.max(sc, axis=1, keepdims=True))
        inv = jnp.where(tq >= CMP_BLOCK - 1, 1.0 / jnp.sum(e, axis=1, keepdims=True), 0.0)
        p = e * inv
        oc_sc[gi] = jnp.dot(p.astype(BF16), cv_ref[0, gi], preferred_element_type=F32)
        imp = p[0:Q_TILE]
        for r in range(1, Q_PER_KV):
            imp = imp + p[r * Q_TILE:(r + 1) * Q_TILE]
        blk = lax.dot_general(at_ref[...], imp, NT_DIMS, precision=lax.Precision.HIGHEST,
                              preferred_element_type=F32)
        scores.append(jnp.where(forced, BIG, jnp.where(valid, blk, -1.0)))

    wlen = WINDOW + Q_TILE
    n_sub = wlen // Q_TILE
    w0 = pl.multiple_of(jnp.maximum(s0 - WINDOW, 0), Q_TILE)
    tri = (lax.broadcasted_iota(jnp.int32, (Q_TILE, Q_TILE), 1)
           <= lax.broadcasted_iota(jnp.int32, (Q_TILE, Q_TILE), 0))
    steady = s0 >= WINDOW
    wbias = []
    for i in range(n_sub):
        lower_ok = jnp.where(steady, i > 0, i <= qt)
        upper_ok = jnp.where(steady, i < n_sub - 1, i < qt)
        b = jnp.where(tri, jnp.where(lower_ok, 0.0, NEG), jnp.where(upper_ok, 0.0, NEG))
        wbias.append(jnp.concatenate([b] * Q_PER_KV, axis=0))
    wbias = jnp.concatenate(wbias, axis=1)
    gv = gt_ref[0]
    gcol = lambda gi, jb: jnp.concatenate(
        [gv[:, gi * LANES + 3 * r + jb:gi * LANES + 3 * r + jb + 1] for r in range(Q_PER_KV)], axis=0)
    for gi in range(GS):
        s = lax.dot_general(Qs[gi], wk_ref[0, gi, pl.ds(w0, wlen), :], NT_DIMS, preferred_element_type=F32)
        s = s + wbias
        pw = jnp.exp2(s - jnp.max(s, axis=1, keepdims=True)).astype(BF16)
        acc_w = jnp.dot(pw, wv_ref[0, gi, pl.ds(w0, wlen), :], preferred_element_type=F32)
        oc_sc[gi] = gcol(gi, 0) * oc_sc[gi] + (gcol(gi, 2) / acc_w[:, HEAD_DIM:HEAD_DIM + 1]) * acc_w

    jv = lax.broadcasted_iota(jnp.int32, (SUBLANES, Q_TILE), 0)

    def rank_rows(n_rows):
        nv = n_rows // SUBLANES
        for gi in range(GS):
            if n_rows <= N_SEL:
                pen_sc[gi] = jnp.where(valid, 0.0, NEG)
                continue
            sv = [scores[gi][v * SUBLANES:(v + 1) * SUBLANES, :] for v in range(nv)]
            cnt = [jnp.zeros((SUBLANES, Q_TILE), F32) for _ in range(nv)]
            for i in range(n_rows):
                vi, si = divmod(i, SUBLANES)
                ri = sv[vi][si:si + 1, :]
                for v in range(nv):
                    if v < vi:
                        beats = ri > sv[v]
                    elif v > vi:
                        beats = ri >= sv[v]
                    else:
                        beats = (ri > sv[v]) | ((jv > si) & (ri >= sv[v]))
                    cnt[v] = cnt[v] + jnp.where(beats, 1.0, 0.0)
            pen = jnp.where(jnp.concatenate(cnt, axis=0) < N_SEL, 0.0, NEG)
            if n_rows < n_blk_rows:
                pen = jnp.concatenate([pen, jnp.full((n_blk_rows - n_rows, Q_TILE), NEG, F32)], axis=0)
            pen_sc[gi] = jnp.where(valid, pen, NEG)

    n_poss = (s0 + Q_TILE - 1) // SEL_BLOCK + 1
    bounds = list(range(N_SEL, n_blk_rows, N_SEL)) + [n_blk_rows]
    for lo, hi in zip([0] + bounds[:-1], bounds):
        pl.when((n_poss > lo) & (n_poss <= hi))(functools.partial(rank_rows, hi))

    Qa = []
    for gi in range(GS):
        pen_rows = [jnp.zeros((HEAD_DIM, Q_TILE), F32), pen_sc[gi]]
        if n_blk_rows < HEAD_DIM:
            pen_rows.append(jnp.zeros((HEAD_DIM - n_blk_rows, Q_TILE), F32))
        pen = jnp.concatenate(pen_rows, axis=0).T
        Qa.append(Qs[gi] + jnp.concatenate([pen.astype(BF16)] * Q_PER_KV, axis=0))

    n_full = s0 // SLC_KT
    m_run[...] = jnp.full(m_run.shape, NEG, F32)
    acc_sc[...] = jnp.zeros(acc_sc.shape, F32)

    def scores_tile(kt, causal):
        k0 = pl.multiple_of(kt * SLC_KT, SLC_KT)
        for gi in range(GS):
            s = lax.dot_general(Qa[gi], kp_ref[0, gi, pl.ds(k0, SLC_KT), :], NT_DIMS, preferred_element_type=F32)
            if causal:
                kp_ = k0 + lax.broadcasted_iota(jnp.int32, (1, SLC_KT), 1)
                s = jnp.where(kp_ <= tq, s, NEG)
            s_buf[gi, kt] = s
            mx = m_run[gi]
            for c in range(SLC_KT // LANES):
                mx = jnp.maximum(mx, s[:, c * LANES:(c + 1) * LANES])
            m_run[gi] = mx

    def p1(i, carry):
        scores_tile(2 * i, False)
        scores_tile(2 * i + 1, False)
        return carry

    lax.fori_loop(0, n_full // 2, p1, 0)
    pl.when(n_full % 2 == 1)(lambda: scores_tile(n_full - 1, False))
    scores_tile(n_full, True)
    ms = [jnp.max(m_run[gi], axis=1, keepdims=True) for gi in range(GS)]

    def pv_tile(kt):
        k0 = pl.multiple_of(kt * SLC_KT, SLC_KT)
        for gi in range(GS):
            pe = jnp.exp2(s_buf[gi, kt] - ms[gi]).astype(BF16)
            acc_sc[gi] += jnp.dot(pe, vs_ref[0, gi, pl.ds(k0, SLC_KT), :], preferred_element_type=F32)

    def p2(i, carry):
        pv_tile(2 * i)
        pv_tile(2 * i + 1)
        return carry

    lax.fori_loop(0, (n_full + 1) // 2, p2, 0)
    pl.when(n_full % 2 == 0)(lambda: pv_tile(n_full))

    first_head = lax.broadcasted_iota(jnp.int32, (Q_TILE, LANES), 1) < HEAD_DIM
    for gi in range(GS):
        acc_s = acc_sc[gi]
        o = oc_sc[gi] + (gcol(gi, 1) / acc_s[:, HEAD_DIM:HEAD_DIM + 1]) * acc_s
        for pr in range(Q_PER_KV // 2):
            a = o[(2 * pr) * Q_TILE:(2 * pr + 1) * Q_TILE]
            b = pltpu.roll(o[(2 * pr + 1) * Q_TILE:(2 * pr + 2) * Q_TILE], HEAD_DIM, axis=1)
            c0 = (gi * Q_PER_KV // 2 + pr) * LANES
            o_ref[0, :, c0:c0 + LANES] = jnp.where(first_head, a, b).astype(BF16)


def _attn_prompt(qh, ck, cv, skp, sv, wk, wv, gates):
    B, _, T, _ = qh.shape
    GS = ATTN_GROUPS
    n_cmp_rows = ck.shape[2]
    n_blocks = T // SEL_BLOCK
    n_blk_rows = max(SUBLANES, n_blocks)
    assert n_blocks <= HEAD_DIM and T >= WINDOW + Q_TILE and T % SLC_KT == 0 and KV_HEADS % GS == 0
    n_cmp = T // CMP_STRIDE - CMP_BLOCK // CMP_STRIDE + 1
    at = jnp.asarray(_block_sum_matrix(n_blk_rows, n_cmp_rows, n_blocks, n_cmp))
    q5 = qh.reshape(B, KV_HEADS, Q_PER_KV, T, LANES)
    R = Q_PER_KV * Q_TILE
    full = lambda n: pl.BlockSpec((1, GS, n, LANES), lambda bi, gi, qi: (bi, gi, 0, 0),
                                  pipeline_mode=pl.Buffered(1))
    return pl.pallas_call(
        functools.partial(_attn_prompt_kernel, n_cmp_rows=n_cmp_rows, n_blk_rows=n_blk_rows),
        grid=(B, KV_HEADS // GS, T // Q_TILE),
        in_specs=[
            pl.BlockSpec((1, GS, Q_PER_KV, Q_TILE, LANES), lambda bi, gi, qi: (bi, gi, 0, qi, 0)),
            full(n_cmp_rows), full(n_cmp_rows), full(T), full(T), full(T), full(T),
            pl.BlockSpec((1, Q_TILE, GS * LANES), lambda bi, gi, qi: (bi, qi, gi)),
            pl.BlockSpec((n_blk_rows, n_cmp_rows), lambda bi, gi, qi: (0, 0)),
        ],
        out_specs=pl.BlockSpec((1, Q_TILE, GS * Q_PER_KV * HEAD_DIM), lambda bi, gi, qi: (bi, qi, gi)),
        out_shape=jax.ShapeDtypeStruct((B, T, D_MODEL), BF16),
        scratch_shapes=[pltpu.VMEM((GS, T // SLC_KT, R, SLC_KT), F32), pltpu.VMEM((GS, R, LANES), F32),
                        pltpu.VMEM((GS, R, LANES), F32), pltpu.VMEM((GS, R, LANES), F32),
                        pltpu.VMEM((GS, n_blk_rows, Q_TILE), F32)],
        compiler_params=_cparams(("parallel", "parallel", "arbitrary")),
        name="attn_prompt",
    )(q5, ck, cv, skp, sv, wk, wv, gates, at)


def _oproj_ln_kernel(h_ref, o_ref, w_ref, g_ref, b_ref, out_ref):
    y = jnp.dot(o_ref[0], w_ref[...], preferred_element_type=F32)
    out_ref[0] = _layer_norm(ALPHA * h_ref[0] + y, g_ref[...], b_ref[...])


def _oproj_ln(h, o, w_o, g, b, tT):
    B, T, D = h.shape
    row = pl.BlockSpec((1, tT, D), lambda bi, ti: (bi, ti, 0))
    return pl.pallas_call(
        _oproj_ln_kernel,
        grid=(B, T // tT),
        in_specs=[row, row, _const_spec((D, D)), _const_spec((1, D)), _const_spec((1, D))],
        out_specs=row,
        out_shape=jax.ShapeDtypeStruct((B, T, D), F32),
        compiler_params=_cparams(("parallel", "parallel")),
        name="oproj_ln",
    )(h, o, w_o.astype(BF16), g.reshape(1, D), b.reshape(1, D))


PEN_TILE = PAGES_PER_STEP * (PAGE_SIZE // SEL_BLOCK)
S1_BATCH = 4


def _compress_pg_kernel(ptab_ref, *refs, n_pg):
    pg_refs = refs[:n_pg]
    wbd_ref, w1f_ref, pos_ref, b1_ref, w2_ref, b2_ref, ck_ref, cv_ref = refs[n_pg:n_pg + 8]
    xs, carry = refs[n_pg + 8:-1], refs[-1]
    pt = pl.program_id(1)

    @pl.when(pt == 0)
    def _():
        carry[...] = jnp.zeros_like(carry)

    hp = PAGE_SIZE // CMP_STRIDE
    M = n_pg * hp
    first = lax.broadcasted_iota(jnp.int32, (M, CMP_HID), 0) == 0
    for kvi, out_ref in enumerate((ck_ref, cv_ref)):
        b1 = b1_ref[kvi:kvi + 1, :] + jnp.dot(pos_ref[kvi], w1f_ref[kvi], preferred_element_type=F32)[0:1, :]
        for pair in range(KV_HEADS // 2):
            c = kvi * (KV_HEADS // 2) + pair
            for i, r in enumerate(pg_refs):
                xs[c][i * PAGE_SIZE:(i + 1) * PAGE_SIZE, :] = r[0, c * LANES:(c + 1) * LANES, :].T
            acc = None
            for t in range(CMP_STRIDE):
                z = xs[c][pl.ds(t, M, stride=CMP_STRIDE), :].astype(BF16)
                d = jnp.dot(z, wbd_ref[kvi, t], preferred_element_type=F32)
                acc = d if acc is None else acc + d
            for gl in range(2):
                g = pair * 2 + gl
                pr0 = acc[:, gl * 2 * CMP_HID:gl * 2 * CMP_HID + CMP_HID]
                pr1 = acc[:, gl * 2 * CMP_HID + CMP_HID:(gl + 1) * 2 * CMP_HID]
                ci = kvi * KV_HEADS + g
                prev = jnp.where(first, carry[ci, 0:1, :], pltpu.roll(pr0, 1, axis=0))
                carry[ci, 0:1, :] = pr0[M - 1:M, :]
                hid = _gelu_tanh(prev + pr1 + b1)
                o = jnp.dot(hid.astype(BF16), w2_ref[kvi], preferred_element_type=F32) + b2_ref[kvi:kvi + 1, :]
                out_ref[0, :, g * HEAD_DIM:(g + 1) * HEAD_DIM] = o.astype(BF16)


def _page_specs(n_pg):
    return [pl.BlockSpec((1, KV_W, PAGE_SIZE),
                         functools.partial(lambda bi, pi, ptab, i: (ptab[bi, pi * n_pg + i], 0, 0), i=i))
            for i in range(n_pg)]


def _compress_pages(pool_t, page_table, cw):
    B, n_pages = page_table.shape
    n_pg = PAGES_PER_STEP
    hp = PAGE_SIZE // CMP_STRIDE
    w1cat, w1f, posf, b1, w2, b2 = _compress_weights(cw)
    z = jnp.zeros_like(w1cat)
    wbd = jnp.concatenate([jnp.concatenate([w1cat, z], axis=-1), jnp.concatenate([z, w1cat], axis=-1)], axis=-2)
    o_spec = pl.BlockSpec((1, n_pg * hp, KV_W // 2), lambda bi, pi, ptab: (bi, pi, 0))
    o_shape = jax.ShapeDtypeStruct((B, n_pages * hp, KV_W // 2), BF16)
    return pl.pallas_call(
        functools.partial(_compress_pg_kernel, n_pg=n_pg),
        grid_spec=pltpu.PrefetchScalarGridSpec(
            num_scalar_prefetch=1, grid=(B, n_pages // n_pg),
            in_specs=_page_specs(n_pg) + [_const_spec(w.shape) for w in (wbd, w1f, posf, b1, w2, b2)],
            out_specs=[o_spec, o_spec],
            scratch_shapes=[pltpu.VMEM((n_pg * PAGE_SIZE, LANES), F32)] * (KV_W // LANES)
            + [pltpu.VMEM((2 * KV_HEADS, SUBLANES, CMP_HID), F32)],
        ),
        out_shape=[o_shape, o_shape],
        compiler_params=_cparams(("parallel", "arbitrary")),
        name="compress_pg",
    )(page_table, *([pool_t] * n_pg), wbd, w1f, posf, b1, w2, b2)


def _row_q(shape):
    return lax.broadcasted_iota(jnp.int32, shape, 0) & 7


def _diag_heads(o_all):
    g_row = (lax.broadcasted_iota(jnp.int32, (LANES, HEAD_DIM), 0) // 8) & (KV_HEADS - 1)
    out = jnp.zeros((LANES, HEAD_DIM), F32)
    for g in range(KV_HEADS):
        out = out + jnp.where(g_row == g, o_all[:, g * HEAD_DIM:(g + 1) * HEAD_DIM], 0.0)
    return out


def _attn_s1_kernel(ck_ref, cv_ref, qbr_ref, at_ref, ocmp_ref, pen_ref, score_sc, *, past, n_blocks, n_tiles):
    n_rows = ck_ref.shape[1]
    nb_rows = at_ref.shape[0]
    mrow = lax.broadcasted_iota(jnp.int32, (1, n_rows), 1)
    cend = jnp.where(mrow >= 1, mrow * CMP_STRIDE + (CMP_BLOCK - CMP_STRIDE - 1), jnp.int32(2 ** 30))
    tq = past + _row_q((LANES, 1))
    j = lax.broadcasted_iota(jnp.int32, (nb_rows, LANES), 0)
    lane = lax.broadcasted_iota(jnp.int32, (nb_rows, LANES), 1)
    grp = lane // (LANES // Q_PER_KV)
    tl = past + (lane & 7)
    cur = tl // SEL_BLOCK
    valid = (j * SEL_BLOCK <= tl) & (j < n_blocks)
    forced = valid & ((j == 0) | (j == cur) | (j == cur - 1))
    rows = LANES // Q_PER_KV
    score = None
    for k in range(S1_BATCH):
        s = lax.dot_general(qbr_ref[k], ck_ref[k], NT_DIMS, preferred_element_type=F32)
        s = jnp.where(cend <= tq, s, NEG)
        e = jnp.exp(s - jnp.max(s, axis=1, keepdims=True))
        inv = jnp.where(tq >= CMP_BLOCK - 1, 1.0 / jnp.sum(e, axis=1, keepdims=True), 0.0)
        p = e * inv
        ocmp_ref[k] = _diag_heads(jnp.dot(p.astype(BF16), cv_ref[k], preferred_element_type=F32))
        imp = p[0:rows]
        for r in range(1, Q_PER_KV):
            imp = imp + p[r * rows:(r + 1) * rows]
        imp = jnp.concatenate([imp] * Q_PER_KV, axis=0)
        blk = lax.dot_general(at_ref[...], imp, NT_DIMS, precision=lax.Precision.HIGHEST,
                              preferred_element_type=F32)
        sk = jnp.where(forced, BIG, jnp.where(valid, blk, -1.0))
        score = sk if score is None else jnp.where(grp == k, sk, score)
    score_sc[...] = score

    def body(i, cnt):
        ri = score_sc[pl.ds(i, 1), :]
        beats = (ri > score) | ((ri == score) & (j > i))
        return cnt + jnp.where(beats, 1.0, 0.0)

    cnt = lax.fori_loop(0, n_blocks, body, jnp.zeros((nb_rows, LANES), F32))
    pen4 = jnp.where((cnt < N_SEL) & valid, 0.0, NEG)
    rolled = [pen4] + [pltpu.roll(pen4, sft * rows, axis=1) for sft in range(1, S1_BATCH)]
    n_chunks = -(-n_tiles * PEN_TILE // LANES)
    zeros = jnp.zeros((LANES, LANES - PEN_TILE), F32)
    for k in range(S1_BATCH):
        pen_t = rolled[(0 - k) % S1_BATCH]
        for m in range(1, S1_BATCH):
            pen_t = jnp.where(grp == m, rolled[(m - k) % S1_BATCH], pen_t)
        pen_t = jnp.concatenate([pen_t, jnp.full((n_chunks * LANES - nb_rows, LANES), NEG, F32)], axis=0)
        pen_r = jnp.concatenate([pen_t[c * LANES:(c + 1) * LANES, :].T for c in range(n_chunks)], axis=1)
        for t in range(n_tiles):
            pen_ref[k, t] = jnp.concatenate([pen_r[:, t * PEN_TILE:(t + 1) * PEN_TILE], zeros], axis=1)


def _attn_s2_kernel(ptab_ref, *refs, n_pg):
    pg_refs = refs[:n_pg]
    qbr_ref, pen_ref, e_ref, m_ref, l_ref, acc_ref = refs[n_pg:]
    kw = KV_W // 2
    kt = jnp.concatenate([r[0, 0:kw, :].astype(BF16) for r in pg_refs], axis=1)
    s = jnp.dot(qbr_ref[0], kt, preferred_element_type=F32)
    s = s + jnp.dot(pen_ref[0, 0].astype(BF16), e_ref[...], preferred_element_type=F32)
    m = jnp.max(s, axis=1, keepdims=True)
    pe = jnp.exp(s - m)
    l = jnp.sum(pe, axis=1, keepdims=True)
    vt = jnp.concatenate([r[0, kw:KV_W, :].astype(BF16) for r in pg_refs], axis=1)
    acc_ref[0, 0] = lax.dot_general(pe.astype(BF16), vt, NT_DIMS, preferred_element_type=F32)
    m_ref[0, 0] = jnp.broadcast_to(m, (LANES, LANES))
    l_ref[0, 0] = jnp.broadcast_to(l, (LANES, LANES))


def _attn_s3_kernel(m_ref, l_ref, acc_ref, pen_ref, ocmp_ref, qbr_ref, snew_ref, wst_ref, wnew_ref, gt_ref,
                    o_ref, *, n_pt, n_new):
    qbr = qbr_ref[0]
    kw = KV_W // 2
    nrow = snew_ref.shape[1]
    qr = _row_q((LANES, 1))
    icol = lax.broadcasted_iota(jnp.int32, (1, nrow), 1)
    new_ok = (icol <= qr) & (icol < n_new)

    xs = snew_ref[0]
    st = lax.dot_general(qbr, xs[:, 0:kw].astype(BF16), NT_DIMS, preferred_element_type=F32)
    st = jnp.where(new_ok, st + pen_ref[0, 0][:, 0:1], NEG)
    m_tot = jnp.max(st, axis=1, keepdims=True)
    for s in range(n_pt):
        m_tot = jnp.maximum(m_tot, m_ref[0, s][:, 0:1])
    pt_ = jnp.exp(st - m_tot)
    l_tot = jnp.sum(pt_, axis=1, keepdims=True)
    acc = jnp.dot(pt_.astype(BF16), xs[:, kw:KV_W].astype(BF16), preferred_element_type=F32)
    for s in range(n_pt):
        a = jnp.exp(m_ref[0, s][:, 0:1] - m_tot)
        l_tot = l_tot + a * l_ref[0, s][:, 0:1]
        acc = acc + a * acc_ref[0, s]
    o_slc = _diag_heads(acc / l_tot)

    wn = wnew_ref[0]
    nw = wst_ref.shape[2]
    s1 = jnp.dot(qbr, wst_ref[0, 0:kw, :].astype(BF16), preferred_element_type=F32)
    s2 = lax.dot_general(qbr, wn[:, 0:kw].astype(BF16), NT_DIMS, preferred_element_type=F32)
    dist1 = nw + qr - lax.broadcasted_iota(jnp.int32, (1, nw), 1)
    mask1 = (dist1 >= 0) & (dist1 < WINDOW)
    mask2 = new_ok & (qr - icol < WINDOW)
    s1 = jnp.where(mask1, s1, NEG)
    s2 = jnp.where(mask2, s2, NEG)
    mw = jnp.maximum(jnp.max(s1, axis=1, keepdims=True), jnp.max(s2, axis=1, keepdims=True))
    p1 = jnp.where(mask1, jnp.exp(s1 - mw), 0.0)
    p2 = jnp.where(mask2, jnp.exp(s2 - mw), 0.0)
    inv = 1.0 / (jnp.sum(p1, axis=1, keepdims=True) + jnp.sum(p2, axis=1, keepdims=True))
    ow = (lax.dot_general((p1 * inv).astype(BF16), wst_ref[0, kw:KV_W, :].astype(BF16), NT_DIMS,
                          preferred_element_type=F32)
          + jnp.dot((p2 * inv).astype(BF16), wn[:, kw:KV_W].astype(BF16), preferred_element_type=F32))
    o_win = _diag_heads(ow)

    gv = gt_ref[0]
    o_ref[0] = gv[:, 0:1] * ocmp_ref[0] + gv[:, 1:2] * o_slc + gv[:, 2:3] * o_win


def _attn_sample(q_flat, gates, ck, cv, slc_pool_t, page_table, slc_new, win_state_t, win_new, past):
    B, Tn, _ = q_flat.shape
    n_pg = PAGES_PER_STEP
    assert Tn == 8 and past % (PAGE_SIZE * n_pg) == 0
    n_pages = past // PAGE_SIZE
    n_pt = n_pages // n_pg
    n_past_blk = past // SEL_BLOCK
    n_blocks = n_past_blk + 1
    nb_rows = -(-n_blocks // SUBLANES) * SUBLANES
    n_cmp = past // CMP_STRIDE - 1
    Mc = ck.shape[1]
    at = jnp.asarray(_block_sum_matrix(nb_rows, Mc, n_blocks, n_cmp))
    kw = KV_W // 2

    q5 = q_flat.reshape(B, Tn, KV_HEADS, Q_PER_KV, HEAD_DIM)
    base = jnp.transpose(q5, (0, 3, 2, 1, 4))
    eye = jnp.eye(KV_HEADS, dtype=q_flat.dtype)
    qbr = (base[:, :, :, :, None, :] * eye[None, None, :, None, :, None]).reshape(B, LANES, kw)
    g5 = gates[:, :, :N_HEADS * 3].reshape(B, Tn, KV_HEADS, Q_PER_KV, 3)
    gt = jnp.pad(jnp.transpose(g5, (0, 3, 2, 1, 4)).reshape(B, LANES, 3), ((0, 0), (0, 0), (0, LANES - 3)))

    per_b = lambda *shape: pl.BlockSpec((1,) + shape, lambda bi, *_: (bi,) + (0,) * len(shape))
    assert B % S1_BATCH == 0 and S1_BATCH == Q_PER_KV
    per_sb = lambda *shape: pl.BlockSpec((S1_BATCH,) + shape, lambda bi: (bi,) + (0,) * len(shape))
    ocmp, pen = pl.pallas_call(
        functools.partial(_attn_s1_kernel, past=past, n_blocks=n_blocks, n_tiles=n_pt + 1),
        grid=(B // S1_BATCH,),
        in_specs=[per_sb(Mc, kw), per_sb(Mc, kw), per_sb(LANES, kw),
                  pl.BlockSpec((nb_rows, Mc), lambda bi: (0, 0))],
        out_specs=[per_sb(LANES, HEAD_DIM), per_sb(n_pt + 1, LANES, LANES)],
        out_shape=[jax.ShapeDtypeStruct((B, LANES, HEAD_DIM), F32),
                   jax.ShapeDtypeStruct((B, n_pt + 1, LANES, LANES), F32)],
        scratch_shapes=[pltpu.VMEM((nb_rows, LANES), F32)],
        compiler_params=_cparams(("parallel",)),
        name="attn_s1",
    )(ck, cv, qbr, at)

    expand = np.zeros((LANES, n_pg * PAGE_SIZE), np.float32)
    for jb in range(PEN_TILE):
        expand[jb, jb * SEL_BLOCK:(jb + 1) * SEL_BLOCK] = 1.0
    part = lambda *shape: pl.BlockSpec((1, 1) + shape, lambda bi, pi, ptab: (bi, pi) + (0,) * len(shape))
    m_p, l_p, acc_p = pl.pallas_call(
        functools.partial(_attn_s2_kernel, n_pg=n_pg),
        grid_spec=pltpu.PrefetchScalarGridSpec(
            num_scalar_prefetch=1, grid=(B, n_pt),
            in_specs=_page_specs(n_pg) + [pl.BlockSpec((1, LANES, kw), lambda bi, pi, ptab: (bi, 0, 0)),
                                          part(LANES, LANES),
                                          pl.BlockSpec(expand.shape, lambda bi, pi, ptab: (0, 0))],
            out_specs=[part(LANES, LANES), part(LANES, LANES), part(LANES, kw)],
        ),
        out_shape=[jax.ShapeDtypeStruct((B, n_pt, LANES, LANES), F32)] * 2
        + [jax.ShapeDtypeStruct((B, n_pt, LANES, kw), F32)],
        compiler_params=_cparams(("parallel", "parallel")),
        name="attn_s2",
    )(page_table, *([slc_pool_t] * n_pg), qbr, pen, jnp.asarray(expand, BF16))

    pad_rows = 16 - Tn
    snew = jnp.pad(slc_new, ((0, 0), (0, pad_rows), (0, 0)))
    wnew = jnp.pad(win_new, ((0, 0), (0, pad_rows), (0, 0)))
    nw = win_state_t.shape[2]
    o = pl.pallas_call(
        functools.partial(_attn_s3_kernel, n_pt=n_pt, n_new=Tn),
        grid=(B,),
        in_specs=[per_b(n_pt, LANES, LANES), per_b(n_pt, LANES, LANES), per_b(n_pt, LANES, kw),
                  pl.BlockSpec((1, 1, LANES, LANES), lambda bi: (bi, n_pt, 0, 0)),
                  per_b(LANES, HEAD_DIM), per_b(LANES, kw),
                  per_b(16, KV_W), per_b(KV_W, nw), per_b(16, KV_W), per_b(LANES, LANES)],
        out_specs=per_b(LANES, HEAD_DIM),
        out_shape=jax.ShapeDtypeStruct((B, LANES, HEAD_DIM), F32),
        compiler_params=_cparams(("parallel",)),
        name="attn_s3",
    )(m_p, l_p, acc_p, pen, ocmp, qbr, snew, win_state_t, wnew, gt)
    o = jnp.transpose(o.reshape(B, Q_PER_KV, KV_HEADS, Tn, HEAD_DIM), (0, 3, 2, 1, 4))
    return o.reshape(B, Tn, D_MODEL).astype(BF16)


def _nsa_prompt_layer(h, w_in, w_o, cw, g, b, tT):
    B, T, _ = h.shape
    cmp_kv, slc_kv, win_kv, qh, skp, sv, wk, wv, gates = _nsa_proj(h, jnp.arange(T), w_in, tT, True)
    ck, cv = _compress_prompt(cmp_kv, cw)
    o = _attn_prompt(qh, ck, cv, skp, sv, wk, wv, gates)
    h2 = _oproj_ln(h, o, w_o, g, b, tT)
    win_len = min(WINDOW, T)
    return h2, cmp_kv, slc_kv, win_kv[:, T - win_len:]


def _pages_device_layout(pool):
    return jnp.transpose(pool, (0, 2, 3, 4, 1)).reshape(pool.shape[0], KV_W, pool.shape[1])


def _nsa_sample_layer(h, cmp_pool, slc_pool, win_buf, page_table, w_in, w_o, cw, g, b):
    B, Tn, _ = h.shape
    n_pages = page_table.shape[1]
    past = n_pages * PAGE_SIZE
    cmp_kv, slc_kv, win_kv, q_flat, gates = _nsa_proj(h, past + jnp.arange(Tn), w_in, Tn, False)
    assert Tn < CMP_STRIDE
    ck, cv = _compress_pages(_pages_device_layout(cmp_pool), page_table, cw)
    o = _attn_sample(q_flat, gates, ck, cv, _pages_device_layout(slc_pool), page_table, slc_kv,
                     _pages_device_layout(win_buf), win_kv, past)
    h2 = _oproj_ln(h, o, w_o, g, b, Tn)
    buf_len = win_buf.shape[1]
    new_win = jnp.concatenate([win_buf, win_kv.reshape(B, Tn, 2, KV_HEADS, HEAD_DIM)], axis=1)[:, -buf_len:]
    return h2, cmp_kv, slc_kv, new_win.reshape(B, buf_len, KV_W)


def _row_tile(T):
    return 512 if T % 512 == 0 else T


def kernel(x_prompt, x_sample, state_pool, cache_cmp_kv, cache_slc_kv, state_win_kv, state_ffn, page_table, ln_g, ln_b, pool_w, pool_scale, nsa_w_in, nsa_w_o, cmp_w1, cmp_b1, cmp_pos, cmp_w2, cmp_b2, ffn_w_up, ffn_conv_w, ffn_conv_b, ffn_w_down):
    Bp, T, D = x_prompt.shape
    Bs, Tn, _ = x_sample.shape
    past = page_table.shape[1] * PAGE_SIZE
    kv_shape = (2, KV_HEADS, HEAD_DIM)
    tp, ts = _row_tile(T), Tn

    hp = _pool_ln(x_prompt, jnp.zeros((Bp, POOL_BUF, D), F32), 0, pool_w[0], pool_scale[0], ln_g[0, 0], ln_b[0, 0], tp)
    hs = _pool_ln(x_sample, state_pool[0], past, pool_w[0], pool_scale[0], ln_g[0, 0], ln_b[0, 0], ts)
    pool_p = jnp.concatenate([jnp.zeros((Bp, POOL_BUF, D), F32), x_prompt], axis=1)[:, -POOL_BUF:][None]
    pool_s = jnp.concatenate([state_pool[0], x_sample], axis=1)[:, -POOL_BUF:][None]
    ffn = lambda h, pre, i, t: _ffn_ln(h, pre, ffn_w_up[i], ffn_conv_w[i], ffn_conv_b[i], ffn_w_down[i],
                                       ln_g[i, 1], ln_b[i, 1], t)
    zero_pre = jnp.zeros((Bp, CONV_W - 1, 2 * D_FF), F32)
    hp, ffn_p0 = ffn(hp, zero_pre, 0, tp)
    ffn_s = lambda h, pre, i: _ffn_ln_short(h, pre, ffn_w_up[i], ffn_conv_w[i], ffn_conv_b[i], ffn_w_down[i],
                                            ln_g[i, 1], ln_b[i, 1])
    hs, ffn_s0 = ffn_s(hs, state_ffn[0], 0)

    cw = (cmp_w1[0], cmp_b1[0], cmp_pos[0], cmp_w2[0], cmp_b2[0])
    hp, c_p, s_p, w_p = _nsa_prompt_layer(hp, nsa_w_in[0], nsa_w_o[0], cw, ln_g[1, 0], ln_b[1, 0], tp)
    hs, c_s, s_s, w_s = _nsa_sample_layer(hs, cache_cmp_kv[0], cache_slc_kv[0], state_win_kv[0], page_table,
                                          nsa_w_in[0], nsa_w_o[0], cw, ln_g[1, 0], ln_b[1, 0])
    hp, ffn_p1 = ffn(hp, zero_pre, 1, tp)
    hs, ffn_s1 = ffn_s(hs, state_ffn[1], 1)

    kv5 = lambda a: a.reshape(a.shape[0], a.shape[1], *kv_shape)[None]
    return (hp, hs, pool_p, pool_s, kv5(c_p), kv5(c_s), kv5(s_p), kv5(s_s), kv5(w_p), kv5(w_s),
            jnp.stack([ffn_p0, ffn_p1]), jnp.stack([ffn_s0, ffn_s1]))
```

```python
import functools

import numpy as np
import jax
import jax.numpy as jnp
from jax import lax
from jax.experimental import pallas as pl
from jax.experimental.pallas import tpu as pltpu

D_MODEL = 1024
DEPTH = 2
ALPHA = (2.0 * DEPTH) ** 0.25
LN_EPS = 1e-5
POOL_WINDOWS = (2, 4, 8, 16)
POOL_CH = D_MODEL // len(POOL_WINDOWS)
POOL_BUF = max(POOL_WINDOWS) - 1
N_HEADS = 16
KV_HEADS = 4
HEAD_DIM = 64
Q_PER_KV = N_HEADS // KV_HEADS
ROT_DIM = HEAD_DIM // 4
ROPE_THETA = 500000.0
CMP_BLOCK = 32
CMP_STRIDE = 16
CMP_HID = 128
SEL_BLOCK = 64
N_SEL = 16
WINDOW = 512
KV_W = 2 * KV_HEADS * HEAD_DIM
D_FF = 2816
CONV_W = 3
PAGE_SIZE = 128
NEG = -1e30
BIG = 1e9
LOG2E = 1.4426950408889634

LANES = 128
SUBLANES = 8
Q_TILE = 128
SLC_KT = 512
ATTN_GROUPS = 2
PAGES_PER_STEP = 16
CMP_PAGES_PER_STEP = 32
CMP_TGROUP = 4
VMEM_LIMIT = 56 * 1024 * 1024

F32 = jnp.float32
BF16 = jnp.bfloat16
NT_DIMS = (((1,), (1,)), ((), ()))


def _cparams(sem):
    return pltpu.CompilerParams(dimension_semantics=sem, vmem_limit_bytes=VMEM_LIMIT)


def _layer_norm(h, g, b):
    mu = jnp.mean(h, axis=-1, keepdims=True)
    hc = h - mu
    var = jnp.mean(hc * hc, axis=-1, keepdims=True)
    return hc * lax.rsqrt(var + LN_EPS) * g + b


def _const_spec(shape):
    nd = len(shape)
    return pl.BlockSpec(shape, lambda *_: (0,) * nd, pipeline_mode=pl.Buffered(1))


def _pool_ln_kernel(x_ref, halo_ref, pre_ref, w_ref, sc_ref, g_ref, b_ref, o_ref, buf, *, tT, pos0):
    t = pl.program_id(1)
    x = x_ref[0]
    buf[0:16, :] = jnp.where(t == 0, pre_ref[0], halo_ref[0])
    buf[16:16 + tT, :] = x
    pos = pos0 + t * tT + lax.broadcasted_iota(jnp.int32, (tT, 1), 0)
    ys = []
    for gi, win in enumerate(POOL_WINDOWS):
        c0 = gi * POOL_CH
        xg = x[:, c0:c0 + POOL_CH]
        s = xg
        for i in range(1, win):
            s = s + buf[pl.ds(16 - i, tT), c0:c0 + POOL_CH]
        cnt = jnp.minimum(pos + 1, win).astype(F32)
        d = s / cnt - xg
        ys.append(jnp.dot(d.astype(BF16), w_ref[gi], preferred_element_type=F32))
    y = jnp.concatenate(ys, axis=1) * sc_ref[...]
    o_ref[0] = _layer_norm(ALPHA * x + y, g_ref[...], b_ref[...])


def _pool_ln(x, prefix, pos0, w, scale, g, b, tT):
    B, T, D = x.shape
    nT = T // tT
    pre16 = jnp.concatenate([jnp.zeros((B, 1, D), x.dtype), prefix], axis=1)
    if T >= 16:
        halo_src = x
        per = tT // 16
        halo_spec = pl.BlockSpec((1, 16, D), lambda bi, ti: (bi, jnp.maximum(ti * per - 1, 0), 0))
    else:
        halo_src = pre16
        halo_spec = pl.BlockSpec((1, 16, D), lambda bi, ti: (bi, 0, 0))
    return pl.pallas_call(
        functools.partial(_pool_ln_kernel, tT=tT, pos0=pos0),
        grid=(B, nT),
        in_specs=[
            pl.BlockSpec((1, tT, D), lambda bi, ti: (bi, ti, 0)),
            halo_spec,
            pl.BlockSpec((1, 16, D), lambda bi, ti: (bi, 0, 0)),
            _const_spec((len(POOL_WINDOWS), POOL_CH, POOL_CH)),
            _const_spec((1, D)), _const_spec((1, D)), _const_spec((1, D)),
        ],
        out_specs=pl.BlockSpec((1, tT, D), lambda bi, ti: (bi, ti, 0)),
        out_shape=jax.ShapeDtypeStruct((B, T, D), F32),
        scratch_shapes=[pltpu.VMEM((16 + tT, D), F32)],
        compiler_params=_cparams(("parallel", "parallel")),
        name="pool_ln",
    )(x, halo_src, pre16, w.astype(BF16), scale.reshape(1, D), g.reshape(1, D), b.reshape(1, D))


FFN_CHUNK = 256


def _ffn_ln_kernel(h_ref, pre_ref, wup_ref, cw_ref, cb_ref, wdn_ref, g_ref, b_ref, o_ref, st_ref,
                   ubuf, act, *, tT):
    t = pl.program_id(1)

    @pl.when(t == 0)
    def _():
        st_ref[0] = pre_ref[0]

    h = h_ref[0]
    hb = h.astype(BF16)
    fc = FFN_CHUNK
    for j in range(D_FF // fc):
        cs = []
        for half in range(2):
            c0 = half * D_FF + j * fc
            up = jnp.dot(hb, wup_ref[:, c0:c0 + fc], preferred_element_type=F32)
            ubuf[half, 0:8, :] = st_ref[0, :, c0:c0 + fc]
            ubuf[half, 8:8 + tT, :] = up
            st_ref[0, :, c0:c0 + fc] = ubuf[half, tT:tT + 8, :]
            c = (cb_ref[:, c0:c0 + fc]
                 + cw_ref[0:1, c0:c0 + fc] * ubuf[half, pl.ds(6, tT), :]
                 + cw_ref[1:2, c0:c0 + fc] * ubuf[half, pl.ds(7, tT), :]
                 + cw_ref[2:3, c0:c0 + fc] * up)
            cs.append(c)
        a = cs[0] * jax.nn.sigmoid(cs[0]) * cs[1]
        act[:, j * fc:(j + 1) * fc] = a.astype(BF16)
    y = jnp.dot(act[...], wdn_ref[...], preferred_element_type=F32)
    o_ref[0] = _layer_norm(ALPHA * h + y, g_ref[...], b_ref[...])


def _ffn_ln(h, prefix, w_up, conv_w, conv_b, w_down, g, b, tT):
    B, T, D = h.shape
    nT = T // tT
    F2 = 2 * D_FF
    pre8 = jnp.concatenate([jnp.zeros((B, 8 - (CONV_W - 1), F2), F32), prefix], axis=1)
    out, st = pl.pallas_call(
        functools.partial(_ffn_ln_kernel, tT=tT),
        grid=(B, nT),
        in_specs=[
            pl.BlockSpec((1, tT, D), lambda bi, ti: (bi, ti, 0)),
            pl.BlockSpec((1, 8, F2), lambda bi, ti: (bi, 0, 0)),
            _const_spec((D, F2)),
            _const_spec((CONV_W, F2)),
            _const_spec((1, F2)),
            _const_spec((D_FF, D)),
            _const_spec((1, D)), _const_spec((1, D)),
        ],
        out_specs=[pl.BlockSpec((1, tT, D), lambda bi, ti: (bi, ti, 0)),
                   pl.BlockSpec((1, 8, F2), lambda bi, ti: (bi, 0, 0))],
        out_shape=[jax.ShapeDtypeStruct((B, T, D), F32), jax.ShapeDtypeStruct((B, 8, F2), F32)],
        scratch_shapes=[pltpu.VMEM((2, tT + 8, FFN_CHUNK), F32), pltpu.VMEM((tT, D_FF), BF16)],
        compiler_params=_cparams(("parallel", "arbitrary")),
        name="ffn_ln",
    )(h, pre8, w_up.astype(BF16), conv_w, conv_b.reshape(1, F2), w_down.astype(BF16),
      g.reshape(1, D), b.reshape(1, D))
    return out, st[:, 8 - (CONV_W - 1):, :]


def _ffn_ln_short_kernel(h_ref, pre_ref, wup_ref, cw_ref, cb_ref, wdn_ref, g_ref, b_ref, o_ref, st_ref, act,
                         *, seq):
    h = h_ref[...]
    hb = h.astype(BF16)
    rows = h.shape[0]
    fc = FFN_CHUNK
    t = lax.broadcasted_iota(jnp.int32, (rows, fc), 0) & (seq - 1)
    for j in range(D_FF // fc):
        cs = []
        for half in range(2):
            c0 = half * D_FF + j * fc
            up = jnp.dot(hb, wup_ref[:, c0:c0 + fc], preferred_element_type=F32)
            st_ref[:, c0:c0 + fc] = up
            pre = pre_ref[:, c0:c0 + fc]
            s1 = jnp.where(t >= 1, pltpu.roll(up, 1, axis=0), pltpu.roll(pre, 1, axis=0))
            s2 = jnp.where(t >= 2, pltpu.roll(up, 2, axis=0), pltpu.roll(pre, 2, axis=0))
            cs.append(cb_ref[:, c0:c0 + fc] + cw_ref[0:1, c0:c0 + fc] * s2 + cw_ref[1:2, c0:c0 + fc] * s1
                      + cw_ref[2:3, c0:c0 + fc] * up)
        a = cs[0] * jax.nn.sigmoid(cs[0]) * cs[1]
        act[:, j * fc:(j + 1) * fc] = a.astype(BF16)
    y = jnp.dot(act[...], wdn_ref[...], preferred_element_type=F32)
    o_ref[...] = _layer_norm(ALPHA * h + y, g_ref[...], b_ref[...])


def _ffn_ln_short(h, prefix, w_up, conv_w, conv_b, w_down, g, b):
    B, T, D = h.shape
    assert T == SUBLANES and CONV_W - 1 <= T
    F2 = 2 * D_FF
    rows = B * T
    pre8 = jnp.concatenate([jnp.zeros((B, T - (CONV_W - 1), F2), F32), prefix], axis=1)
    pre_rows = jnp.roll(pre8, -1, axis=0).reshape(rows, F2)
    full = lambda *shape: pl.BlockSpec(shape, lambda i: (0,) * len(shape))
    out, st = pl.pallas_call(
        functools.partial(_ffn_ln_short_kernel, seq=T),
        grid=(1,),
        in_specs=[_const_spec(sh) for sh in ((rows, D), (rows, F2), (D, F2), (CONV_W, F2), (1, F2), (D_FF, D),
                                             (1, D), (1, D))],
        out_specs=[full(rows, D), full(rows, F2)],
        out_shape=[jax.ShapeDtypeStruct((rows, D), F32), jax.ShapeDtypeStruct((rows, F2), F32)],
        scratch_shapes=[pltpu.VMEM((rows, D_FF), BF16)],
        compiler_params=_cparams(("arbitrary",)),
        name="ffn_ln_short",
    )(h.reshape(rows, D), pre_rows, w_up.astype(BF16), conv_w, conv_b.reshape(1, F2), w_down.astype(BF16),
      g.reshape(1, D), b.reshape(1, D))
    return out.reshape(B, T, D), st.reshape(B, T, F2)[:, T - (CONV_W - 1):, :]


def _rope_tables(pos):
    half = ROT_DIM // 2
    inv = jnp.power(ROPE_THETA, -2.0 * jnp.arange(half, dtype=F32) / ROT_DIM)
    ang = pos.astype(F32)[:, None] * inv[None, :]
    cos, sin = jnp.cos(ang), jnp.sin(ang)
    T = pos.shape[0]
    one, zero = jnp.ones((T, HEAD_DIM - ROT_DIM), F32), jnp.zeros((T, HEAD_DIM - half), F32)
    c64 = jnp.concatenate([cos, cos, one], axis=1)
    lo64 = jnp.concatenate([-sin, zero], axis=1)
    hi64 = jnp.concatenate([jnp.zeros((T, half), F32), sin, jnp.zeros((T, HEAD_DIM - ROT_DIM), F32)], axis=1)
    tile = lambda a: jnp.concatenate([a, a], axis=1)
    return tile(c64), tile(lo64), tile(hi64)


def _rope_chunk(x, c, lo, hi):
    return x * c + pltpu.roll(x, LANES - ROT_DIM // 2, axis=1) * lo + pltpu.roll(x, ROT_DIM // 2, axis=1) * hi


def _nsa_proj_kernel(h_ref, wq_ref, wkv_ref, wg_ref, c_ref, lo_ref, hi_ref, *outs, tT, head_major):
    t = pl.program_id(1)
    hb = h_ref[0].astype(BF16)
    c, lo, hi = c_ref[...], lo_ref[...], hi_ref[...]
    q = jnp.dot(hb, wq_ref[...], preferred_element_type=F32)
    kv = jnp.dot(hb, wkv_ref[...], preferred_element_type=F32)
    gl = jnp.dot(hb, wg_ref[...], preferred_element_type=F32)
    gates = jax.nn.sigmoid(gl)
    lane = lax.broadcasted_iota(jnp.int32, (tT, LANES), 1)
    low = lane < HEAD_DIM
    scale = HEAD_DIM ** -0.5 * (LOG2E if head_major else 1.0)
    qc = [_rope_chunk(q[:, i * LANES:(i + 1) * LANES], c, lo, hi) * scale for i in range(D_MODEL // LANES)]
    kvc = []
    for br in range(3):
        for i in range(KV_W // LANES):
            x = kv[:, br * KV_W + i * LANES: br * KV_W + (i + 1) * LANES]
            kvc.append(_rope_chunk(x, c, lo, hi) if i < KV_W // (2 * LANES) else x)
    if head_major:
        cmp_ref, slc_ref, win_ref, qh_ref, skp_ref, sv_ref, wk_ref, wv_ref, g_ref = outs
    else:
        cmp_ref, slc_ref, win_ref, qf_ref, g_ref = outs
    for br, ref in enumerate((cmp_ref, slc_ref, win_ref)):
        ref[0] = jnp.concatenate(kvc[br * 4:(br + 1) * 4], axis=1)
    g_ref[0] = gates
    if not head_major:
        qf_ref[0] = jnp.concatenate(qc, axis=1).astype(BF16)
        return

    def split(x):
        return jnp.where(low, x, 0.0), jnp.where(low, pltpu.roll(x, HEAD_DIM, axis=1), 0.0)

    for i in range(D_MODEL // LANES):
        a, b = split(qc[i])
        qh_ref[0, 2 * i] = a.astype(BF16)
        qh_ref[0, 2 * i + 1] = b.astype(BF16)
    blk = (t * tT + lax.broadcasted_iota(jnp.int32, (tT, LANES), 0)) // SEL_BLOCK
    onehot = jnp.where(lane - HEAD_DIM == blk, 1.0, 0.0)
    ones_col = jnp.where(lane == HEAD_DIM, 1.0, 0.0)
    for br, (kref, vref) in ((1, (skp_ref, sv_ref)), (2, (wk_ref, wv_ref))):
        for i in range(2):
            ka, kb = split(kvc[br * 4 + i])
            va, vb = split(kvc[br * 4 + 2 + i])
            if br == 1:
                ka, kb = ka + onehot, kb + onehot
            va, vb = va + ones_col, vb + ones_col
            kref[0, 2 * i] = ka.astype(BF16)
            kref[0, 2 * i + 1] = kb.astype(BF16)
            vref[0, 2 * i] = va.astype(BF16)
            vref[0, 2 * i + 1] = vb.astype(BF16)


def _nsa_proj(h, pos, w_in, tT, head_major):
    B, T, D = h.shape
    nT = T // tT
    nq = N_HEADS * HEAD_DIM
    wq = w_in[:, :nq].astype(BF16)
    wkv = w_in[:, nq:nq + 3 * KV_W].astype(BF16)
    wg = w_in[:, nq + 3 * KV_W:]
    if head_major:
        GW = KV_HEADS * LANES
        wg = jnp.pad(wg.reshape(D, KV_HEADS, Q_PER_KV * 3), ((0, 0), (0, 0), (0, LANES - Q_PER_KV * 3))).reshape(D, GW)
    else:
        GW = LANES
        wg = jnp.pad(wg, ((0, 0), (0, LANES - wg.shape[1])))
    wg = wg.astype(BF16)
    c, lo, hi = _rope_tables(pos)
    row = lambda w: pl.BlockSpec((1, tT, w), lambda bi, ti: (bi, ti, 0))
    hm = lambda n: pl.BlockSpec((1, n, tT, LANES), lambda bi, ti: (bi, 0, ti, 0))
    tab = pl.BlockSpec((tT, LANES), lambda bi, ti: (ti, 0))
    out_specs = [row(KV_W), row(KV_W), row(KV_W)]
    out_shape = [jax.ShapeDtypeStruct((B, T, KV_W), F32)] * 3
    if head_major:
        out_specs += [hm(N_HEADS)] + [hm(KV_HEADS)] * 4 + [row(GW)]
        out_shape += ([jax.ShapeDtypeStruct((B, N_HEADS, T, LANES), BF16)]
                      + [jax.ShapeDtypeStruct((B, KV_HEADS, T, LANES), BF16)] * 4
                      + [jax.ShapeDtypeStruct((B, T, GW), F32)])
    else:
        out_specs += [row(D), row(GW)]
        out_shape += [jax.ShapeDtypeStruct((B, T, D), BF16), jax.ShapeDtypeStruct((B, T, GW), F32)]
    return pl.pallas_call(
        functools.partial(_nsa_proj_kernel, tT=tT, head_major=head_major),
        grid=(B, nT),
        in_specs=[row(D), _const_spec((D, nq)), _const_spec((D, 3 * KV_W)), _const_spec((D, GW)), tab, tab, tab],
        out_specs=out_specs,
        out_shape=out_shape,
        compiler_params=_cparams(("parallel", "parallel")),
        name="nsa_proj",
    )(h, wq, wkv, wg, c, lo, hi)


def _gelu_tanh(x):
    return 0.5 * x * (1.0 + jnp.tanh(0.7978845608028654 * (x + 0.044715 * x * x * x)))


def _compress_kernel(x_ref, w1_ref, w1f_ref, pos_ref, b1_ref, w2_ref, b2_ref, ck_ref, cv_ref):
    for kvi, out_ref in enumerate((ck_ref, cv_ref)):
        b1 = b1_ref[kvi:kvi + 1, :] + jnp.dot(pos_ref[kvi], w1f_ref[kvi], preferred_element_type=F32)[0:1, :]
        for g in range(KV_HEADS):
            acc = None
            for t in range(CMP_STRIDE):
                off = t * KV_W + kvi * (KV_W // 2) + g * HEAD_DIM
                d = jnp.dot(x_ref[0, :, off:off + HEAD_DIM].astype(BF16), w1_ref[kvi, t],
                            preferred_element_type=F32)
                acc = d if acc is None else acc + d
            pr0, pr1 = acc[:, :CMP_HID], acc[:, CMP_HID:]
            hid = _gelu_tanh(pltpu.roll(pr0, 1, axis=0) + pr1 + b1)
            o = jnp.dot(hid.astype(BF16), w2_ref[kvi], preferred_element_type=F32) + b2_ref[kvi:kvi + 1, :]
            out_ref[0, g] = jnp.concatenate([o, jnp.zeros_like(o)], axis=1).astype(BF16)


def _compress_weights(cw):
    w1, b1, pos_emb, w2, b2 = cw
    assert CMP_BLOCK == 2 * CMP_STRIDE
    w1cat = jnp.concatenate([w1[:, :CMP_STRIDE], w1[:, CMP_STRIDE:]], axis=-1).astype(BF16)
    w1f = w1.reshape(2, CMP_BLOCK * HEAD_DIM, CMP_HID).astype(BF16)
    posf = jnp.broadcast_to(pos_emb.reshape(2, 1, CMP_BLOCK * HEAD_DIM), (2, SUBLANES, CMP_BLOCK * HEAD_DIM)).astype(BF16)
    return w1cat, w1f, posf, b1, w2.astype(BF16), b2


def _compress_prompt(cmp_kv, cw):
    B, T, _ = cmp_kv.shape
    n_half = T // CMP_STRIDE
    x = cmp_kv.reshape(B, n_half, CMP_STRIDE * KV_W)
    ws = _compress_weights(cw)
    o_spec = pl.BlockSpec((1, KV_HEADS, n_half, LANES), lambda bi: (bi, 0, 0, 0))
    o_shape = jax.ShapeDtypeStruct((B, KV_HEADS, n_half, LANES), BF16)
    return pl.pallas_call(
        _compress_kernel,
        grid=(B,),
        in_specs=[pl.BlockSpec((1, n_half, CMP_STRIDE * KV_W), lambda bi: (bi, 0, 0))]
        + [_const_spec(w.shape) for w in ws],
        out_specs=[o_spec, o_spec],
        out_shape=[o_shape, o_shape],
        compiler_params=_cparams(("parallel",)),
        name="compress_hm",
    )(x, *ws)


def _block_sum_matrix(n_out, n_in, n_blocks, n_cmp):
    a = np.zeros((n_out, n_in), np.float32)
    for j in range(n_blocks):
        for m in range(4 * j, 4 * j + 5):
            if 1 <= m <= n_cmp:
                a[j, m] = 1.0
    return a


def _attn_prompt_kernel(q_ref, ck_ref, cv_ref, kp_ref, vs_ref, wk_ref, wv_ref, gt_ref, at_ref, o_ref,
                        s_buf, m_run, acc_sc, oc_sc, pen_sc, *, n_cmp_rows, n_blk_rows):
    qt = pl.program_id(2)
    s0 = qt * Q_TILE
    R = Q_PER_KV * Q_TILE
    GS = ATTN_GROUPS
    Qs = [q_ref[0, gi].reshape(R, LANES) for gi in range(GS)]
    tq = s0 + (lax.broadcasted_iota(jnp.int32, (R, 1), 0) & (Q_TILE - 1))

    mrow = lax.broadcasted_iota(jnp.int32, (1, n_cmp_rows), 1)
    cend = jnp.where(mrow >= 1, mrow * CMP_STRIDE + (CMP_BLOCK - CMP_STRIDE - 1), jnp.int32(2 ** 30))
    j = lax.broadcasted_iota(jnp.int32, (n_blk_rows, Q_TILE), 0)
    tl = s0 + lax.broadcasted_iota(jnp.int32, (n_blk_rows, Q_TILE), 1)
    cur = tl // SEL_BLOCK
    valid = j * SEL_BLOCK <= tl
    forced = valid & ((j == 0) | (j == cur) | (j == cur - 1))
    scores = []
    for gi in range(GS):
        sc = lax.dot_general(Qs[gi], ck_ref[0, gi], NT_DIMS, preferred_element_type=F32)
        sc = jnp.where(cend <= tq, sc, NEG)
        e = jnp.exp2(sc - jnp.max(sc, axis=1, keepdims=True))
        inv = jnp.where(tq >= CMP_BLOCK - 1, 1.0 / jnp.sum(e, axis=1, keepdims=True), 0.0)
        p = e * inv
        oc_sc[gi] = jnp.dot(p.astype(BF16), cv_ref[0, gi], preferred_element_type=F32)
        imp = p[0:Q_TILE]
        for r in range(1, Q_PER_KV):
            imp = imp + p[r * Q_TILE:(r + 1) * Q_TILE]
        blk = lax.dot_general(at_ref[...], imp, NT_DIMS, precision=lax.Precision.HIGHEST,
                              preferred_element_type=F32)
        scores.append(jnp.where(forced, BIG, jnp.where(valid, blk, -1.0)))

    wlen = WINDOW + Q_TILE
    n_sub = wlen // Q_TILE
    w0 = pl.multiple_of(jnp.maximum(s0 - WINDOW, 0), Q_TILE)
    tri = (lax.broadcasted_iota(jnp.int32, (Q_TILE, Q_TILE), 1)
           <= lax.broadcasted_iota(jnp.int32, (Q_TILE, Q_TILE), 0))
    steady = s0 >= WINDOW
    wbias = []
    for i in range(n_sub):
        lower_ok = jnp.where(steady, i > 0, i <= qt)
        upper_ok = jnp.where(steady, i < n_sub - 1, i < qt)
        b = jnp.where(tri, jnp.where(lower_ok, 0.0, NEG), jnp.where(upper_ok, 0.0, NEG))
        wbias.append(jnp.concatenate([b] * Q_PER_KV, axis=0))
    wbias = jnp.concatenate(wbias, axis=1)
    gv = gt_ref[0]
    gcol = lambda gi, jb: jnp.concatenate(
        [gv[:, gi * LANES + 3 * r + jb:gi * LANES + 3 * r + jb + 1] for r in range(Q_PER_KV)], axis=0)
    for gi in range(GS):
        s = lax.dot_general(Qs[gi], wk_ref[0, gi, pl.ds(w0, wlen), :], NT_DIMS, preferred_element_type=F32)
        s = s + wbias
        pw = jnp.exp2(s - jnp.max(s, axis=1, keepdims=True)).astype(BF16)
        acc_w = jnp.dot(pw, wv_ref[0, gi, pl.ds(w0, wlen), :], preferred_element_type=F32)
        oc_sc[gi] = gcol(gi, 0) * oc_sc[gi] + (gcol(gi, 2) / acc_w[:, HEAD_DIM:HEAD_DIM + 1]) * acc_w

    jv = lax.broadcasted_iota(jnp.int32, (SUBLANES, Q_TILE), 0)

    def rank_rows(n_rows):
        nv = n_rows // SUBLANES
        for gi in range(GS):
            if n_rows <= N_SEL:
                pen_sc[gi] = jnp.where(valid, 0.0, NEG)
                continue
            sv = [scores[gi][v * SUBLANES:(v + 1) * SUBLANES, :] for v in range(nv)]
            cnt = [jnp.zeros((SUBLANES, Q_TILE), F32) for _ in range(nv)]
            for i in range(n_rows):
                vi, si = divmod(i, SUBLANES)
                ri = sv[vi][si:si + 1, :]
                for v in range(nv):
                    if v < vi:
                        beats = ri > sv[v]
                    elif v > vi:
                        beats = ri >= sv[v]
                    else:
                        beats = (ri > sv[v]) | ((jv > si) & (ri >= sv[v]))
                    cnt[v] = cnt[v] + jnp.where(beats, 1.0, 0.0)
            pen = jnp.where(jnp.concatenate(cnt, axis=0) < N_SEL, 0.0, NEG)
            if n_rows < n_blk_rows:
                pen = jnp.concatenate([pen, jnp.full((n_blk_rows - n_rows, Q_TILE), NEG, F32)], axis=0)
            pen_sc[gi] = jnp.where(valid, pen, NEG)

    n_poss = (s0 + Q_TILE - 1) // SEL_BLOCK + 1
    bounds = list(range(N_SEL, n_blk_rows, N_SEL)) + [n_blk_rows]
    for lo, hi in zip([0] + bounds[:-1], bounds):
        pl.when((n_poss > lo) & (n_poss <= hi))(functools.partial(rank_rows, hi))

    Qa = []
    for gi in range(GS):
        pen_rows = [jnp.zeros((HEAD_DIM, Q_TILE), F32), pen_sc[gi]]
        if n_blk_rows < HEAD_DIM:
            pen_rows.append(jnp.zeros((HEAD_DIM - n_blk_rows, Q_TILE), F32))
        pen = jnp.concatenate(pen_rows, axis=0).T
        Qa.append(Qs[gi] + jnp.concatenate([pen.astype(BF16)] * Q_PER_KV, axis=0))

    n_full = s0 // SLC_KT
    m_run[...] = jnp.full(m_run.shape, NEG, F32)
    acc_sc[...] = jnp.zeros(acc_sc.shape, F32)

    def scores_tile(kt, causal):
        k0 = pl.multiple_of(kt * SLC_KT, SLC_KT)
        for gi in range(GS):
            s = lax.dot_general(Qa[gi], kp_ref[0, gi, pl.ds(k0, SLC_KT), :], NT_DIMS, preferred_element_type=F32)
            if causal:
                kp_ = k0 + lax.broadcasted_iota(jnp.int32, (1, SLC_KT), 1)
                s = jnp.where(kp_ <= tq, s, NEG)
            s_buf[gi, kt] = s
            mx = m_run[gi]
            for c in range(SLC_KT // LANES):
                mx = jnp.maximum(mx, s[:, c * LANES:(c + 1) * LANES])
            m_run[gi] = mx

    def p1(i, carry):
        scores_tile(2 * i, False)
        scores_tile(2 * i + 1, False)
        return carry

    lax.fori_loop(0, n_full // 2, p1, 0)
    pl.when(n_full % 2 == 1)(lambda: scores_tile(n_full - 1, False))
    scores_tile(n_full, True)
    ms = [jnp.max(m_run[gi], axis=1, keepdims=True) for gi in range(GS)]

    def pv_tile(kt):
        k0 = pl.multiple_of(kt * SLC_KT, SLC_KT)
        for gi in range(GS):
            pe = jnp.exp2(s_buf[gi, kt] - ms[gi]).astype(BF16)
            acc_sc[gi] += jnp.dot(pe, vs_ref[0, gi, pl.ds(k0, SLC_KT), :], preferred_element_type=F32)

    def p2(i, carry):
        pv_tile(2 * i)
        pv_tile(2 * i + 1)
        return carry

    lax.fori_loop(0, (n_full + 1) // 2, p2, 0)
    pl.when(n_full % 2 == 0)(lambda: pv_tile(n_full))

    first_head = lax.broadcasted_iota(jnp.int32, (Q_TILE, LANES), 1) < HEAD_DIM
    for gi in range(GS):
        acc_s = acc_sc[gi]
        o = oc_sc[gi] + (gcol(gi, 1) / acc_s[:, HEAD_DIM:HEAD_DIM + 1]) * acc_s
        for pr in range(Q_PER_KV // 2):
            a = o[(2 * pr) * Q_TILE:(2 * pr + 1) * Q_TILE]
            b = pltpu.roll(o[(2 * pr + 1) * Q_TILE:(2 * pr + 2) * Q_TILE], HEAD_DIM, axis=1)
            c0 = (gi * Q_PER_KV // 2 + pr) * LANES
            o_ref[0, :, c0:c0 + LANES] = jnp.where(first_head, a, b).astype(BF16)


def _attn_prompt(qh, ck, cv, skp, sv, wk, wv, gates):
    B, _, T, _ = qh.shape
    GS = ATTN_GROUPS
    n_cmp_rows = ck.shape[2]
    n_blocks = T // SEL_BLOCK
    n_blk_rows = max(SUBLANES, n_blocks)
    assert n_blocks <= HEAD_DIM and T >= WINDOW + Q_TILE and T % SLC_KT == 0 and KV_HEADS % GS == 0
    n_cmp = T // CMP_STRIDE - CMP_BLOCK // CMP_STRIDE + 1
    at = jnp.asarray(_block_sum_matrix(n_blk_rows, n_cmp_rows, n_blocks, n_cmp))
    q5 = qh.reshape(B, KV_HEADS, Q_PER_KV, T, LANES)
    R = Q_PER_KV * Q_TILE
    full = lambda n: pl.BlockSpec((1, GS, n, LANES), lambda bi, gi, qi: (bi, gi, 0, 0),
                                  pipeline_mode=pl.Buffered(1))
    return pl.pallas_call(
        functools.partial(_attn_prompt_kernel, n_cmp_rows=n_cmp_rows, n_blk_rows=n_blk_rows),
        grid=(B, KV_HEADS // GS, T // Q_TILE),
        in_specs=[
            pl.BlockSpec((1, GS, Q_PER_KV, Q_TILE, LANES), lambda bi, gi, qi: (bi, gi, 0, qi, 0)),
            full(n_cmp_rows), full(n_cmp_rows), full(T), full(T), full(T), full(T),
            pl.BlockSpec((1, Q_TILE, GS * LANES), lambda bi, gi, qi: (bi, qi, gi)),
            pl.BlockSpec((n_blk_rows, n_cmp_rows), lambda bi, gi, qi: (0, 0)),
        ],
        out_specs=pl.BlockSpec((1, Q_TILE, GS * Q_PER_KV * HEAD_DIM), lambda bi, gi, qi: (bi, qi, gi)),
        out_shape=jax.ShapeDtypeStruct((B, T, D_MODEL), BF16),
        scratch_shapes=[pltpu.VMEM((GS, T // SLC_KT, R, SLC_KT), F32), pltpu.VMEM((GS, R, LANES), F32),
                        pltpu.VMEM((GS, R, LANES), F32), pltpu.VMEM((GS, R, LANES), F32),
                        pltpu.VMEM((GS, n_blk_rows, Q_TILE), F32)],
        compiler_params=_cparams(("parallel", "parallel", "arbitrary")),
        name="attn_prompt",
    )(q5, ck, cv, skp, sv, wk, wv, gates, at)


def _oproj_ln_kernel(h_ref, o_ref, w_ref, g_ref, b_ref, out_ref):
    y = jnp.dot(o_ref[0], w_ref[...], preferred_element_type=F32)
    out_ref[0] = _layer_norm(ALPHA * h_ref[0] + y, g_ref[...], b_ref[...])


def _oproj_ln(h, o, w_o, g, b, tT):
    B, T, D = h.shape
    row = pl.BlockSpec((1, tT, D), lambda bi, ti: (bi, ti, 0))
    return pl.pallas_call(
        _oproj_ln_kernel,
        grid=(B, T // tT),
        in_specs=[row, row, _const_spec((D, D)), _const_spec((1, D)), _const_spec((1, D))],
        out_specs=row,
        out_shape=jax.ShapeDtypeStruct((B, T, D), F32),
        compiler_params=_cparams(("parallel", "parallel")),
        name="oproj_ln",
    )(h, o, w_o.astype(BF16), g.reshape(1, D), b.reshape(1, D))


PEN_TILE = PAGES_PER_STEP * (PAGE_SIZE // SEL_BLOCK)
S1_BATCH = 4


def _compress_pg_kernel(ptab_ref, *refs, n_pg):
    pg_refs = refs[:n_pg]
    w4_ref, w1f_ref, pos_ref, b1_ref, w2_ref, b2_ref, ck_ref, cv_ref = refs[n_pg:n_pg + 8]
    xs, carry = refs[n_pg + 8:-1], refs[-1]
    pt = pl.program_id(1)

    @pl.when(pt == 0)
    def _():
        carry[...] = jnp.zeros_like(carry)

    hp = PAGE_SIZE // CMP_STRIDE
    M = n_pg * hp
    first = lax.broadcasted_iota(jnp.int32, (M, CMP_HID), 0) == 0
    for kvi, out_ref in enumerate((ck_ref, cv_ref)):
        b1 = b1_ref[kvi:kvi + 1, :] + jnp.dot(pos_ref[kvi], w1f_ref[kvi], preferred_element_type=F32)[0:1, :]
        for pair in range(KV_HEADS // 2):
            c = kvi * (KV_HEADS // 2) + pair
            for i, r in enumerate(pg_refs):
                xs[c][i * PAGE_SIZE:(i + 1) * PAGE_SIZE, :] = r[0, c * LANES:(c + 1) * LANES, :].T
            acc = None
            for tq in range(CMP_STRIDE // CMP_TGROUP):
                lhs = jnp.concatenate(
                    [xs[c][pl.ds(tq * CMP_TGROUP + k, M, stride=CMP_STRIDE), :].astype(BF16)
                     for k in range(CMP_TGROUP)], axis=1)
                d = jnp.dot(lhs, w4_ref[kvi, tq], preferred_element_type=F32)
                acc = d if acc is None else acc + d
            for gl in range(2):
                g = pair * 2 + gl
                pr0 = acc[:, gl * 2 * CMP_HID:gl * 2 * CMP_HID + CMP_HID]
                pr1 = acc[:, gl * 2 * CMP_HID + CMP_HID:(gl + 1) * 2 * CMP_HID]
                ci = kvi * KV_HEADS + g
                prev = jnp.where(first, carry[ci, 0:1, :], pltpu.roll(pr0, 1, axis=0))
                carry[ci, 0:1, :] = pr0[M - 1:M, :]
                hid = _gelu_tanh(prev + pr1 + b1)
                o = jnp.dot(hid.astype(BF16), w2_ref[kvi], preferred_element_type=F32) + b2_ref[kvi:kvi + 1, :]
                out_ref[0, :, g * HEAD_DIM:(g + 1) * HEAD_DIM] = o.astype(BF16)


def _page_specs(n_pg):
    return [pl.BlockSpec((1, KV_W, PAGE_SIZE),
                         functools.partial(lambda bi, pi, ptab, i: (ptab[bi, pi * n_pg + i], 0, 0), i=i))
            for i in range(n_pg)]


def _compress_pages(pool_t, page_table, cw):
    B, n_pages = page_table.shape
    n_pg = CMP_PAGES_PER_STEP
    assert n_pages % n_pg == 0
    hp = PAGE_SIZE // CMP_STRIDE
    w1cat, w1f, posf, b1, w2, b2 = _compress_weights(cw)
    ntq = CMP_STRIDE // CMP_TGROUP
    w1q = w1cat.reshape(2, ntq, CMP_TGROUP, 1, HEAD_DIM, 1, 2 * CMP_HID)
    eye = jnp.eye(2, dtype=w1cat.dtype).reshape(1, 1, 1, 2, 1, 2, 1)
    w4 = (w1q * eye).reshape(2, ntq, CMP_TGROUP * 2 * HEAD_DIM, 2 * 2 * CMP_HID)
    o_spec = pl.BlockSpec((1, n_pg * hp, KV_W // 2), lambda bi, pi, ptab: (bi, pi, 0))
    o_shape = jax.ShapeDtypeStruct((B, n_pages * hp, KV_W // 2), BF16)
    return pl.pallas_call(
        functools.partial(_compress_pg_kernel, n_pg=n_pg),
        grid_spec=pltpu.PrefetchScalarGridSpec(
            num_scalar_prefetch=1, grid=(B, n_pages // n_pg),
            in_specs=_page_specs(n_pg) + [_const_spec(w.shape) for w in (w4, w1f, posf, b1, w2, b2)],
            out_specs=[o_spec, o_spec],
            scratch_shapes=[pltpu.VMEM((n_pg * PAGE_SIZE, LANES), F32)] * (KV_W // LANES)
            + [pltpu.VMEM((2 * KV_HEADS, SUBLANES, CMP_HID), F32)],
        ),
        out_shape=[o_shape, o_shape],
        compiler_params=_cparams(("parallel", "arbitrary")),
        name="compress_pg",
    )(page_table, *([pool_t] * n_pg), w4, w1f, posf, b1, w2, b2)


def _row_q(shape):
    return lax.broadcasted_iota(jnp.int32, shape, 0) & 7


def _diag_heads(o_all):
    g_row = (lax.broadcasted_iota(jnp.int32, (LANES, HEAD_DIM), 0) // 8) & (KV_HEADS - 1)
    out = jnp.zeros((LANES, HEAD_DIM), F32)
    for g in range(KV_HEADS):
        out = out + jnp.where(g_row == g, o_all[:, g * HEAD_DIM:(g + 1) * HEAD_DIM], 0.0)
    return out


def _attn_s1_kernel(ck_ref, cv_ref, qbr_ref, at_ref, ocmp_ref, pen_ref, score_sc, *, past, n_blocks, n_tiles):
    n_rows = ck_ref.shape[1]
    nb_rows = at_ref.shape[0]
    mrow = lax.broadcasted_iota(jnp.int32, (1, n_rows), 1)
    cend = jnp.where(mrow >= 1, mrow * CMP_STRIDE + (CMP_BLOCK - CMP_STRIDE - 1), jnp.int32(2 ** 30))
    tq = past + _row_q((LANES, 1))
    j = lax.broadcasted_iota(jnp.int32, (nb_rows, LANES), 0)
    lane = lax.broadcasted_iota(jnp.int32, (nb_rows, LANES), 1)
    grp = lane // (LANES // Q_PER_KV)
    tl = past + (lane & 7)
    cur = tl // SEL_BLOCK
    valid = (j * SEL_BLOCK <= tl) & (j < n_blocks)
    forced = valid & ((j == 0) | (j == cur) | (j == cur - 1))
    rows = LANES // Q_PER_KV
    score = None
    for k in range(S1_BATCH):
        s = lax.dot_general(qbr_ref[k], ck_ref[k], NT_DIMS, preferred_element_type=F32)
        s = jnp.where(cend <= tq, s, NEG)
        e = jnp.exp(s - jnp.max(s, axis=1, keepdims=True))
        inv = jnp.where(tq >= CMP_BLOCK - 1, 1.0 / jnp.sum(e, axis=1, keepdims=True), 0.0)
        p = e * inv
        ocmp_ref[k] = _diag_heads(jnp.dot(p.astype(BF16), cv_ref[k], preferred_element_type=F32))
        imp = p[0:rows]
        for r in range(1, Q_PER_KV):
            imp = imp + p[r * rows:(r + 1) * rows]
        imp = jnp.concatenate([imp] * Q_PER_KV, axis=0)
        blk = lax.dot_general(at_ref[...], imp, NT_DIMS, precision=lax.Precision.HIGHEST,
                              preferred_element_type=F32)
        sk = jnp.where(forced, BIG, jnp.where(valid, blk, -1.0))
        score = sk if score is None else jnp.where(grp == k, sk, score)
    score_sc[...] = score

    def body(i, cnt):
        ri = score_sc[pl.ds(i, 1), :]
        beats = (ri > score) | ((ri == score) & (j > i))
        return cnt + jnp.where(beats, 1.0, 0.0)

    cnt = lax.fori_loop(0, n_blocks, body, jnp.zeros((nb_rows, LANES), F32))
    pen4 = jnp.where((cnt < N_SEL) & valid, 0.0, NEG)
    rolled = [pen4] + [pltpu.roll(pen4, sft * rows, axis=1) for sft in range(1, S1_BATCH)]
    n_chunks = -(-n_tiles * PEN_TILE // LANES)
    zeros = jnp.zeros((LANES, LANES - PEN_TILE), F32)
    for k in range(S1_BATCH):
        pen_t = rolled[(0 - k) % S1_BATCH]
        for m in range(1, S1_BATCH):
            pen_t = jnp.where(grp == m, rolled[(m - k) % S1_BATCH], pen_t)
        pen_t = jnp.concatenate([pen_t, jnp.full((n_chunks * LANES - nb_rows, LANES), NEG, F32)], axis=0)
        pen_r = jnp.concatenate([pen_t[c * LANES:(c + 1) * LANES, :].T for c in range(n_chunks)], axis=1)
        for t in range(n_tiles):
            pen_ref[k, t] = jnp.concatenate([pen_r[:, t * PEN_TILE:(t + 1) * PEN_TILE], zeros], axis=1)


def _attn_s2_kernel(ptab_ref, *refs, n_pg):
    pg_refs = refs[:n_pg]
    qbr_ref, pen_ref, e_ref, m_ref, l_ref, acc_ref = refs[n_pg:]
    kw = KV_W // 2
    kt = jnp.concatenate([r[0, 0:kw, :].astype(BF16) for r in pg_refs], axis=1)
    s = jnp.dot(qbr_ref[0], kt, preferred_element_type=F32)
    s = s + jnp.dot(pen_ref[0, 0].astype(BF16), e_ref[...], preferred_element_type=F32)
    m = jnp.max(s, axis=1, keepdims=True)
    pe = jnp.exp(s - m)
    l = jnp.sum(pe, axis=1, keepdims=True)
    vt = jnp.concatenate([r[0, kw:KV_W, :].astype(BF16) for r in pg_refs], axis=1)
    acc_ref[0, 0] = lax.dot_general(pe.astype(BF16), vt, NT_DIMS, preferred_element_type=F32)
    m_ref[0, 0] = jnp.broadcast_to(m, (LANES, LANES))
    l_ref[0, 0] = jnp.broadcast_to(l, (LANES, LANES))


def _attn_s3_kernel(m_ref, l_ref, acc_ref, pen_ref, ocmp_ref, qbr_ref, snew_ref, wst_ref, wnew_ref, gt_ref,
                    o_ref, *, n_pt, n_new):
    qbr = qbr_ref[0]
    kw = KV_W // 2
    nrow = snew_ref.shape[1]
    qr = _row_q((LANES, 1))
    icol = lax.broadcasted_iota(jnp.int32, (1, nrow), 1)
    new_ok = (icol <= qr) & (icol < n_new)

    xs = snew_ref[0]
    st = lax.dot_general(qbr, xs[:, 0:kw].astype(BF16), NT_DIMS, preferred_element_type=F32)
    st = jnp.where(new_ok, st + pen_ref[0, 0][:, 0:1], NEG)
    m_tot = jnp.max(st, axis=1, keepdims=True)
    for s in range(n_pt):
        m_tot = jnp.maximum(m_tot, m_ref[0, s][:, 0:1])
    pt_ = jnp.exp(st - m_tot)
    l_tot = jnp.sum(pt_, axis=1, keepdims=True)
    acc = jnp.dot(pt_.astype(BF16), xs[:, kw:KV_W].astype(BF16), preferred_element_type=F32)
    for s in range(n_pt):
        a = jnp.exp(m_ref[0, s][:, 0:1] - m_tot)
        l_tot = l_tot + a * l_ref[0, s][:, 0:1]
        acc = acc + a * acc_ref[0, s]
    o_slc = _diag_heads(acc / l_tot)

    wn = wnew_ref[0]
    nw = wst_ref.shape[2]
    s1 = jnp.dot(qbr, wst_ref[0, 0:kw, :].astype(BF16), preferred_element_type=F32)
    s2 = lax.dot_general(qbr, wn[:, 0:kw].astype(BF16), NT_DIMS, preferred_element_type=F32)
    dist1 = nw + qr - lax.broadcasted_iota(jnp.int32, (1, nw), 1)
    mask1 = (dist1 >= 0) & (dist1 < WINDOW)
    mask2 = new_ok & (qr - icol < WINDOW)
    s1 = jnp.where(mask1, s1, NEG)
    s2 = jnp.where(mask2, s2, NEG)
    mw = jnp.maximum(jnp.max(s1, axis=1, keepdims=True), jnp.max(s2, axis=1, keepdims=True))
    p1 = jnp.where(mask1, jnp.exp(s1 - mw), 0.0)
    p2 = jnp.where(mask2, jnp.exp(s2 - mw), 0.0)
    inv = 1.0 / (jnp.sum(p1, axis=1, keepdims=True) + jnp.sum(p2, axis=1, keepdims=True))
    ow = (lax.dot_general((p1 * inv).astype(BF16), wst_ref[0, kw:KV_W, :].astype(BF16), NT_DIMS,
                          preferred_element_type=F32)
          + jnp.dot((p2 * inv).astype(BF16), wn[:, kw:KV_W].astype(BF16), preferred_element_type=F32))
    o_win = _diag_heads(ow)

    gv = gt_ref[0]
    o_ref[0] = gv[:, 0:1] * ocmp_ref[0] + gv[:, 1:2] * o_slc + gv[:, 2:3] * o_win


def _attn_sample(q_flat, gates, ck, cv, slc_pool_t, page_table, slc_new, win_state_t, win_new, past):
    B, Tn, _ = q_flat.shape
    n_pg = PAGES_PER_STEP
    assert Tn == 8 and past % (PAGE_SIZE * n_pg) == 0
    n_pages = past // PAGE_SIZE
    n_pt = n_pages // n_pg
    n_past_blk = past // SEL_BLOCK
    n_blocks = n_past_blk + 1
    nb_rows = -(-n_blocks // SUBLANES) * SUBLANES
    n_cmp = past // CMP_STRIDE - 1
    Mc = ck.shape[1]
    at = jnp.asarray(_block_sum_matrix(nb_rows, Mc, n_blocks, n_cmp))
    kw = KV_W // 2

    q5 = q_flat.reshape(B, Tn, KV_HEADS, Q_PER_KV, HEAD_DIM)
    base = jnp.transpose(q5, (0, 3, 2, 1, 4))
    eye = jnp.eye(KV_HEADS, dtype=q_flat.dtype)
    qbr = (base[:, :, :, :, None, :] * eye[None, None, :, None, :, None]).reshape(B, LANES, kw)
    g5 = gates[:, :, :N_HEADS * 3].reshape(B, Tn, KV_HEADS, Q_PER_KV, 3)
    gt = jnp.pad(jnp.transpose(g5, (0, 3, 2, 1, 4)).reshape(B, LANES, 3), ((0, 0), (0, 0), (0, LANES - 3)))

    per_b = lambda *shape: pl.BlockSpec((1,) + shape, lambda bi, *_: (bi,) + (0,) * len(shape))
    assert B % S1_BATCH == 0 and S1_BATCH == Q_PER_KV
    per_sb = lambda *shape: pl.BlockSpec((S1_BATCH,) + shape, lambda bi: (bi,) + (0,) * len(shape))
    ocmp, pen = pl.pallas_call(
        functools.partial(_attn_s1_kernel, past=past, n_blocks=n_blocks, n_tiles=n_pt + 1),
        grid=(B // S1_BATCH,),
        in_specs=[per_sb(Mc, kw), per_sb(Mc, kw), per_sb(LANES, kw),
                  pl.BlockSpec((nb_rows, Mc), lambda bi: (0, 0))],
        out_specs=[per_sb(LANES, HEAD_DIM), per_sb(n_pt + 1, LANES, LANES)],
        out_shape=[jax.ShapeDtypeStruct((B, LANES, HEAD_DIM), F32),
                   jax.ShapeDtypeStruct((B, n_pt + 1, LANES, LANES), F32)],
        scratch_shapes=[pltpu.VMEM((nb_rows, LANES), F32)],
        compiler_params=_cparams(("parallel",)),
        name="attn_s1",
    )(ck, cv, qbr, at)

    expand = np.zeros((LANES, n_pg * PAGE_SIZE), np.float32)
    for jb in range(PEN_TILE):
        expand[jb, jb * SEL_BLOCK:(jb + 1) * SEL_BLOCK] = 1.0
    part = lambda *shape: pl.BlockSpec((1, 1) + shape, lambda bi, pi, ptab: (bi, pi) + (0,) * len(shape))
    m_p, l_p, acc_p = pl.pallas_call(
        functools.partial(_attn_s2_kernel, n_pg=n_pg),
        grid_spec=pltpu.PrefetchScalarGridSpec(
            num_scalar_prefetch=1, grid=(B, n_pt),
            in_specs=_page_specs(n_pg) + [pl.BlockSpec((1, LANES, kw), lambda bi, pi, ptab: (bi, 0, 0)),
                                          part(LANES, LANES),
                                          pl.BlockSpec(expand.shape, lambda bi, pi, ptab: (0, 0))],
            out_specs=[part(LANES, LANES), part(LANES, LANES), part(LANES, kw)],
        ),
        out_shape=[jax.ShapeDtypeStruct((B, n_pt, LANES, LANES), F32)] * 2
        + [jax.ShapeDtypeStruct((B, n_pt, LANES, kw), F32)],
        compiler_params=_cparams(("parallel", "parallel")),
        name="attn_s2",
    )(page_table, *([slc_pool_t] * n_pg), qbr, pen, jnp.asarray(expand, BF16))

    pad_rows = 16 - Tn
    snew = jnp.pad(slc_new, ((0, 0), (0, pad_rows), (0, 0)))
    wnew = jnp.pad(win_new, ((0, 0), (0, pad_rows), (0, 0)))
    nw = win_state_t.shape[2]
    o = pl.pallas_call(
        functools.partial(_attn_s3_kernel, n_pt=n_pt, n_new=Tn),
        grid=(B,),
        in_specs=[per_b(n_pt, LANES, LANES), per_b(n_pt, LANES, LANES), per_b(n_pt, LANES, kw),
                  pl.BlockSpec((1, 1, LANES, LANES), lambda bi: (bi, n_pt, 0, 0)),
                  per_b(LANES, HEAD_DIM), per_b(LANES, kw),
                  per_b(16, KV_W), per_b(KV_W, nw), per_b(16, KV_W), per_b(LANES, LANES)],
        out_specs=per_b(LANES, HEAD_DIM),
        out_shape=jax.ShapeDtypeStruct((B, LANES, HEAD_DIM), F32),
        compiler_params=_cparams(("parallel",)),
        name="attn_s3",
    )(m_p, l_p, acc_p, pen, ocmp, qbr, snew, win_state_t, wnew, gt)
    o = jnp.transpose(o.reshape(B, Q_PER_KV, KV_HEADS, Tn, HEAD_DIM), (0, 3, 2, 1, 4))
    return o.reshape(B, Tn, D_MODEL).astype(BF16)


def _nsa_prompt_layer(h, w_in, w_o, cw, g, b, tT):
    B, T, _ = h.shape
    cmp_kv, slc_kv, win_kv, qh, skp, sv, wk, wv, gates = _nsa_proj(h, jnp.arange(T), w_in, tT, True)
    ck, cv = _compress_prompt(cmp_kv, cw)
    o = _attn_prompt(qh, ck, cv, skp, sv, wk, wv, gates)
    h2 = _oproj_ln(h, o, w_o, g, b, tT)
    win_len = min(WINDOW, T)
    return h2, cmp_kv, slc_kv, win_kv[:, T - win_len:]


def _pages_device_layout(pool):
    return jnp.transpose(pool, (0, 2, 3, 4, 1)).reshape(pool.shape[0], KV_W, pool.shape[1])


def _nsa_sample_layer(h, cmp_pool, slc_pool, win_buf, page_table, w_in, w_o, cw, g, b):
    B, Tn, _ = h.shape
    n_pages = page_table.shape[1]
    past = n_pages * PAGE_SIZE
    cmp_kv, slc_kv, win_kv, q_flat, gates = _nsa_proj(h, past + jnp.arange(Tn), w_in, Tn, False)
    assert Tn < CMP_STRIDE
    ck, cv = _compress_pages(_pages_device_layout(cmp_pool), page_table, cw)
    o = _attn_sample(q_flat, gates, ck, cv, _pages_device_layout(slc_pool), page_table, slc_kv,
                     _pages_device_layout(win_buf), win_kv, past)
    h2 = _oproj_ln(h, o, w_o, g, b, Tn)
    buf_len = win_buf.shape[1]
    new_win = jnp.concatenate([win_buf, win_kv.reshape(B, Tn, 2, KV_HEADS, HEAD_DIM)], axis=1)[:, -buf_len:]
    return h2, cmp_kv, slc_kv, new_win.reshape(B, buf_len, KV_W)


def _row_tile(T):
    return 512 if T % 512 == 0 else T


def kernel(x_prompt, x_sample, state_pool, cache_cmp_kv, cache_slc_kv, state_win_kv, state_ffn, page_table, ln_g, ln_b, pool_w, pool_scale, nsa_w_in, nsa_w_o, cmp_w1, cmp_b1, cmp_pos, cmp_w2, cmp_b2, ffn_w_up, ffn_conv_w, ffn_conv_b, ffn_w_down):
    Bp, T, D = x_prompt.shape
    Bs, Tn, _ = x_sample.shape
    past = page_table.shape[1] * PAGE_SIZE
    kv_shape = (2, KV_HEADS, HEAD_DIM)
    tp, ts = _row_tile(T), Tn

    hp = _pool_ln(x_prompt, jnp.zeros((Bp, POOL_BUF, D), F32), 0, pool_w[0], pool_scale[0], ln_g[0, 0], ln_b[0, 0], tp)
    hs = _pool_ln(x_sample, state_pool[0], past, pool_w[0], pool_scale[0], ln_g[0, 0], ln_b[0, 0], ts)
    pool_p = jnp.concatenate([jnp.zeros((Bp, POOL_BUF, D), F32), x_prompt], axis=1)[:, -POOL_BUF:][None]
    pool_s = jnp.concatenate([state_pool[0], x_sample], axis=1)[:, -POOL_BUF:][None]
    ffn = lambda h, pre, i, t: _ffn_ln(h, pre, ffn_w_up[i], ffn_conv_w[i], ffn_conv_b[i], ffn_w_down[i],
                                       ln_g[i, 1], ln_b[i, 1], t)
    zero_pre = jnp.zeros((Bp, CONV_W - 1, 2 * D_FF), F32)
    hp, ffn_p0 = ffn(hp, zero_pre, 0, tp)
    ffn_s = lambda h, pre, i: _ffn_ln_short(h, pre, ffn_w_up[i], ffn_conv_w[i], ffn_conv_b[i], ffn_w_down[i],
                                            ln_g[i, 1], ln_b[i, 1])
    hs, ffn_s0 = ffn_s(hs, state_ffn[0], 0)

    cw = (cmp_w1[0], cmp_b1[0], cmp_pos[0], cmp_w2[0], cmp_b2[0])
    hp, c_p, s_p, w_p = _nsa_prompt_layer(hp, nsa_w_in[0], nsa_w_o[0], cw, ln_g[1, 0], ln_b[1, 0], tp)
    hs, c_s, s_s, w_s = _nsa_sample_layer(hs, cache_cmp_kv[0], cache_slc_kv[0], state_win_kv[0], page_table,
                                          nsa_w_in[0], nsa_w_o[0], cw, ln_g[1, 0], ln_b[1, 0])
    hp, ffn_p1 = ffn(hp, zero_pre, 1, tp)
    hs, ffn_s1 = ffn_s(hs, state_ffn[1], 1)

    kv5 = lambda a: a.reshape(a.shape[0], a.shape[1], *kv_shape)[None]
    return (hp, hs, pool_p, pool_s, kv5(c_p), kv5(c_s), kv5(s_p), kv5(s_s), kv5(w_p), kv5(w_s),
            jnp.stack([ffn_p0, ffn_p1]), jnp.stack([ffn_s0, ffn_s1]))
```

```python
import functools

import numpy as np
import jax
import jax.numpy as jnp
from jax import lax
from jax.experimental import pallas as pl
from jax.experimental.pallas import tpu as pltpu

D_MODEL = 1024
DEPTH = 2
ALPHA = (2.0 * DEPTH) ** 0.25
LN_EPS = 1e-5
POOL_WINDOWS = (2, 4, 8, 16)
POOL_CH = D_MODEL // len(POOL_WINDOWS)
POOL_BUF = max(POOL_WINDOWS) - 1
N_HEADS = 16
KV_HEADS = 4
HEAD_DIM = 64
Q_PER_KV = N_HEADS // KV_HEADS
ROT_DIM = HEAD_DIM // 4
ROPE_THETA = 500000.0
CMP_BLOCK = 32
CMP_STRIDE = 16
CMP_HID = 128
SEL_BLOCK = 64
N_SEL = 16
WINDOW = 512
KV_W = 2 * KV_HEADS * HEAD_DIM
D_FF = 2816
CONV_W = 3
PAGE_SIZE = 128
NEG = -1e30
BIG = 1e9
LOG2E = 1.4426950408889634

LANES = 128
SUBLANES = 8
Q_TILE = 128
SLC_KT = 512
ATTN_GROUPS = 4
PAGES_PER_STEP = 32
CMP_PAGES_PER_STEP = 32
CMP_TGROUP = 4
VMEM_LIMIT = 60 * 1024 * 1024

F32 = jnp.float32
BF16 = jnp.bfloat16
NT_DIMS = (((1,), (1,)), ((), ()))


def _cparams(sem):
    return pltpu.CompilerParams(dimension_semantics=sem, vmem_limit_bytes=VMEM_LIMIT)


def _layer_norm(h, g, b):
    mu = jnp.mean(h, axis=-1, keepdims=True)
    hc = h - mu
    var = jnp.mean(hc * hc, axis=-1, keepdims=True)
    return hc * lax.rsqrt(var + LN_EPS) * g + b


def _const_spec(shape):
    nd = len(shape)
    return pl.BlockSpec(shape, lambda *_: (0,) * nd, pipeline_mode=pl.Buffered(1))


def _pool_ln_kernel(x_ref, halo_ref, pre_ref, w_ref, sc_ref, g_ref, b_ref, o_ref, buf, *, tT, pos0):
    t = pl.program_id(1)
    x = x_ref[0]
    buf[0:16, :] = jnp.where(t == 0, pre_ref[0], halo_ref[0])
    buf[16:16 + tT, :] = x
    pos = pos0 + t * tT + lax.broadcasted_iota(jnp.int32, (tT, 1), 0)
    ys = []
    for gi, win in enumerate(POOL_WINDOWS):
        c0 = gi * POOL_CH
        xg = x[:, c0:c0 + POOL_CH]
        s = xg
        for i in range(1, win):
            s = s + buf[pl.ds(16 - i, tT), c0:c0 + POOL_CH]
        cnt = jnp.minimum(pos + 1, win).astype(F32)
        d = s / cnt - xg
        ys.append(jnp.dot(d.astype(BF16), w_ref[gi], preferred_element_type=F32))
    y = jnp.concatenate(ys, axis=1) * sc_ref[...]
    o_ref[0] = _layer_norm(ALPHA * x + y, g_ref[...], b_ref[...])


def _pool_ln(x, prefix, pos0, w, scale, g, b, tT):
    B, T, D = x.shape
    nT = T // tT
    pre16 = jnp.concatenate([jnp.zeros((B, 1, D), x.dtype), prefix], axis=1)
    if T >= 16:
        halo_src = x
        per = tT // 16
        halo_spec = pl.BlockSpec((1, 16, D), lambda bi, ti: (bi, jnp.maximum(ti * per - 1, 0), 0))
    else:
        halo_src = pre16
        halo_spec = pl.BlockSpec((1, 16, D), lambda bi, ti: (bi, 0, 0))
    return pl.pallas_call(
        functools.partial(_pool_ln_kernel, tT=tT, pos0=pos0),
        grid=(B, nT),
        in_specs=[
            pl.BlockSpec((1, tT, D), lambda bi, ti: (bi, ti, 0)),
            halo_spec,
            pl.BlockSpec((1, 16, D), lambda bi, ti: (bi, 0, 0)),
            _const_spec((len(POOL_WINDOWS), POOL_CH, POOL_CH)),
            _const_spec((1, D)), _const_spec((1, D)), _const_spec((1, D)),
        ],
        out_specs=pl.BlockSpec((1, tT, D), lambda bi, ti: (bi, ti, 0)),
        out_shape=jax.ShapeDtypeStruct((B, T, D), F32),
        scratch_shapes=[pltpu.VMEM((16 + tT, D), F32)],
        compiler_params=_cparams(("parallel", "parallel")),
        name="pool_ln",
    )(x, halo_src, pre16, w.astype(BF16), scale.reshape(1, D), g.reshape(1, D), b.reshape(1, D))


FFN_CHUNK = 256


def _ffn_ln_kernel(h_ref, pre_ref, wup_ref, cw_ref, cb_ref, wdn_ref, g_ref, b_ref, o_ref, st_ref,
                   ubuf, act, *, tT):
    t = pl.program_id(1)

    @pl.when(t == 0)
    def _():
        st_ref[0] = pre_ref[0]

    h = h_ref[0]
    hb = h.astype(BF16)
    fc = FFN_CHUNK
    for j in range(D_FF // fc):
        cs = []
        for half in range(2):
            c0 = half * D_FF + j * fc
            up = jnp.dot(hb, wup_ref[:, c0:c0 + fc], preferred_element_type=F32)
            ubuf[half, 0:8, :] = st_ref[0, :, c0:c0 + fc]
            ubuf[half, 8:8 + tT, :] = up
            st_ref[0, :, c0:c0 + fc] = ubuf[half, tT:tT + 8, :]
            c = (cb_ref[:, c0:c0 + fc]
                 + cw_ref[0:1, c0:c0 + fc] * ubuf[half, pl.ds(6, tT), :]
                 + cw_ref[1:2, c0:c0 + fc] * ubuf[half, pl.ds(7, tT), :]
                 + cw_ref[2:3, c0:c0 + fc] * up)
            cs.append(c)
        a = cs[0] * jax.nn.sigmoid(cs[0]) * cs[1]
        act[:, j * fc:(j + 1) * fc] = a.astype(BF16)
    y = jnp.dot(act[...], wdn_ref[...], preferred_element_type=F32)
    o_ref[0] = _layer_norm(ALPHA * h + y, g_ref[...], b_ref[...])


def _ffn_ln(h, prefix, w_up, conv_w, conv_b, w_down, g, b, tT):
    B, T, D = h.shape
    nT = T // tT
    F2 = 2 * D_FF
    pre8 = jnp.concatenate([jnp.zeros((B, 8 - (CONV_W - 1), F2), F32), prefix], axis=1)
    out, st = pl.pallas_call(
        functools.partial(_ffn_ln_kernel, tT=tT),
        grid=(B, nT),
        in_specs=[
            pl.BlockSpec((1, tT, D), lambda bi, ti: (bi, ti, 0)),
            pl.BlockSpec((1, 8, F2), lambda bi, ti: (bi, 0, 0)),
            _const_spec((D, F2)),
            _const_spec((CONV_W, F2)),
            _const_spec((1, F2)),
            _const_spec((D_FF, D)),
            _const_spec((1, D)), _const_spec((1, D)),
        ],
        out_specs=[pl.BlockSpec((1, tT, D), lambda bi, ti: (bi, ti, 0)),
                   pl.BlockSpec((1, 8, F2), lambda bi, ti: (bi, 0, 0))],
        out_shape=[jax.ShapeDtypeStruct((B, T, D), F32), jax.ShapeDtypeStruct((B, 8, F2), F32)],
        scratch_shapes=[pltpu.VMEM((2, tT + 8, FFN_CHUNK), F32), pltpu.VMEM((tT, D_FF), BF16)],
        compiler_params=_cparams(("parallel", "arbitrary")),
        name="ffn_ln",
    )(h, pre8, w_up.astype(BF16), conv_w, conv_b.reshape(1, F2), w_down.astype(BF16),
      g.reshape(1, D), b.reshape(1, D))
    return out, st[:, 8 - (CONV_W - 1):, :]


def _ffn_ln_short_kernel(h_ref, pre_ref, wup_ref, cw_ref, cb_ref, wdn_ref, g_ref, b_ref, o_ref, st_ref, act,
                         *, seq):
    h = h_ref[...]
    hb = h.astype(BF16)
    rows = h.shape[0]
    fc = FFN_CHUNK
    t = lax.broadcasted_iota(jnp.int32, (rows, fc), 0) & (seq - 1)
    for j in range(D_FF // fc):
        cs = []
        for half in range(2):
            c0 = half * D_FF + j * fc
            up = jnp.dot(hb, wup_ref[:, c0:c0 + fc], preferred_element_type=F32)
            st_ref[:, c0:c0 + fc] = up
            pre = pre_ref[:, c0:c0 + fc]
            s1 = jnp.where(t >= 1, pltpu.roll(up, 1, axis=0), pltpu.roll(pre, 1, axis=0))
            s2 = jnp.where(t >= 2, pltpu.roll(up, 2, axis=0), pltpu.roll(pre, 2, axis=0))
            cs.append(cb_ref[:, c0:c0 + fc] + cw_ref[0:1, c0:c0 + fc] * s2 + cw_ref[1:2, c0:c0 + fc] * s1
                      + cw_ref[2:3, c0:c0 + fc] * up)
        a = cs[0] * jax.nn.sigmoid(cs[0]) * cs[1]
        act[:, j * fc:(j + 1) * fc] = a.astype(BF16)
    y = jnp.dot(act[...], wdn_ref[...], preferred_element_type=F32)
    o_ref[...] = _layer_norm(ALPHA * h + y, g_ref[...], b_ref[...])


def _ffn_ln_short(h, prefix, w_up, conv_w, conv_b, w_down, g, b):
    B, T, D = h.shape
    assert T == SUBLANES and CONV_W - 1 <= T
    F2 = 2 * D_FF
    rows = B * T
    pre8 = jnp.concatenate([jnp.zeros((B, T - (CONV_W - 1), F2), F32), prefix], axis=1)
    pre_rows = jnp.roll(pre8, -1, axis=0).reshape(rows, F2)
    full = lambda *shape: pl.BlockSpec(shape, lambda i: (0,) * len(shape))
    out, st = pl.pallas_call(
        functools.partial(_ffn_ln_short_kernel, seq=T),
        grid=(1,),
        in_specs=[_const_spec(sh) for sh in ((rows, D), (rows, F2), (D, F2), (CONV_W, F2), (1, F2), (D_FF, D),
                                             (1, D), (1, D))],
        out_specs=[full(rows, D), full(rows, F2)],
        out_shape=[jax.ShapeDtypeStruct((rows, D), F32), jax.ShapeDtypeStruct((rows, F2), F32)],
        scratch_shapes=[pltpu.VMEM((rows, D_FF), BF16)],
        compiler_params=_cparams(("arbitrary",)),
        name="ffn_ln_short",
    )(h.reshape(rows, D), pre_rows, w_up.astype(BF16), conv_w, conv_b.reshape(1, F2), w_down.astype(BF16),
      g.reshape(1, D), b.reshape(1, D))
    return out.reshape(B, T, D), st.reshape(B, T, F2)[:, T - (CONV_W - 1):, :]


def _rope_tables(pos):
    half = ROT_DIM // 2
    inv = jnp.power(ROPE_THETA, -2.0 * jnp.arange(half, dtype=F32) / ROT_DIM)
    ang = pos.astype(F32)[:, None] * inv[None, :]
    cos, sin = jnp.cos(ang), jnp.sin(ang)
    T = pos.shape[0]
    one, zero = jnp.ones((T, HEAD_DIM - ROT_DIM), F32), jnp.zeros((T, HEAD_DIM - half), F32)
    c64 = jnp.concatenate([cos, cos, one], axis=1)
    lo64 = jnp.concatenate([-sin, zero], axis=1)
    hi64 = jnp.concatenate([jnp.zeros((T, half), F32), sin, jnp.zeros((T, HEAD_DIM - ROT_DIM), F32)], axis=1)
    tile = lambda a: jnp.concatenate([a, a], axis=1)
    return tile(c64), tile(lo64), tile(hi64)


def _rope_chunk(x, c, lo, hi):
    return x * c + pltpu.roll(x, LANES - ROT_DIM // 2, axis=1) * lo + pltpu.roll(x, ROT_DIM // 2, axis=1) * hi


def _nsa_proj_kernel(h_ref, wq_ref, wkv_ref, wg_ref, c_ref, lo_ref, hi_ref, *outs, tT, head_major):
    t = pl.program_id(1)
    hb = h_ref[0].astype(BF16)
    c, lo, hi = c_ref[...], lo_ref[...], hi_ref[...]
    q = jnp.dot(hb, wq_ref[...], preferred_element_type=F32)
    kv = jnp.dot(hb, wkv_ref[...], preferred_element_type=F32)
    gl = jnp.dot(hb, wg_ref[...], preferred_element_type=F32)
    gates = jax.nn.sigmoid(gl)
    lane = lax.broadcasted_iota(jnp.int32, (tT, LANES), 1)
    low = lane < HEAD_DIM
    scale = HEAD_DIM ** -0.5 * (LOG2E if head_major else 1.0)
    qc = [_rope_chunk(q[:, i * LANES:(i + 1) * LANES], c, lo, hi) * scale for i in range(D_MODEL // LANES)]
    kvc = []
    for br in range(3):
        for i in range(KV_W // LANES):
            x = kv[:, br * KV_W + i * LANES: br * KV_W + (i + 1) * LANES]
            kvc.append(_rope_chunk(x, c, lo, hi) if i < KV_W // (2 * LANES) else x)
    if head_major:
        cmp_ref, slc_ref, win_ref, qh_ref, skp_ref, sv_ref, wk_ref, wv_ref, g_ref = outs
    else:
        cmp_ref, slc_ref, win_ref, qf_ref, g_ref = outs
    for br, ref in enumerate((cmp_ref, slc_ref, win_ref)):
        ref[0] = jnp.concatenate(kvc[br * 4:(br + 1) * 4], axis=1)
    g_ref[0] = gates
    if not head_major:
        qf_ref[0] = jnp.concatenate(qc, axis=1).astype(BF16)
        return

    def split(x):
        return jnp.where(low, x, 0.0), jnp.where(low, pltpu.roll(x, HEAD_DIM, axis=1), 0.0)

    for i in range(D_MODEL // LANES):
        a, b = split(qc[i])
        qh_ref[0, 2 * i] = a.astype(BF16)
        qh_ref[0, 2 * i + 1] = b.astype(BF16)
    blk = (t * tT + lax.broadcasted_iota(jnp.int32, (tT, LANES), 0)) // SEL_BLOCK
    onehot = jnp.where(lane - HEAD_DIM == blk, 1.0, 0.0)
    ones_col = jnp.where(lane == HEAD_DIM, 1.0, 0.0)
    for br, (kref, vref) in ((1, (skp_ref, sv_ref)), (2, (wk_ref, wv_ref))):
        for i in range(2):
            ka, kb = split(kvc[br * 4 + i])
            va, vb = split(kvc[br * 4 + 2 + i])
            if br == 1:
                ka, kb = ka + onehot, kb + onehot
            va, vb = va + ones_col, vb + ones_col
            kref[0, 2 * i] = ka.astype(BF16)
            kref[0, 2 * i + 1] = kb.astype(BF16)
            vref[0, 2 * i] = va.astype(BF16)
            vref[0, 2 * i + 1] = vb.astype(BF16)


def _nsa_proj(h, pos, w_in, tT, head_major):
    B, T, D = h.shape
    nT = T // tT
    nq = N_HEADS * HEAD_DIM
    wq = w_in[:, :nq].astype(BF16)
    wkv = w_in[:, nq:nq + 3 * KV_W].astype(BF16)
    wg = w_in[:, nq + 3 * KV_W:]
    if head_major:
        GW = KV_HEADS * LANES
        wg = jnp.pad(wg.reshape(D, KV_HEADS, Q_PER_KV * 3), ((0, 0), (0, 0), (0, LANES - Q_PER_KV * 3))).reshape(D, GW)
    else:
        GW = LANES
        wg = jnp.pad(wg, ((0, 0), (0, LANES - wg.shape[1])))
    wg = wg.astype(BF16)
    c, lo, hi = _rope_tables(pos)
    row = lambda w: pl.BlockSpec((1, tT, w), lambda bi, ti: (bi, ti, 0))
    hm = lambda n: pl.BlockSpec((1, n, tT, LANES), lambda bi, ti: (bi, 0, ti, 0))
    tab = pl.BlockSpec((tT, LANES), lambda bi, ti: (ti, 0))
    out_specs = [row(KV_W), row(KV_W), row(KV_W)]
    out_shape = [jax.ShapeDtypeStruct((B, T, KV_W), F32)] * 3
    if head_major:
        out_specs += [hm(N_HEADS)] + [hm(KV_HEADS)] * 4 + [row(GW)]
        out_shape += ([jax.ShapeDtypeStruct((B, N_HEADS, T, LANES), BF16)]
                      + [jax.ShapeDtypeStruct((B, KV_HEADS, T, LANES), BF16)] * 4
                      + [jax.ShapeDtypeStruct((B, T, GW), F32)])
    else:
        out_specs += [row(D), row(GW)]
        out_shape += [jax.ShapeDtypeStruct((B, T, D), BF16), jax.ShapeDtypeStruct((B, T, GW), F32)]
    return pl.pallas_call(
        functools.partial(_nsa_proj_kernel, tT=tT, head_major=head_major),
        grid=(B, nT),
        in_specs=[row(D), _const_spec((D, nq)), _const_spec((D, 3 * KV_W)), _const_spec((D, GW)), tab, tab, tab],
        out_specs=out_specs,
        out_shape=out_shape,
        compiler_params=_cparams(("parallel", "parallel")),
        name="nsa_proj",
    )(h, wq, wkv, wg, c, lo, hi)


def _gelu_tanh(x):
    return 0.5 * x * (1.0 + jnp.tanh(0.7978845608028654 * (x + 0.044715 * x * x * x)))


def _compress_kernel(x_ref, w1_ref, w1f_ref, pos_ref, b1_ref, w2_ref, b2_ref, ck_ref, cv_ref):
    for kvi, out_ref in enumerate((ck_ref, cv_ref)):
        b1 = b1_ref[kvi:kvi + 1, :] + jnp.dot(pos_ref[kvi], w1f_ref[kvi], preferred_element_type=F32)[0:1, :]
        for g in range(KV_HEADS):
            acc = None
            for t in range(CMP_STRIDE):
                off = t * KV_W + kvi * (KV_W // 2) + g * HEAD_DIM
                d = jnp.dot(x_ref[0, :, off:off + HEAD_DIM].astype(BF16), w1_ref[kvi, t],
                            preferred_element_type=F32)
                acc = d if acc is None else acc + d
            pr0, pr1 = acc[:, :CMP_HID], acc[:, CMP_HID:]
            hid = _gelu_tanh(pltpu.roll(pr0, 1, axis=0) + pr1 + b1)
            o = jnp.dot(hid.astype(BF16), w2_ref[kvi], preferred_element_type=F32) + b2_ref[kvi:kvi + 1, :]
            out_ref[0, g] = jnp.concatenate([o, jnp.zeros_like(o)], axis=1).astype(BF16)


def _compress_weights(cw):
    w1, b1, pos_emb, w2, b2 = cw
    assert CMP_BLOCK == 2 * CMP_STRIDE
    w1cat = jnp.concatenate([w1[:, :CMP_STRIDE], w1[:, CMP_STRIDE:]], axis=-1).astype(BF16)
    w1f = w1.reshape(2, CMP_BLOCK * HEAD_DIM, CMP_HID).astype(BF16)
    posf = jnp.broadcast_to(pos_emb.reshape(2, 1, CMP_BLOCK * HEAD_DIM), (2, SUBLANES, CMP_BLOCK * HEAD_DIM)).astype(BF16)
    return w1cat, w1f, posf, b1, w2.astype(BF16), b2


def _compress_prompt(cmp_kv, cw):
    B, T, _ = cmp_kv.shape
    n_half = T // CMP_STRIDE
    x = cmp_kv.reshape(B, n_half, CMP_STRIDE * KV_W)
    ws = _compress_weights(cw)
    o_spec = pl.BlockSpec((1, KV_HEADS, n_half, LANES), lambda bi: (bi, 0, 0, 0))
    o_shape = jax.ShapeDtypeStruct((B, KV_HEADS, n_half, LANES), BF16)
    return pl.pallas_call(
        _compress_kernel,
        grid=(B,),
        in_specs=[pl.BlockSpec((1, n_half, CMP_STRIDE * KV_W), lambda bi: (bi, 0, 0))]
        + [_const_spec(w.shape) for w in ws],
        out_specs=[o_spec, o_spec],
        out_shape=[o_shape, o_shape],
        compiler_params=_cparams(("parallel",)),
        name="compress_hm",
    )(x, *ws)


def _block_sum_matrix(n_out, n_in, n_blocks, n_cmp):
    a = np.zeros((n_out, n_in), np.float32)
    for j in range(n_blocks):
        for m in range(4 * j, 4 * j + 5):
            if 1 <= m <= n_cmp:
                a[j, m] = 1.0
    return a


def _attn_prompt_kernel(q_ref, ck_ref, cv_ref, kp_ref, vs_ref, wk_ref, wv_ref, gt_ref, at_ref, o_ref,
                        s_buf, m_run, acc_sc, oc_sc, pen_sc, *, n_cmp_rows, n_blk_rows):
    qt = pl.program_id(2)
    s0 = qt * Q_TILE
    R = Q_PER_KV * Q_TILE
    GS = ATTN_GROUPS
    Qs = [q_ref[0, gi].reshape(R, LANES) for gi in range(GS)]
    tq = s0 + (lax.broadcasted_iota(jnp.int32, (R, 1), 0) & (Q_TILE - 1))

    mrow = lax.broadcasted_iota(jnp.int32, (1, n_cmp_rows), 1)
    cend = jnp.where(mrow >= 1, mrow * CMP_STRIDE + (CMP_BLOCK - CMP_STRIDE - 1), jnp.int32(2 ** 30))
    j = lax.broadcasted_iota(jnp.int32, (n_blk_rows, Q_TILE), 0)
    tl = s0 + lax.broadcasted_iota(jnp.int32, (n_blk_rows, Q_TILE), 1)
    cur = tl // SEL_BLOCK
    valid = j * SEL_BLOCK <= tl
    forced = valid & ((j == 0) | (j == cur) | (j == cur - 1))
    scores = []
    for gi in range(GS):
        sc = lax.dot_general(Qs[gi], ck_ref[0, gi], NT_DIMS, preferred_element_type=F32)
        sc = jnp.where(cend <= tq, sc, NEG)
        e = jnp.exp2(sc - jnp.max(sc, axis=1, keepdims=True))
        inv = jnp.where(tq >= CMP_BLOCK - 1, 1.0 / jnp.sum(e, axis=1, keepdims=True), 0.0)
        p = e * inv
        oc_sc[gi] = jnp.dot(p.astype(BF16), cv_ref[0, gi], preferred_element_type=F32)
        imp = p[0:Q_TILE]
        for r in range(1, Q_PER_KV):
            imp = imp + p[r * Q_TILE:(r + 1) * Q_TILE]
        blk = lax.dot_general(at_ref[...], imp, NT_DIMS, precision=lax.Precision.HIGHEST,
                              preferred_element_type=F32)
        scores.append(jnp.where(forced, BIG, jnp.where(valid, blk, -1.0)))

    wlen = WINDOW + Q_TILE
    n_sub = wlen // Q_TILE
    w0 = pl.multiple_of(jnp.maximum(s0 - WINDOW, 0), Q_TILE)
    tri = (lax.broadcasted_iota(jnp.int32, (Q_TILE, Q_TILE), 1)
           <= lax.broadcasted_iota(jnp.int32, (Q_TILE, Q_TILE), 0))
    steady = s0 >= WINDOW
    wbias = []
    for i in range(n_sub):
        lower_ok = jnp.where(steady, i > 0, i <= qt)
        upper_ok = jnp.where(steady, i < n_sub - 1, i < qt)
        b = jnp.where(tri, jnp.where(lower_ok, 0.0, NEG), jnp.where(upper_ok, 0.0, NEG))
        wbias.append(jnp.concatenate([b] * Q_PER_KV, axis=0))
    wbias = jnp.concatenate(wbias, axis=1)
    gv = gt_ref[0]
    gcol = lambda gi, jb: jnp.concatenate(
        [gv[:, gi * LANES + 3 * r + jb:gi * LANES + 3 * r + jb + 1] for r in range(Q_PER_KV)], axis=0)
    for gi in range(GS):
        s = lax.dot_general(Qs[gi], wk_ref[0, gi, pl.ds(w0, wlen), :], NT_DIMS, preferred_element_type=F32)
        s = s + wbias
        pw = jnp.exp2(s - jnp.max(s, axis=1, keepdims=True)).astype(BF16)
        acc_w = jnp.dot(pw, wv_ref[0, gi, pl.ds(w0, wlen), :], preferred_element_type=F32)
        oc_sc[gi] = gcol(gi, 0) * oc_sc[gi] + (gcol(gi, 2) / acc_w[:, HEAD_DIM:HEAD_DIM + 1]) * acc_w

    jv = lax.broadcasted_iota(jnp.int32, (SUBLANES, Q_TILE), 0)

    def rank_rows(n_rows):
        nv = n_rows // SUBLANES
        for gi in range(GS):
            if n_rows <= N_SEL:
                pen_sc[gi] = jnp.where(valid, 0.0, NEG)
                continue
            sv = [scores[gi][v * SUBLANES:(v + 1) * SUBLANES, :] for v in range(nv)]
            cnt = [jnp.zeros((SUBLANES, Q_TILE), F32) for _ in range(nv)]
            for i in range(n_rows):
                vi, si = divmod(i, SUBLANES)
                ri = sv[vi][si:si + 1, :]
                for v in range(nv):
                    if v < vi:
                        beats = ri > sv[v]
                    elif v > vi:
                        beats = ri >= sv[v]
                    else:
                        beats = (ri > sv[v]) | ((jv > si) & (ri >= sv[v]))
                    cnt[v] = cnt[v] + jnp.where(beats, 1.0, 0.0)
            pen = jnp.where(jnp.concatenate(cnt, axis=0) < N_SEL, 0.0, NEG)
            if n_rows < n_blk_rows:
                pen = jnp.concatenate([pen, jnp.full((n_blk_rows - n_rows, Q_TILE), NEG, F32)], axis=0)
            pen_sc[gi] = jnp.where(valid, pen, NEG)

    n_poss = (s0 + Q_TILE - 1) // SEL_BLOCK + 1
    bounds = list(range(N_SEL, n_blk_rows, N_SEL)) + [n_blk_rows]
    for lo, hi in zip([0] + bounds[:-1], bounds):
        pl.when((n_poss > lo) & (n_poss <= hi))(functools.partial(rank_rows, hi))

    Qa = []
    for gi in range(GS):
        pen_rows = [jnp.zeros((HEAD_DIM, Q_TILE), F32), pen_sc[gi]]
        if n_blk_rows < HEAD_DIM:
            pen_rows.append(jnp.zeros((HEAD_DIM - n_blk_rows, Q_TILE), F32))
        pen = jnp.concatenate(pen_rows, axis=0).T
        Qa.append(Qs[gi] + jnp.concatenate([pen.astype(BF16)] * Q_PER_KV, axis=0))

    n_full = s0 // SLC_KT
    m_run[...] = jnp.full(m_run.shape, NEG, F32)
    acc_sc[...] = jnp.zeros(acc_sc.shape, F32)

    def scores_tile(kt, causal):
        k0 = pl.multiple_of(kt * SLC_KT, SLC_KT)
        for gi in range(GS):
            s = lax.dot_general(Qa[gi], kp_ref[0, gi, pl.ds(k0, SLC_KT), :], NT_DIMS, preferred_element_type=F32)
            if causal:
                kp_ = k0 + lax.broadcasted_iota(jnp.int32, (1, SLC_KT), 1)
                s = jnp.where(kp_ <= tq, s, NEG)
            s_buf[gi, kt] = s
            mx = m_run[gi]
            for c in range(SLC_KT // LANES):
                mx = jnp.maximum(mx, s[:, c * LANES:(c + 1) * LANES])
            m_run[gi] = mx

    def p1(i, carry):
        scores_tile(2 * i, False)
        scores_tile(2 * i + 1, False)
        return carry

    lax.fori_loop(0, n_full // 2, p1, 0)
    pl.when(n_full % 2 == 1)(lambda: scores_tile(n_full - 1, False))
    scores_tile(n_full, True)
    ms = [jnp.max(m_run[gi], axis=1, keepdims=True) for gi in range(GS)]

    def pv_tile(kt):
        k0 = pl.multiple_of(kt * SLC_KT, SLC_KT)
        for gi in range(GS):
            pe = jnp.exp2(s_buf[gi, kt] - ms[gi]).astype(BF16)
            acc_sc[gi] += jnp.dot(pe, vs_ref[0, gi, pl.ds(k0, SLC_KT), :], preferred_element_type=F32)

    def p2(i, carry):
        pv_tile(2 * i)
        pv_tile(2 * i + 1)
        return carry

    lax.fori_loop(0, (n_full + 1) // 2, p2, 0)
    pl.when(n_full % 2 == 0)(lambda: pv_tile(n_full))

    first_head = lax.broadcasted_iota(jnp.int32, (Q_TILE, LANES), 1) < HEAD_DIM
    for gi in range(GS):
        acc_s = acc_sc[gi]
        o = oc_sc[gi] + (gcol(gi, 1) / acc_s[:, HEAD_DIM:HEAD_DIM + 1]) * acc_s
        for pr in range(Q_PER_KV // 2):
            a = o[(2 * pr) * Q_TILE:(2 * pr + 1) * Q_TILE]
            b = pltpu.roll(o[(2 * pr + 1) * Q_TILE:(2 * pr + 2) * Q_TILE], HEAD_DIM, axis=1)
            c0 = (gi * Q_PER_KV // 2 + pr) * LANES
            o_ref[0, :, c0:c0 + LANES] = jnp.where(first_head, a, b).astype(BF16)


def _attn_prompt(qh, ck, cv, skp, sv, wk, wv, gates):
    B, _, T, _ = qh.shape
    GS = ATTN_GROUPS
    n_cmp_rows = ck.shape[2]
    n_blocks = T // SEL_BLOCK
    n_blk_rows = max(SUBLANES, n_blocks)
    assert n_blocks <= HEAD_DIM and T >= WINDOW + Q_TILE and T % SLC_KT == 0 and KV_HEADS % GS == 0
    n_cmp = T // CMP_STRIDE - CMP_BLOCK // CMP_STRIDE + 1
    at = jnp.asarray(_block_sum_matrix(n_blk_rows, n_cmp_rows, n_blocks, n_cmp))
    q5 = qh.reshape(B, KV_HEADS, Q_PER_KV, T, LANES)
    R = Q_PER_KV * Q_TILE
    full = lambda n: pl.BlockSpec((1, GS, n, LANES), lambda bi, gi, qi: (bi, gi, 0, 0),
                                  pipeline_mode=pl.Buffered(1))
    return pl.pallas_call(
        functools.partial(_attn_prompt_kernel, n_cmp_rows=n_cmp_rows, n_blk_rows=n_blk_rows),
        grid=(B, KV_HEADS // GS, T // Q_TILE),
        in_specs=[
            pl.BlockSpec((1, GS, Q_PER_KV, Q_TILE, LANES), lambda bi, gi, qi: (bi, gi, 0, qi, 0)),
            full(n_cmp_rows), full(n_cmp_rows), full(T), full(T), full(T), full(T),
            pl.BlockSpec((1, Q_TILE, GS * LANES), lambda bi, gi, qi: (bi, qi, gi)),
            pl.BlockSpec((n_blk_rows, n_cmp_rows), lambda bi, gi, qi: (0, 0)),
        ],
        out_specs=pl.BlockSpec((1, Q_TILE, GS * Q_PER_KV * HEAD_DIM), lambda bi, gi, qi: (bi, qi, gi)),
        out_shape=jax.ShapeDtypeStruct((B, T, D_MODEL), BF16),
        scratch_shapes=[pltpu.VMEM((GS, T // SLC_KT, R, SLC_KT), F32), pltpu.VMEM((GS, R, LANES), F32),
                        pltpu.VMEM((GS, R, LANES), F32), pltpu.VMEM((GS, R, LANES), F32),
                        pltpu.VMEM((GS, n_blk_rows, Q_TILE), F32)],
        compiler_params=_cparams(("parallel", "parallel", "arbitrary")),
        name="attn_prompt",
    )(q5, ck, cv, skp, sv, wk, wv, gates, at)


def _oproj_ln_kernel(h_ref, o_ref, w_ref, g_ref, b_ref, out_ref):
    y = jnp.dot(o_ref[0], w_ref[...], preferred_element_type=F32)
    out_ref[0] = _layer_norm(ALPHA * h_ref[0] + y, g_ref[...], b_ref[...])


def _oproj_ln(h, o, w_o, g, b, tT):
    B, T, D = h.shape
    row = pl.BlockSpec((1, tT, D), lambda bi, ti: (bi, ti, 0))
    return pl.pallas_call(
        _oproj_ln_kernel,
        grid=(B, T // tT),
        in_specs=[row, row, _const_spec((D, D)), _const_spec((1, D)), _const_spec((1, D))],
        out_specs=row,
        out_shape=jax.ShapeDtypeStruct((B, T, D), F32),
        compiler_params=_cparams(("parallel", "parallel")),
        name="oproj_ln",
    )(h, o, w_o.astype(BF16), g.reshape(1, D), b.reshape(1, D))


PEN_TILE = PAGES_PER_STEP * (PAGE_SIZE // SEL_BLOCK)
S1_BATCH = 4


def _compress_pg_kernel(ptab_ref, *refs, n_pg):
    pg_refs = refs[:n_pg]
    w4_ref, w1f_ref, pos_ref, b1_ref, w2_ref, b2_ref, ck_ref, cv_ref = refs[n_pg:n_pg + 8]
    xs, carry = refs[n_pg + 8:-1], refs[-1]
    pt = pl.program_id(1)

    @pl.when(pt == 0)
    def _():
        carry[...] = jnp.zeros_like(carry)

    hp = PAGE_SIZE // CMP_STRIDE
    M = n_pg * hp
    first = lax.broadcasted_iota(jnp.int32, (M, CMP_HID), 0) == 0
    for kvi, out_ref in enumerate((ck_ref, cv_ref)):
        b1 = b1_ref[kvi:kvi + 1, :] + jnp.dot(pos_ref[kvi], w1f_ref[kvi], preferred_element_type=F32)[0:1, :]
        for pair in range(KV_HEADS // 2):
            c = kvi * (KV_HEADS // 2) + pair
            for i, r in enumerate(pg_refs):
                xs[c][i * PAGE_SIZE:(i + 1) * PAGE_SIZE, :] = r[0, c * LANES:(c + 1) * LANES, :].T
            acc = None
            for tq in range(CMP_STRIDE // CMP_TGROUP):
                lhs = jnp.concatenate(
                    [xs[c][pl.ds(tq * CMP_TGROUP + k, M, stride=CMP_STRIDE), :].astype(BF16)
                     for k in range(CMP_TGROUP)], axis=1)
                d = jnp.dot(lhs, w4_ref[kvi, tq], preferred_element_type=F32)
                acc = d if acc is None else acc + d
            for gl in range(2):
                g = pair * 2 + gl
                pr0 = acc[:, gl * 2 * CMP_HID:gl * 2 * CMP_HID + CMP_HID]
                pr1 = acc[:, gl * 2 * CMP_HID + CMP_HID:(gl + 1) * 2 * CMP_HID]
                ci = kvi * KV_HEADS + g
                prev = jnp.where(first, carry[ci, 0:1, :], pltpu.roll(pr0, 1, axis=0))
                carry[ci, 0:1, :] = pr0[M - 1:M, :]
                hid = _gelu_tanh(prev + pr1 + b1)
                o = jnp.dot(hid.astype(BF16), w2_ref[kvi], preferred_element_type=F32) + b2_ref[kvi:kvi + 1, :]
                out_ref[0, :, g * HEAD_DIM:(g + 1) * HEAD_DIM] = o.astype(BF16)


def _page_specs(n_pg):
    return [pl.BlockSpec((1, KV_W, PAGE_SIZE),
                         functools.partial(lambda bi, pi, ptab, i: (ptab[bi, pi * n_pg + i], 0, 0), i=i))
            for i in range(n_pg)]


def _compress_pages(pool_t, page_table, cw):
    B, n_pages = page_table.shape
    n_pg = CMP_PAGES_PER_STEP
    assert n_pages % n_pg == 0
    hp = PAGE_SIZE // CMP_STRIDE
    w1cat, w1f, posf, b1, w2, b2 = _compress_weights(cw)
    ntq = CMP_STRIDE // CMP_TGROUP
    w1q = w1cat.reshape(2, ntq, CMP_TGROUP, 1, HEAD_DIM, 1, 2 * CMP_HID)
    eye = jnp.eye(2, dtype=w1cat.dtype).reshape(1, 1, 1, 2, 1, 2, 1)
    w4 = (w1q * eye).reshape(2, ntq, CMP_TGROUP * 2 * HEAD_DIM, 2 * 2 * CMP_HID)
    o_spec = pl.BlockSpec((1, n_pg * hp, KV_W // 2), lambda bi, pi, ptab: (bi, pi, 0))
    o_shape = jax.ShapeDtypeStruct((B, n_pages * hp, KV_W // 2), BF16)
    return pl.pallas_call(
        functools.partial(_compress_pg_kernel, n_pg=n_pg),
        grid_spec=pltpu.PrefetchScalarGridSpec(
            num_scalar_prefetch=1, grid=(B, n_pages // n_pg),
            in_specs=_page_specs(n_pg) + [_const_spec(w.shape) for w in (w4, w1f, posf, b1, w2, b2)],
            out_specs=[o_spec, o_spec],
            scratch_shapes=[pltpu.VMEM((n_pg * PAGE_SIZE, LANES), F32)] * (KV_W // LANES)
            + [pltpu.VMEM((2 * KV_HEADS, SUBLANES, CMP_HID), F32)],
        ),
        out_shape=[o_shape, o_shape],
        compiler_params=_cparams(("parallel", "arbitrary")),
        name="compress_pg",
    )(page_table, *([pool_t] * n_pg), w4, w1f, posf, b1, w2, b2)


def _row_q(shape):
    return lax.broadcasted_iota(jnp.int32, shape, 0) & 7


def _diag_heads(o_all):
    g_row = (lax.broadcasted_iota(jnp.int32, (LANES, HEAD_DIM), 0) // 8) & (KV_HEADS - 1)
    out = jnp.zeros((LANES, HEAD_DIM), F32)
    for g in range(KV_HEADS):
        out = out + jnp.where(g_row == g, o_all[:, g * HEAD_DIM:(g + 1) * HEAD_DIM], 0.0)
    return out


def _attn_s1_kernel(ck_ref, cv_ref, qbr_ref, at_ref, ocmp_ref, pen_ref, score_sc, *, past, n_blocks, n_tiles):
    n_rows = ck_ref.shape[1]
    nb_rows = at_ref.shape[0]
    mrow = lax.broadcasted_iota(jnp.int32, (1, n_rows), 1)
    cend = jnp.where(mrow >= 1, mrow * CMP_STRIDE + (CMP_BLOCK - CMP_STRIDE - 1), jnp.int32(2 ** 30))
    tq = past + _row_q((LANES, 1))
    j = lax.broadcasted_iota(jnp.int32, (nb_rows, LANES), 0)
    lane = lax.broadcasted_iota(jnp.int32, (nb_rows, LANES), 1)
    grp = lane // (LANES // Q_PER_KV)
    tl = past + (lane & 7)
    cur = tl // SEL_BLOCK
    valid = (j * SEL_BLOCK <= tl) & (j < n_blocks)
    forced = valid & ((j == 0) | (j == cur) | (j == cur - 1))
    rows = LANES // Q_PER_KV
    score = None
    for k in range(S1_BATCH):
        s = lax.dot_general(qbr_ref[k], ck_ref[k], NT_DIMS, preferred_element_type=F32)
        s = jnp.where(cend <= tq, s, NEG)
        e = jnp.exp(s - jnp.max(s, axis=1, keepdims=True))
        inv = jnp.where(tq >= CMP_BLOCK - 1, 1.0 / jnp.sum(e, axis=1, keepdims=True), 0.0)
        p = e * inv
        ocmp_ref[k] = _diag_heads(jnp.dot(p.astype(BF16), cv_ref[k], preferred_element_type=F32))
        imp = p[0:rows]
        for r in range(1, Q_PER_KV):
            imp = imp + p[r * rows:(r + 1) * rows]
        imp = jnp.concatenate([imp] * Q_PER_KV, axis=0)
        blk = lax.dot_general(at_ref[...], imp, NT_DIMS, precision=lax.Precision.HIGHEST,
                              preferred_element_type=F32)
        sk = jnp.where(forced, BIG, jnp.where(valid, blk, -1.0))
        score = sk if score is None else jnp.where(grp == k, sk, score)
    score_sc[...] = score

    def body(i, cnt):
        ri = score_sc[pl.ds(i, 1), :]
        beats = (ri > score) | ((ri == score) & (j > i))
        return cnt + jnp.where(beats, 1.0, 0.0)

    cnt = lax.fori_loop(0, n_blocks, body, jnp.zeros((nb_rows, LANES), F32))
    pen4 = jnp.where((cnt < N_SEL) & valid, 0.0, NEG)
    rolled = [pen4] + [pltpu.roll(pen4, sft * rows, axis=1) for sft in range(1, S1_BATCH)]
    n_chunks = -(-n_tiles * PEN_TILE // LANES)
    zeros = jnp.zeros((LANES, LANES - PEN_TILE), F32)
    for k in range(S1_BATCH):
        pen_t = rolled[(0 - k) % S1_BATCH]
        for m in range(1, S1_BATCH):
            pen_t = jnp.where(grp == m, rolled[(m - k) % S1_BATCH], pen_t)
        pen_t = jnp.concatenate([pen_t, jnp.full((n_chunks * LANES - nb_rows, LANES), NEG, F32)], axis=0)
        pen_r = jnp.concatenate([pen_t[c * LANES:(c + 1) * LANES, :].T for c in range(n_chunks)], axis=1)
        for t in range(n_tiles):
            pen_ref[k, t] = jnp.concatenate([pen_r[:, t * PEN_TILE:(t + 1) * PEN_TILE], zeros], axis=1)


def _attn_s2_kernel(ptab_ref, *refs, n_pg):
    pg_refs = refs[:n_pg]
    qbr_ref, pen_ref, e_ref, m_ref, l_ref, acc_ref = refs[n_pg:]
    kw = KV_W // 2
    kt = jnp.concatenate([r[0, 0:kw, :].astype(BF16) for r in pg_refs], axis=1)
    s = jnp.dot(qbr_ref[0], kt, preferred_element_type=F32)
    s = s + jnp.dot(pen_ref[0, 0].astype(BF16), e_ref[...], preferred_element_type=F32)
    m = jnp.max(s, axis=1, keepdims=True)
    pe = jnp.exp(s - m)
    l = jnp.sum(pe, axis=1, keepdims=True)
    vt = jnp.concatenate([r[0, kw:KV_W, :].astype(BF16) for r in pg_refs], axis=1)
    acc_ref[0, 0] = lax.dot_general(pe.astype(BF16), vt, NT_DIMS, preferred_element_type=F32)
    m_ref[0, 0] = jnp.broadcast_to(m, (LANES, LANES))
    l_ref[0, 0] = jnp.broadcast_to(l, (LANES, LANES))


def _attn_s3_kernel(m_ref, l_ref, acc_ref, pen_ref, ocmp_ref, qbr_ref, snew_ref, wst_ref, wnew_ref, gt_ref,
                    o_ref, *, n_pt, n_new):
    qbr = qbr_ref[0]
    kw = KV_W // 2
    nrow = snew_ref.shape[1]
    qr = _row_q((LANES, 1))
    icol = lax.broadcasted_iota(jnp.int32, (1, nrow), 1)
    new_ok = (icol <= qr) & (icol < n_new)

    xs = snew_ref[0]
    st = lax.dot_general(qbr, xs[:, 0:kw].astype(BF16), NT_DIMS, preferred_element_type=F32)
    st = jnp.where(new_ok, st + pen_ref[0, 0][:, 0:1], NEG)
    m_tot = jnp.max(st, axis=1, keepdims=True)
    for s in range(n_pt):
        m_tot = jnp.maximum(m_tot, m_ref[0, s][:, 0:1])
    pt_ = jnp.exp(st - m_tot)
    l_tot = jnp.sum(pt_, axis=1, keepdims=True)
    acc = jnp.dot(pt_.astype(BF16), xs[:, kw:KV_W].astype(BF16), preferred_element_type=F32)
    for s in range(n_pt):
        a = jnp.exp(m_ref[0, s][:, 0:1] - m_tot)
        l_tot = l_tot + a * l_ref[0, s][:, 0:1]
        acc = acc + a * acc_ref[0, s]
    o_slc = _diag_heads(acc / l_tot)

    wn = wnew_ref[0]
    nw = wst_ref.shape[2]
    s1 = jnp.dot(qbr, wst_ref[0, 0:kw, :].astype(BF16), preferred_element_type=F32)
    s2 = lax.dot_general(qbr, wn[:, 0:kw].astype(BF16), NT_DIMS, preferred_element_type=F32)
    dist1 = nw + qr - lax.broadcasted_iota(jnp.int32, (1, nw), 1)
    mask1 = (dist1 >= 0) & (dist1 < WINDOW)
    mask2 = new_ok & (qr - icol < WINDOW)
    s1 = jnp.where(mask1, s1, NEG)
    s2 = jnp.where(mask2, s2, NEG)
    mw = jnp.maximum(jnp.max(s1, axis=1, keepdims=True), jnp.max(s2, axis=1, keepdims=True))
    p1 = jnp.where(mask1, jnp.exp(s1 - mw), 0.0)
    p2 = jnp.where(mask2, jnp.exp(s2 - mw), 0.0)
    inv = 1.0 / (jnp.sum(p1, axis=1, keepdims=True) + jnp.sum(p2, axis=1, keepdims=True))
    ow = (lax.dot_general((p1 * inv).astype(BF16), wst_ref[0, kw:KV_W, :].astype(BF16), NT_DIMS,
                          preferred_element_type=F32)
          + jnp.dot((p2 * inv).astype(BF16), wn[:, kw:KV_W].astype(BF16), preferred_element_type=F32))
    o_win = _diag_heads(ow)

    gv = gt_ref[0]
    o_ref[0] = gv[:, 0:1] * ocmp_ref[0] + gv[:, 1:2] * o_slc + gv[:, 2:3] * o_win


def _attn_sample(q_flat, gates, ck, cv, slc_pool_t, page_table, slc_new, win_state_t, win_new, past):
    B, Tn, _ = q_flat.shape
    n_pg = PAGES_PER_STEP
    assert Tn == 8 and past % (PAGE_SIZE * n_pg) == 0
    n_pages = past // PAGE_SIZE
    n_pt = n_pages // n_pg
    n_past_blk = past // SEL_BLOCK
    n_blocks = n_past_blk + 1
    nb_rows = -(-n_blocks // SUBLANES) * SUBLANES
    n_cmp = past // CMP_STRIDE - 1
    Mc = ck.shape[1]
    at = jnp.asarray(_block_sum_matrix(nb_rows, Mc, n_blocks, n_cmp))
    kw = KV_W // 2

    q5 = q_flat.reshape(B, Tn, KV_HEADS, Q_PER_KV, HEAD_DIM)
    base = jnp.transpose(q5, (0, 3, 2, 1, 4))
    eye = jnp.eye(KV_HEADS, dtype=q_flat.dtype)
    qbr = (base[:, :, :, :, None, :] * eye[None, None, :, None, :, None]).reshape(B, LANES, kw)
    g5 = gates[:, :, :N_HEADS * 3].reshape(B, Tn, KV_HEADS, Q_PER_KV, 3)
    gt = jnp.pad(jnp.transpose(g5, (0, 3, 2, 1, 4)).reshape(B, LANES, 3), ((0, 0), (0, 0), (0, LANES - 3)))

    per_b = lambda *shape: pl.BlockSpec((1,) + shape, lambda bi, *_: (bi,) + (0,) * len(shape))
    assert B % S1_BATCH == 0 and S1_BATCH == Q_PER_KV
    per_sb = lambda *shape: pl.BlockSpec((S1_BATCH,) + shape, lambda bi: (bi,) + (0,) * len(shape))
    ocmp, pen = pl.pallas_call(
        functools.partial(_attn_s1_kernel, past=past, n_blocks=n_blocks, n_tiles=n_pt + 1),
        grid=(B // S1_BATCH,),
        in_specs=[per_sb(Mc, kw), per_sb(Mc, kw), per_sb(LANES, kw),
                  pl.BlockSpec((nb_rows, Mc), lambda bi: (0, 0))],
        out_specs=[per_sb(LANES, HEAD_DIM), per_sb(n_pt + 1, LANES, LANES)],
        out_shape=[jax.ShapeDtypeStruct((B, LANES, HEAD_DIM), F32),
                   jax.ShapeDtypeStruct((B, n_pt + 1, LANES, LANES), F32)],
        scratch_shapes=[pltpu.VMEM((nb_rows, LANES), F32)],
        compiler_params=_cparams(("parallel",)),
        name="attn_s1",
    )(ck, cv, qbr, at)

    expand = np.zeros((LANES, n_pg * PAGE_SIZE), np.float32)
    for jb in range(PEN_TILE):
        expand[jb, jb * SEL_BLOCK:(jb + 1) * SEL_BLOCK] = 1.0
    part = lambda *shape: pl.BlockSpec((1, 1) + shape, lambda bi, pi, ptab: (bi, pi) + (0,) * len(shape))
    m_p, l_p, acc_p = pl.pallas_call(
        functools.partial(_attn_s2_kernel, n_pg=n_pg),
        grid_spec=pltpu.PrefetchScalarGridSpec(
            num_scalar_prefetch=1, grid=(B, n_pt),
            in_specs=_page_specs(n_pg) + [pl.BlockSpec((1, LANES, kw), lambda bi, pi, ptab: (bi, 0, 0)),
                                          part(LANES, LANES),
                                          pl.BlockSpec(expand.shape, lambda bi, pi, ptab: (0, 0))],
            out_specs=[part(LANES, LANES), part(LANES, LANES), part(LANES, kw)],
        ),
        out_shape=[jax.ShapeDtypeStruct((B, n_pt, LANES, LANES), F32)] * 2
        + [jax.ShapeDtypeStruct((B, n_pt, LANES, kw), F32)],
        compiler_params=_cparams(("parallel", "parallel")),
        name="attn_s2",
    )(page_table, *([slc_pool_t] * n_pg), qbr, pen, jnp.asarray(expand, BF16))

    pad_rows = 16 - Tn
    snew = jnp.pad(slc_new, ((0, 0), (0, pad_rows), (0, 0)))
    wnew = jnp.pad(win_new, ((0, 0), (0, pad_rows), (0, 0)))
    nw = win_state_t.shape[2]
    o = pl.pallas_call(
        functools.partial(_attn_s3_kernel, n_pt=n_pt, n_new=Tn),
        grid=(B,),
        in_specs=[per_b(n_pt, LANES, LANES), per_b(n_pt, LANES, LANES), per_b(n_pt, LANES, kw),
                  pl.BlockSpec((1, 1, LANES, LANES), lambda bi: (bi, n_pt, 0, 0)),
                  per_b(LANES, HEAD_DIM), per_b(LANES, kw),
                  per_b(16, KV_W), per_b(KV_W, nw), per_b(16, KV_W), per_b(LANES, LANES)],
        out_specs=per_b(LANES, HEAD_DIM),
        out_shape=jax.ShapeDtypeStruct((B, LANES, HEAD_DIM), F32),
        compiler_params=_cparams(("parallel",)),
        name="attn_s3",
    )(m_p, l_p, acc_p, pen, ocmp, qbr, snew, win_state_t, wnew, gt)
    o = jnp.transpose(o.reshape(B, Q_PER_KV, KV_HEADS, Tn, HEAD_DIM), (0, 3, 2, 1, 4))
    return o.reshape(B, Tn, D_MODEL).astype(BF16)


def _nsa_prompt_layer(h, w_in, w_o, cw, g, b, tT):
    B, T, _ = h.shape
    cmp_kv, slc_kv, win_kv, qh, skp, sv, wk, wv, gates = _nsa_proj(h, jnp.arange(T), w_in, tT, True)
    ck, cv = _compress_prompt(cmp_kv, cw)
    o = _attn_prompt(qh, ck, cv, skp, sv, wk, wv, gates)
    h2 = _oproj_ln(h, o, w_o, g, b, tT)
    win_len = min(WINDOW, T)
    return h2, cmp_kv, slc_kv, win_kv[:, T - win_len:]


def _pages_device_layout(pool):
    return jnp.transpose(pool, (0, 2, 3, 4, 1)).reshape(pool.shape[0], KV_W, pool.shape[1])


def _nsa_sample_layer(h, cmp_pool, slc_pool, win_buf, page_table, w_in, w_o, cw, g, b):
    B, Tn, _ = h.shape
    n_pages = page_table.shape[1]
    past = n_pages * PAGE_SIZE
    cmp_kv, slc_kv, win_kv, q_flat, gates = _nsa_proj(h, past + jnp.arange(Tn), w_in, Tn, False)
    assert Tn < CMP_STRIDE
    ck, cv = _compress_pages(_pages_device_layout(cmp_pool), page_table, cw)
    o = _attn_sample(q_flat, gates, ck, cv, _pages_device_layout(slc_pool), page_table, slc_kv,
                     _pages_device_layout(win_buf), win_kv, past)
    h2 = _oproj_ln(h, o, w_o, g, b, Tn)
    buf_len = win_buf.shape[1]
    new_win = jnp.concatenate([win_buf, win_kv.reshape(B, Tn, 2, KV_HEADS, HEAD_DIM)], axis=1)[:, -buf_len:]
    return h2, cmp_kv, slc_kv, new_win.reshape(B, buf_len, KV_W)


def _row_tile(T):
    return 512 if T % 512 == 0 else T


def kernel(x_prompt, x_sample, state_pool, cache_cmp_kv, cache_slc_kv, state_win_kv, state_ffn, page_table, ln_g, ln_b, pool_w, pool_scale, nsa_w_in, nsa_w_o, cmp_w1, cmp_b1, cmp_pos, cmp_w2, cmp_b2, ffn_w_up, ffn_conv_w, ffn_conv_b, ffn_w_down):
    Bp, T, D = x_prompt.shape
    Bs, Tn, _ = x_sample.shape
    past = page_table.shape[1] * PAGE_SIZE
    kv_shape = (2, KV_HEADS, HEAD_DIM)
    tp, ts = _row_tile(T), Tn

    hp = _pool_ln(x_prompt, jnp.zeros((Bp, POOL_BUF, D), F32), 0, pool_w[0], pool_scale[0], ln_g[0, 0], ln_b[0, 0], tp)
    hs = _pool_ln(x_sample, state_pool[0], past, pool_w[0], pool_scale[0], ln_g[0, 0], ln_b[0, 0], ts)
    pool_p = jnp.concatenate([jnp.zeros((Bp, POOL_BUF, D), F32), x_prompt], axis=1)[:, -POOL_BUF:][None]
    pool_s = jnp.concatenate([state_pool[0], x_sample], axis=1)[:, -POOL_BUF:][None]
    ffn = lambda h, pre, i, t: _ffn_ln(h, pre, ffn_w_up[i], ffn_conv_w[i], ffn_conv_b[i], ffn_w_down[i],
                                       ln_g[i, 1], ln_b[i, 1], t)
    zero_pre = jnp.zeros((Bp, CONV_W - 1, 2 * D_FF), F32)
    hp, ffn_p0 = ffn(hp, zero_pre, 0, tp)
    ffn_s = lambda h, pre, i: _ffn_ln_short(h, pre, ffn_w_up[i], ffn_conv_w[i], ffn_conv_b[i], ffn_w_down[i],
                                            ln_g[i, 1], ln_b[i, 1])
    hs, ffn_s0 = ffn_s(hs, state_ffn[0], 0)

    cw = (cmp_w1[0], cmp_b1[0], cmp_pos[0], cmp_w2[0], cmp_b2[0])
    hp, c_p, s_p, w_p = _nsa_prompt_layer(hp, nsa_w_in[0], nsa_w_o[0], cw, ln_g[1, 0], ln_b[1, 0], tp)
    hs, c_s, s_s, w_s = _nsa_sample_layer(hs, cache_cmp_kv[0], cache_slc_kv[0], state_win_kv[0], page_table,
                                          nsa_w_in[0], nsa_w_o[0], cw, ln_g[1, 0], ln_b[1, 0])
    hp, ffn_p1 = ffn(hp, zero_pre, 1, tp)
    hs, ffn_s1 = ffn_s(hs, state_ffn[1], 1)

    kv5 = lambda a: a.reshape(a.shape[0], a.shape[1], *kv_shape)[None]
    return (hp, hs, pool_p, pool_s, kv5(c_p), kv5(c_s), kv5(s_p), kv5(s_s), kv5(w_p), kv5(w_s),
            jnp.stack([ffn_p0, ffn_p1]), jnp.stack([ffn_s0, ffn_s1]))
```

```python
import functools

import numpy as np
import jax
import jax.numpy as jnp
from jax import lax
from jax.experimental import pallas as pl
from jax.experimental.pallas import tpu as pltpu

D_MODEL = 1024
DEPTH = 2
ALPHA = (2.0 * DEPTH) ** 0.25
LN_EPS = 1e-5
POOL_WINDOWS = (2, 4, 8, 16)
POOL_CH = D_MODEL // len(POOL_WINDOWS)
POOL_BUF = max(POOL_WINDOWS) - 1
N_HEADS = 16
KV_HEADS = 4
HEAD_DIM = 64
Q_PER_KV = N_HEADS // KV_HEADS
ROT_DIM = HEAD_DIM // 4
ROPE_THETA = 500000.0
CMP_BLOCK = 32
CMP_STRIDE = 16
CMP_HID = 128
SEL_BLOCK = 64
N_SEL = 16
WINDOW = 512
KV_W = 2 * KV_HEADS * HEAD_DIM
D_FF = 2816
CONV_W = 3
PAGE_SIZE = 128
NEG = -1e30
BIG = 1e9
LOG2E = 1.4426950408889634

LANES = 128
SUBLANES = 8
Q_TILE = 128
SLC_KT = 512
ATTN_GROUPS = 4
PAGES_PER_STEP = 64
CMP_PAGES_PER_STEP = 64
CMP_TGROUP = 4
VMEM_LIMIT = 60 * 1024 * 1024

F32 = jnp.float32
BF16 = jnp.bfloat16
NT_DIMS = (((1,), (1,)), ((), ()))


def _cparams(sem):
    return pltpu.CompilerParams(dimension_semantics=sem, vmem_limit_bytes=VMEM_LIMIT)


def _layer_norm(h, g, b):
    mu = jnp.mean(h, axis=-1, keepdims=True)
    hc = h - mu
    var = jnp.mean(hc * hc, axis=-1, keepdims=True)
    return hc * lax.rsqrt(var + LN_EPS) * g + b


def _const_spec(shape):
    nd = len(shape)
    return pl.BlockSpec(shape, lambda *_: (0,) * nd, pipeline_mode=pl.Buffered(1))


def _pool_ln_kernel(x_ref, halo_ref, pre_ref, w_ref, sc_ref, g_ref, b_ref, o_ref, buf, *, tT, pos0):
    t = pl.program_id(1)
    x = x_ref[0]
    buf[0:16, :] = jnp.where(t == 0, pre_ref[0], halo_ref[0])
    buf[16:16 + tT, :] = x
    pos = pos0 + t * tT + lax.broadcasted_iota(jnp.int32, (tT, 1), 0)
    ys = []
    for gi, win in enumerate(POOL_WINDOWS):
        c0 = gi * POOL_CH
        xg = x[:, c0:c0 + POOL_CH]
        s = xg
        for i in range(1, win):
            s = s + buf[pl.ds(16 - i, tT), c0:c0 + POOL_CH]
        cnt = jnp.minimum(pos + 1, win).astype(F32)
        d = s / cnt - xg
        ys.append(jnp.dot(d.astype(BF16), w_ref[gi], preferred_element_type=F32))
    y = jnp.concatenate(ys, axis=1) * sc_ref[...]
    o_ref[0] = _layer_norm(ALPHA * x + y, g_ref[...], b_ref[...])


def _pool_ln(x, prefix, pos0, w, scale, g, b, tT):
    B, T, D = x.shape
    nT = T // tT
    pre16 = jnp.concatenate([jnp.zeros((B, 1, D), x.dtype), prefix], axis=1)
    if T >= 16:
        halo_src = x
        per = tT // 16
        halo_spec = pl.BlockSpec((1, 16, D), lambda bi, ti: (bi, jnp.maximum(ti * per - 1, 0), 0))
    else:
        halo_src = pre16
        halo_spec = pl.BlockSpec((1, 16, D), lambda bi, ti: (bi, 0, 0))
    return pl.pallas_call(
        functools.partial(_pool_ln_kernel, tT=tT, pos0=pos0),
        grid=(B, nT),
        in_specs=[
            pl.BlockSpec((1, tT, D), lambda bi, ti: (bi, ti, 0)),
            halo_spec,
            pl.BlockSpec((1, 16, D), lambda bi, ti: (bi, 0, 0)),
            _const_spec((len(POOL_WINDOWS), POOL_CH, POOL_CH)),
            _const_spec((1, D)), _const_spec((1, D)), _const_spec((1, D)),
        ],
        out_specs=pl.BlockSpec((1, tT, D), lambda bi, ti: (bi, ti, 0)),
        out_shape=jax.ShapeDtypeStruct((B, T, D), F32),
        scratch_shapes=[pltpu.VMEM((16 + tT, D), F32)],
        compiler_params=_cparams(("parallel", "parallel")),
        name="pool_ln",
    )(x, halo_src, pre16, w.astype(BF16), scale.reshape(1, D), g.reshape(1, D), b.reshape(1, D))


FFN_CHUNK = 256


def _ffn_ln_kernel(h_ref, pre_ref, wup_ref, cw_ref, cb_ref, wdn_ref, g_ref, b_ref, o_ref, st_ref,
                   ubuf, act, *, tT):
    t = pl.program_id(1)

    @pl.when(t == 0)
    def _():
        st_ref[0] = pre_ref[0]

    h = h_ref[0]
    hb = h.astype(BF16)
    fc = FFN_CHUNK
    for j in range(D_FF // fc):
        cs = []
        for half in range(2):
            c0 = half * D_FF + j * fc
            up = jnp.dot(hb, wup_ref[:, c0:c0 + fc], preferred_element_type=F32)
            ubuf[half, 0:8, :] = st_ref[0, :, c0:c0 + fc]
            ubuf[half, 8:8 + tT, :] = up
            st_ref[0, :, c0:c0 + fc] = ubuf[half, tT:tT + 8, :]
            c = (cb_ref[:, c0:c0 + fc]
                 + cw_ref[0:1, c0:c0 + fc] * ubuf[half, pl.ds(6, tT), :]
                 + cw_ref[1:2, c0:c0 + fc] * ubuf[half, pl.ds(7, tT), :]
                 + cw_ref[2:3, c0:c0 + fc] * up)
            cs.append(c)
        a = cs[0] * jax.nn.sigmoid(cs[0]) * cs[1]
        act[:, j * fc:(j + 1) * fc] = a.astype(BF16)
    y = jnp.dot(act[...], wdn_ref[...], preferred_element_type=F32)
    o_ref[0] = _layer_norm(ALPHA * h + y, g_ref[...], b_ref[...])


def _ffn_ln(h, prefix, w_up, conv_w, conv_b, w_down, g, b, tT):
    B, T, D = h.shape
    nT = T // tT
    F2 = 2 * D_FF
    pre8 = jnp.concatenate([jnp.zeros((B, 8 - (CONV_W - 1), F2), F32), prefix], axis=1)
    out, st = pl.pallas_call(
        functools.partial(_ffn_ln_kernel, tT=tT),
        grid=(B, nT),
        in_specs=[
            pl.BlockSpec((1, tT, D), lambda bi, ti: (bi, ti, 0)),
            pl.BlockSpec((1, 8, F2), lambda bi, ti: (bi, 0, 0)),
            _const_spec((D, F2)),
            _const_spec((CONV_W, F2)),
            _const_spec((1, F2)),
            _const_spec((D_FF, D)),
            _const_spec((1, D)), _const_spec((1, D)),
        ],
        out_specs=[pl.BlockSpec((1, tT, D), lambda bi, ti: (bi, ti, 0)),
                   pl.BlockSpec((1, 8, F2), lambda bi, ti: (bi, 0, 0))],
        out_shape=[jax.ShapeDtypeStruct((B, T, D), F32), jax.ShapeDtypeStruct((B, 8, F2), F32)],
        scratch_shapes=[pltpu.VMEM((2, tT + 8, FFN_CHUNK), F32), pltpu.VMEM((tT, D_FF), BF16)],
        compiler_params=_cparams(("parallel", "arbitrary")),
        name="ffn_ln",
    )(h, pre8, w_up.astype(BF16), conv_w, conv_b.reshape(1, F2), w_down.astype(BF16),
      g.reshape(1, D), b.reshape(1, D))
    return out, st[:, 8 - (CONV_W - 1):, :]


def _ffn_ln_short_kernel(h_ref, pre_ref, wup_ref, cw_ref, cb_ref, wdn_ref, g_ref, b_ref, o_ref, st_ref, act,
                         *, seq):
    h = h_ref[...]
    hb = h.astype(BF16)
    rows = h.shape[0]
    fc = FFN_CHUNK
    t = lax.broadcasted_iota(jnp.int32, (rows, fc), 0) & (seq - 1)
    for j in range(D_FF // fc):
        cs = []
        for half in range(2):
            c0 = half * D_FF + j * fc
            up = jnp.dot(hb, wup_ref[:, c0:c0 + fc], preferred_element_type=F32)
            st_ref[:, c0:c0 + fc] = up
            pre = pre_ref[:, c0:c0 + fc]
            s1 = jnp.where(t >= 1, pltpu.roll(up, 1, axis=0), pltpu.roll(pre, 1, axis=0))
            s2 = jnp.where(t >= 2, pltpu.roll(up, 2, axis=0), pltpu.roll(pre, 2, axis=0))
            cs.append(cb_ref[:, c0:c0 + fc] + cw_ref[0:1, c0:c0 + fc] * s2 + cw_ref[1:2, c0:c0 + fc] * s1
                      + cw_ref[2:3, c0:c0 + fc] * up)
        a = cs[0] * jax.nn.sigmoid(cs[0]) * cs[1]
        act[:, j * fc:(j + 1) * fc] = a.astype(BF16)
    y = jnp.dot(act[...], wdn_ref[...], preferred_element_type=F32)
    o_ref[...] = _layer_norm(ALPHA * h + y, g_ref[...], b_ref[...])


def _ffn_ln_short(h, prefix, w_up, conv_w, conv_b, w_down, g, b):
    B, T, D = h.shape
    assert T == SUBLANES and CONV_W - 1 <= T
    F2 = 2 * D_FF
    rows = B * T
    pre8 = jnp.concatenate([jnp.zeros((B, T - (CONV_W - 1), F2), F32), prefix], axis=1)
    pre_rows = jnp.roll(pre8, -1, axis=0).reshape(rows, F2)
    full = lambda *shape: pl.BlockSpec(shape, lambda i: (0,) * len(shape))
    out, st = pl.pallas_call(
        functools.partial(_ffn_ln_short_kernel, seq=T),
        grid=(1,),
        in_specs=[_const_spec(sh) for sh in ((rows, D), (rows, F2), (D, F2), (CONV_W, F2), (1, F2), (D_FF, D),
                                             (1, D), (1, D))],
        out_specs=[full(rows, D), full(rows, F2)],
        out_shape=[jax.ShapeDtypeStruct((rows, D), F32), jax.ShapeDtypeStruct((rows, F2), F32)],
        scratch_shapes=[pltpu.VMEM((rows, D_FF), BF16)],
        compiler_params=_cparams(("arbitrary",)),
        name="ffn_ln_short",
    )(h.reshape(rows, D), pre_rows, w_up.astype(BF16), conv_w, conv_b.reshape(1, F2), w_down.astype(BF16),
      g.reshape(1, D), b.reshape(1, D))
    return out.reshape(B, T, D), st.reshape(B, T, F2)[:, T - (CONV_W - 1):, :]


def _rope_tables(pos):
    half = ROT_DIM // 2
    inv = jnp.power(ROPE_THETA, -2.0 * jnp.arange(half, dtype=F32) / ROT_DIM)
    ang = pos.astype(F32)[:, None] * inv[None, :]
    cos, sin = jnp.cos(ang), jnp.sin(ang)
    T = pos.shape[0]
    one, zero = jnp.ones((T, HEAD_DIM - ROT_DIM), F32), jnp.zeros((T, HEAD_DIM - half), F32)
    c64 = jnp.concatenate([cos, cos, one], axis=1)
    lo64 = jnp.concatenate([-sin, zero], axis=1)
    hi64 = jnp.concatenate([jnp.zeros((T, half), F32), sin, jnp.zeros((T, HEAD_DIM - ROT_DIM), F32)], axis=1)
    tile = lambda a: jnp.concatenate([a, a], axis=1)
    return tile(c64), tile(lo64), tile(hi64)


def _rope_chunk(x, c, lo, hi):
    return x * c + pltpu.roll(x, LANES - ROT_DIM // 2, axis=1) * lo + pltpu.roll(x, ROT_DIM // 2, axis=1) * hi


def _nsa_proj_kernel(h_ref, wq_ref, wkv_ref, wg_ref, c_ref, lo_ref, hi_ref, *outs, tT, head_major):
    t = pl.program_id(1)
    hb = h_ref[0].astype(BF16)
    c, lo, hi = c_ref[...], lo_ref[...], hi_ref[...]
    q = jnp.dot(hb, wq_ref[...], preferred_element_type=F32)
    kv = jnp.dot(hb, wkv_ref[...], preferred_element_type=F32)
    gl = jnp.dot(hb, wg_ref[...], preferred_element_type=F32)
    gates = jax.nn.sigmoid(gl)
    lane = lax.broadcasted_iota(jnp.int32, (tT, LANES), 1)
    low = lane < HEAD_DIM
    scale = HEAD_DIM ** -0.5 * (LOG2E if head_major else 1.0)
    qc = [_rope_chunk(q[:, i * LANES:(i + 1) * LANES], c, lo, hi) * scale for i in range(D_MODEL // LANES)]
    kvc = []
    for br in range(3):
        for i in range(KV_W // LANES):
            x = kv[:, br * KV_W + i * LANES: br * KV_W + (i + 1) * LANES]
            kvc.append(_rope_chunk(x, c, lo, hi) if i < KV_W // (2 * LANES) else x)
    if head_major:
        cmp_ref, slc_ref, win_ref, qh_ref, skp_ref, sv_ref, wk_ref, wv_ref, g_ref = outs
    else:
        cmp_ref, slc_ref, win_ref, qf_ref, g_ref = outs
    for br, ref in enumerate((cmp_ref, slc_ref, win_ref)):
        ref[0] = jnp.concatenate(kvc[br * 4:(br + 1) * 4], axis=1)
    g_ref[0] = gates
    if not head_major:
        qf_ref[0] = jnp.concatenate(qc, axis=1).astype(BF16)
        return

    def split(x):
        return jnp.where(low, x, 0.0), jnp.where(low, pltpu.roll(x, HEAD_DIM, axis=1), 0.0)

    for i in range(D_MODEL // LANES):
        a, b = split(qc[i])
        qh_ref[0, 2 * i] = a.astype(BF16)
        qh_ref[0, 2 * i + 1] = b.astype(BF16)
    blk = (t * tT + lax.broadcasted_iota(jnp.int32, (tT, LANES), 0)) // SEL_BLOCK
    onehot = jnp.where(lane - HEAD_DIM == blk, 1.0, 0.0)
    ones_col = jnp.where(lane == HEAD_DIM, 1.0, 0.0)
    for br, (kref, vref) in ((1, (skp_ref, sv_ref)), (2, (wk_ref, wv_ref))):
        for i in range(2):
            ka, kb = split(kvc[br * 4 + i])
            va, vb = split(kvc[br * 4 + 2 + i])
            if br == 1:
                ka, kb = ka + onehot, kb + onehot
            va, vb = va + ones_col, vb + ones_col
            kref[0, 2 * i] = ka.astype(BF16)
            kref[0, 2 * i + 1] = kb.astype(BF16)
            vref[0, 2 * i] = va.astype(BF16)
            vref[0, 2 * i + 1] = vb.astype(BF16)


def _nsa_proj(h, pos, w_in, tT, head_major):
    B, T, D = h.shape
    nT = T // tT
    nq = N_HEADS * HEAD_DIM
    wq = w_in[:, :nq].astype(BF16)
    wkv = w_in[:, nq:nq + 3 * KV_W].astype(BF16)
    wg = w_in[:, nq + 3 * KV_W:]
    if head_major:
        GW = KV_HEADS * LANES
        wg = jnp.pad(wg.reshape(D, KV_HEADS, Q_PER_KV * 3), ((0, 0), (0, 0), (0, LANES - Q_PER_KV * 3))).reshape(D, GW)
    else:
        GW = LANES
        wg = jnp.pad(wg, ((0, 0), (0, LANES - wg.shape[1])))
    wg = wg.astype(BF16)
    c, lo, hi = _rope_tables(pos)
    row = lambda w: pl.BlockSpec((1, tT, w), lambda bi, ti: (bi, ti, 0))
    hm = lambda n: pl.BlockSpec((1, n, tT, LANES), lambda bi, ti: (bi, 0, ti, 0))
    tab = pl.BlockSpec((tT, LANES), lambda bi, ti: (ti, 0))
    out_specs = [row(KV_W), row(KV_W), row(KV_W)]
    out_shape = [jax.ShapeDtypeStruct((B, T, KV_W), F32)] * 3
    if head_major:
        out_specs += [hm(N_HEADS)] + [hm(KV_HEADS)] * 4 + [row(GW)]
        out_shape += ([jax.ShapeDtypeStruct((B, N_HEADS, T, LANES), BF16)]
                      + [jax.ShapeDtypeStruct((B, KV_HEADS, T, LANES), BF16)] * 4
                      + [jax.ShapeDtypeStruct((B, T, GW), F32)])
    else:
        out_specs += [row(D), row(GW)]
        out_shape += [jax.ShapeDtypeStruct((B, T, D), BF16), jax.ShapeDtypeStruct((B, T, GW), F32)]
    return pl.pallas_call(
        functools.partial(_nsa_proj_kernel, tT=tT, head_major=head_major),
        grid=(B, nT),
        in_specs=[row(D), _const_spec((D, nq)), _const_spec((D, 3 * KV_W)), _const_spec((D, GW)), tab, tab, tab],
        out_specs=out_specs,
        out_shape=out_shape,
        compiler_params=_cparams(("parallel", "parallel")),
        name="nsa_proj",
    )(h, wq, wkv, wg, c, lo, hi)


def _gelu_tanh(x):
    return 0.5 * x * (1.0 + jnp.tanh(0.7978845608028654 * (x + 0.044715 * x * x * x)))


def _compress_kernel(x_ref, w1_ref, w1f_ref, pos_ref, b1_ref, w2_ref, b2_ref, ck_ref, cv_ref):
    for kvi, out_ref in enumerate((ck_ref, cv_ref)):
        b1 = b1_ref[kvi:kvi + 1, :] + jnp.dot(pos_ref[kvi], w1f_ref[kvi], preferred_element_type=F32)[0:1, :]
        for g in range(KV_HEADS):
            acc = None
            for t in range(CMP_STRIDE):
                off = t * KV_W + kvi * (KV_W // 2) + g * HEAD_DIM
                d = jnp.dot(x_ref[0, :, off:off + HEAD_DIM].astype(BF16), w1_ref[kvi, t],
                            preferred_element_type=F32)
                acc = d if acc is None else acc + d
            pr0, pr1 = acc[:, :CMP_HID], acc[:, CMP_HID:]
            hid = _gelu_tanh(pltpu.roll(pr0, 1, axis=0) + pr1 + b1)
            o = jnp.dot(hid.astype(BF16), w2_ref[kvi], preferred_element_type=F32) + b2_ref[kvi:kvi + 1, :]
            out_ref[0, g] = jnp.concatenate([o, jnp.zeros_like(o)], axis=1).astype(BF16)


def _compress_weights(cw):
    w1, b1, pos_emb, w2, b2 = cw
    assert CMP_BLOCK == 2 * CMP_STRIDE
    w1cat = jnp.concatenate([w1[:, :CMP_STRIDE], w1[:, CMP_STRIDE:]], axis=-1).astype(BF16)
    w1f = w1.reshape(2, CMP_BLOCK * HEAD_DIM, CMP_HID).astype(BF16)
    posf = jnp.broadcast_to(pos_emb.reshape(2, 1, CMP_BLOCK * HEAD_DIM), (2, SUBLANES, CMP_BLOCK * HEAD_DIM)).astype(BF16)
    return w1cat, w1f, posf, b1, w2.astype(BF16), b2


def _compress_prompt(cmp_kv, cw):
    B, T, _ = cmp_kv.shape
    n_half = T // CMP_STRIDE
    x = cmp_kv.reshape(B, n_half, CMP_STRIDE * KV_W)
    ws = _compress_weights(cw)
    o_spec = pl.BlockSpec((1, KV_HEADS, n_half, LANES), lambda bi: (bi, 0, 0, 0))
    o_shape = jax.ShapeDtypeStruct((B, KV_HEADS, n_half, LANES), BF16)
    return pl.pallas_call(
        _compress_kernel,
        grid=(B,),
        in_specs=[pl.BlockSpec((1, n_half, CMP_STRIDE * KV_W), lambda bi: (bi, 0, 0))]
        + [_const_spec(w.shape) for w in ws],
        out_specs=[o_spec, o_spec],
        out_shape=[o_shape, o_shape],
        compiler_params=_cparams(("parallel",)),
        name="compress_hm",
    )(x, *ws)


def _block_sum_matrix(n_out, n_in, n_blocks, n_cmp):
    a = np.zeros((n_out, n_in), np.float32)
    for j in range(n_blocks):
        for m in range(4 * j, 4 * j + 5):
            if 1 <= m <= n_cmp:
                a[j, m] = 1.0
    return a


def _attn_prompt_kernel(q_ref, ck_ref, cv_ref, kp_ref, vs_ref, wk_ref, wv_ref, gt_ref, at_ref, o_ref,
                        s_buf, m_run, acc_sc, oc_sc, pen_sc, *, n_cmp_rows, n_blk_rows):
    qt = pl.program_id(2)
    s0 = qt * Q_TILE
    R = Q_PER_KV * Q_TILE
    GS = ATTN_GROUPS
    Qs = [q_ref[0, gi].reshape(R, LANES) for gi in range(GS)]
    tq = s0 + (lax.broadcasted_iota(jnp.int32, (R, 1), 0) & (Q_TILE - 1))

    mrow = lax.broadcasted_iota(jnp.int32, (1, n_cmp_rows), 1)
    cend = jnp.where(mrow >= 1, mrow * CMP_STRIDE + (CMP_BLOCK - CMP_STRIDE - 1), jnp.int32(2 ** 30))
    j = lax.broadcasted_iota(jnp.int32, (n_blk_rows, Q_TILE), 0)
    tl = s0 + lax.broadcasted_iota(jnp.int32, (n_blk_rows, Q_TILE), 1)
    cur = tl // SEL_BLOCK
    valid = j * SEL_BLOCK <= tl
    forced = valid & ((j == 0) | (j == cur) | (j == cur - 1))
    scores = []
    for gi in range(GS):
        sc = lax.dot_general(Qs[gi], ck_ref[0, gi], NT_DIMS, preferred_element_type=F32)
        sc = jnp.where(cend <= tq, sc, NEG)
        e = jnp.exp2(sc - jnp.max(sc, axis=1, keepdims=True))
        inv = jnp.where(tq >= CMP_BLOCK - 1, 1.0 / jnp.sum(e, axis=1, keepdims=True), 0.0)
        p = e * inv
        oc_sc[gi] = jnp.dot(p.astype(BF16), cv_ref[0, gi], preferred_element_type=F32)
        imp = p[0:Q_TILE]
        for r in range(1, Q_PER_KV):
            imp = imp + p[r * Q_TILE:(r + 1) * Q_TILE]
        blk = lax.dot_general(at_ref[...], imp, NT_DIMS, precision=lax.Precision.HIGHEST,
                              preferred_element_type=F32)
        scores.append(jnp.where(forced, BIG, jnp.where(valid, blk, -1.0)))

    wlen = WINDOW + Q_TILE
    n_sub = wlen // Q_TILE
    w0 = pl.multiple_of(jnp.maximum(s0 - WINDOW, 0), Q_TILE)
    tri = (lax.broadcasted_iota(jnp.int32, (Q_TILE, Q_TILE), 1)
           <= lax.broadcasted_iota(jnp.int32, (Q_TILE, Q_TILE), 0))
    steady = s0 >= WINDOW
    wbias = []
    for i in range(n_sub):
        lower_ok = jnp.where(steady, i > 0, i <= qt)
        upper_ok = jnp.where(steady, i < n_sub - 1, i < qt)
        b = jnp.where(tri, jnp.where(lower_ok, 0.0, NEG), jnp.where(upper_ok, 0.0, NEG))
        wbias.append(jnp.concatenate([b] * Q_PER_KV, axis=0))
    wbias = jnp.concatenate(wbias, axis=1)
    gv = gt_ref[0]
    gcol = lambda gi, jb: jnp.concatenate(
        [gv[:, gi * LANES + 3 * r + jb:gi * LANES + 3 * r + jb + 1] for r in range(Q_PER_KV)], axis=0)
    for gi in range(GS):
        s = lax.dot_general(Qs[gi], wk_ref[0, gi, pl.ds(w0, wlen), :], NT_DIMS, preferred_element_type=F32)
        s = s + wbias
        pw = jnp.exp2(s - jnp.max(s, axis=1, keepdims=True)).astype(BF16)
        acc_w = jnp.dot(pw, wv_ref[0, gi, pl.ds(w0, wlen), :], preferred_element_type=F32)
        oc_sc[gi] = gcol(gi, 0) * oc_sc[gi] + (gcol(gi, 2) / acc_w[:, HEAD_DIM:HEAD_DIM + 1]) * acc_w

    jv = lax.broadcasted_iota(jnp.int32, (SUBLANES, Q_TILE), 0)

    def rank_rows(n_rows):
        nv = n_rows // SUBLANES
        for gi in range(GS):
            if n_rows <= N_SEL:
                pen_sc[gi] = jnp.where(valid, 0.0, NEG)
                continue
            sv = [scores[gi][v * SUBLANES:(v + 1) * SUBLANES, :] for v in range(nv)]
            cnt = [jnp.zeros((SUBLANES, Q_TILE), F32) for _ in range(nv)]
            for i in range(n_rows):
                vi, si = divmod(i, SUBLANES)
                ri = sv[vi][si:si + 1, :]
                for v in range(nv):
                    if v < vi:
                        beats = ri > sv[v]
                    elif v > vi:
                        beats = ri >= sv[v]
                    else:
                        beats = (ri > sv[v]) | ((jv > si) & (ri >= sv[v]))
                    cnt[v] = cnt[v] + jnp.where(beats, 1.0, 0.0)
            pen = jnp.where(jnp.concatenate(cnt, axis=0) < N_SEL, 0.0, NEG)
            if n_rows < n_blk_rows:
                pen = jnp.concatenate([pen, jnp.full((n_blk_rows - n_rows, Q_TILE), NEG, F32)], axis=0)
            pen_sc[gi] = jnp.where(valid, pen, NEG)

    n_poss = (s0 + Q_TILE - 1) // SEL_BLOCK + 1
    bounds = list(range(N_SEL, n_blk_rows, N_SEL)) + [n_blk_rows]
    for lo, hi in zip([0] + bounds[:-1], bounds):
        pl.when((n_poss > lo) & (n_poss <= hi))(functools.partial(rank_rows, hi))

    Qa = []
    for gi in range(GS):
        pen_rows = [jnp.zeros((HEAD_DIM, Q_TILE), F32), pen_sc[gi]]
        if n_blk_rows < HEAD_DIM:
            pen_rows.append(jnp.zeros((HEAD_DIM - n_blk_rows, Q_TILE), F32))
        pen = jnp.concatenate(pen_rows, axis=0).T
        Qa.append(Qs[gi] + jnp.concatenate([pen.astype(BF16)] * Q_PER_KV, axis=0))

    n_full = s0 // SLC_KT
    m_run[...] = jnp.full(m_run.shape, NEG, F32)
    acc_sc[...] = jnp.zeros(acc_sc.shape, F32)

    def scores_tile(kt, causal):
        k0 = pl.multiple_of(kt * SLC_KT, SLC_KT)
        for gi in range(GS):
            s = lax.dot_general(Qa[gi], kp_ref[0, gi, pl.ds(k0, SLC_KT), :], NT_DIMS, preferred_element_type=F32)
            if causal:
                kp_ = k0 + lax.broadcasted_iota(jnp.int32, (1, SLC_KT), 1)
                s = jnp.where(kp_ <= tq, s, NEG)
            s_buf[gi, kt] = s
            mx = m_run[gi]
            for c in range(SLC_KT // LANES):
                mx = jnp.maximum(mx, s[:, c * LANES:(c + 1) * LANES])
            m_run[gi] = mx

    def p1(i, carry):
        scores_tile(2 * i, False)
        scores_tile(2 * i + 1, False)
        return carry

    lax.fori_loop(0, n_full // 2, p1, 0)
    pl.when(n_full % 2 == 1)(lambda: scores_tile(n_full - 1, False))
    scores_tile(n_full, True)
    ms = [jnp.max(m_run[gi], axis=1, keepdims=True) for gi in range(GS)]

    def pv_tile(kt):
        k0 = pl.multiple_of(kt * SLC_KT, SLC_KT)
        for gi in range(GS):
            pe = jnp.exp2(s_buf[gi, kt] - ms[gi]).astype(BF16)
            acc_sc[gi] += jnp.dot(pe, vs_ref[0, gi, pl.ds(k0, SLC_KT), :], preferred_element_type=F32)

    def p2(i, carry):
        pv_tile(2 * i)
        pv_tile(2 * i + 1)
        return carry

    lax.fori_loop(0, (n_full + 1) // 2, p2, 0)
    pl.when(n_full % 2 == 0)(lambda: pv_tile(n_full))

    first_head = lax.broadcasted_iota(jnp.int32, (Q_TILE, LANES), 1) < HEAD_DIM
    for gi in range(GS):
        acc_s = acc_sc[gi]
        o = oc_sc[gi] + (gcol(gi, 1) / acc_s[:, HEAD_DIM:HEAD_DIM + 1]) * acc_s
        for pr in range(Q_PER_KV // 2):
            a = o[(2 * pr) * Q_TILE:(2 * pr + 1) * Q_TILE]
            b = pltpu.roll(o[(2 * pr + 1) * Q_TILE:(2 * pr + 2) * Q_TILE], HEAD_DIM, axis=1)
            c0 = (gi * Q_PER_KV // 2 + pr) * LANES
            o_ref[0, :, c0:c0 + LANES] = jnp.where(first_head, a, b).astype(BF16)


def _attn_prompt(qh, ck, cv, skp, sv, wk, wv, gates):
    B, _, T, _ = qh.shape
    GS = ATTN_GROUPS
    n_cmp_rows = ck.shape[2]
    n_blocks = T // SEL_BLOCK
    n_blk_rows = max(SUBLANES, n_blocks)
    assert n_blocks <= HEAD_DIM and T >= WINDOW + Q_TILE and T % SLC_KT == 0 and KV_HEADS % GS == 0
    n_cmp = T // CMP_STRIDE - CMP_BLOCK // CMP_STRIDE + 1
    at = jnp.asarray(_block_sum_matrix(n_blk_rows, n_cmp_rows, n_blocks, n_cmp))
    q5 = qh.reshape(B, KV_HEADS, Q_PER_KV, T, LANES)
    R = Q_PER_KV * Q_TILE
    full = lambda n: pl.BlockSpec((1, GS, n, LANES), lambda bi, gi, qi: (bi, gi, 0, 0),
                                  pipeline_mode=pl.Buffered(1))
    return pl.pallas_call(
        functools.partial(_attn_prompt_kernel, n_cmp_rows=n_cmp_rows, n_blk_rows=n_blk_rows),
        grid=(B, KV_HEADS // GS, T // Q_TILE),
        in_specs=[
            pl.BlockSpec((1, GS, Q_PER_KV, Q_TILE, LANES), lambda bi, gi, qi: (bi, gi, 0, qi, 0)),
            full(n_cmp_rows), full(n_cmp_rows), full(T), full(T), full(T), full(T),
            pl.BlockSpec((1, Q_TILE, GS * LANES), lambda bi, gi, qi: (bi, qi, gi)),
            pl.BlockSpec((n_blk_rows, n_cmp_rows), lambda bi, gi, qi: (0, 0)),
        ],
        out_specs=pl.BlockSpec((1, Q_TILE, GS * Q_PER_KV * HEAD_DIM), lambda bi, gi, qi: (bi, qi, gi)),
        out_shape=jax.ShapeDtypeStruct((B, T, D_MODEL), BF16),
        scratch_shapes=[pltpu.VMEM((GS, T // SLC_KT, R, SLC_KT), F32), pltpu.VMEM((GS, R, LANES), F32),
                        pltpu.VMEM((GS, R, LANES), F32), pltpu.VMEM((GS, R, LANES), F32),
                        pltpu.VMEM((GS, n_blk_rows, Q_TILE), F32)],
        compiler_params=_cparams(("parallel", "parallel", "arbitrary")),
        name="attn_prompt",
    )(q5, ck, cv, skp, sv, wk, wv, gates, at)


def _oproj_ln_kernel(h_ref, o_ref, w_ref, g_ref, b_ref, out_ref):
    y = jnp.dot(o_ref[0], w_ref[...], preferred_element_type=F32)
    out_ref[0] = _layer_norm(ALPHA * h_ref[0] + y, g_ref[...], b_ref[...])


def _oproj_ln(h, o, w_o, g, b, tT):
    B, T, D = h.shape
    row = pl.BlockSpec((1, tT, D), lambda bi, ti: (bi, ti, 0))
    return pl.pallas_call(
        _oproj_ln_kernel,
        grid=(B, T // tT),
        in_specs=[row, row, _const_spec((D, D)), _const_spec((1, D)), _const_spec((1, D))],
        out_specs=row,
        out_shape=jax.ShapeDtypeStruct((B, T, D), F32),
        compiler_params=_cparams(("parallel", "parallel")),
        name="oproj_ln",
    )(h, o, w_o.astype(BF16), g.reshape(1, D), b.reshape(1, D))


PEN_TILE = PAGES_PER_STEP * (PAGE_SIZE // SEL_BLOCK)
S1_BATCH = 4


def _compress_pg_kernel(ptab_ref, *refs, n_pg):
    pg_refs = refs[:n_pg]
    w4_ref, w1f_ref, pos_ref, b1_ref, w2_ref, b2_ref, ck_ref, cv_ref = refs[n_pg:n_pg + 8]
    xs, carry = refs[n_pg + 8:-1], refs[-1]
    pt = pl.program_id(1)

    @pl.when(pt == 0)
    def _():
        carry[...] = jnp.zeros_like(carry)

    hp = PAGE_SIZE // CMP_STRIDE
    M = n_pg * hp
    first = lax.broadcasted_iota(jnp.int32, (M, CMP_HID), 0) == 0
    for kvi, out_ref in enumerate((ck_ref, cv_ref)):
        b1 = b1_ref[kvi:kvi + 1, :] + jnp.dot(pos_ref[kvi], w1f_ref[kvi], preferred_element_type=F32)[0:1, :]
        for pair in range(KV_HEADS // 2):
            c = kvi * (KV_HEADS // 2) + pair
            for i, r in enumerate(pg_refs):
                xs[c][i * PAGE_SIZE:(i + 1) * PAGE_SIZE, :] = r[0, c * LANES:(c + 1) * LANES, :].T
            acc = None
            for tq in range(CMP_STRIDE // CMP_TGROUP):
                lhs = jnp.concatenate(
                    [xs[c][pl.ds(tq * CMP_TGROUP + k, M, stride=CMP_STRIDE), :].astype(BF16)
                     for k in range(CMP_TGROUP)], axis=1)
                d = jnp.dot(lhs, w4_ref[kvi, tq], preferred_element_type=F32)
                acc = d if acc is None else acc + d
            for gl in range(2):
                g = pair * 2 + gl
                pr0 = acc[:, gl * 2 * CMP_HID:gl * 2 * CMP_HID + CMP_HID]
                pr1 = acc[:, gl * 2 * CMP_HID + CMP_HID:(gl + 1) * 2 * CMP_HID]
                ci = kvi * KV_HEADS + g
                prev = jnp.where(first, carry[ci, 0:1, :], pltpu.roll(pr0, 1, axis=0))
                carry[ci, 0:1, :] = pr0[M - 1:M, :]
                hid = _gelu_tanh(prev + pr1 + b1)
                o = jnp.dot(hid.astype(BF16), w2_ref[kvi], preferred_element_type=F32) + b2_ref[kvi:kvi + 1, :]
                out_ref[0, :, g * HEAD_DIM:(g + 1) * HEAD_DIM] = o.astype(BF16)


def _page_specs(n_pg):
    return [pl.BlockSpec((1, KV_W, PAGE_SIZE),
                         functools.partial(lambda bi, pi, ptab, i: (ptab[bi, pi * n_pg + i], 0, 0), i=i))
            for i in range(n_pg)]


def _compress_pages(pool_t, page_table, cw):
    B, n_pages = page_table.shape
    n_pg = CMP_PAGES_PER_STEP
    assert n_pages % n_pg == 0
    hp = PAGE_SIZE // CMP_STRIDE
    w1cat, w1f, posf, b1, w2, b2 = _compress_weights(cw)
    ntq = CMP_STRIDE // CMP_TGROUP
    w1q = w1cat.reshape(2, ntq, CMP_TGROUP, 1, HEAD_DIM, 1, 2 * CMP_HID)
    eye = jnp.eye(2, dtype=w1cat.dtype).reshape(1, 1, 1, 2, 1, 2, 1)
    w4 = (w1q * eye).reshape(2, ntq, CMP_TGROUP * 2 * HEAD_DIM, 2 * 2 * CMP_HID)
    o_spec = pl.BlockSpec((1, n_pg * hp, KV_W // 2), lambda bi, pi, ptab: (bi, pi, 0))
    o_shape = jax.ShapeDtypeStruct((B, n_pages * hp, KV_W // 2), BF16)
    return pl.pallas_call(
        functools.partial(_compress_pg_kernel, n_pg=n_pg),
        grid_spec=pltpu.PrefetchScalarGridSpec(
            num_scalar_prefetch=1, grid=(B, n_pages // n_pg),
            in_specs=_page_specs(n_pg) + [_const_spec(w.shape) for w in (w4, w1f, posf, b1, w2, b2)],
            out_specs=[o_spec, o_spec],
            scratch_shapes=[pltpu.VMEM((n_pg * PAGE_SIZE, LANES), F32)] * (KV_W // LANES)
            + [pltpu.VMEM((2 * KV_HEADS, SUBLANES, CMP_HID), F32)],
        ),
        out_shape=[o_shape, o_shape],
        compiler_params=_cparams(("parallel", "arbitrary")),
        name="compress_pg",
    )(page_table, *([pool_t] * n_pg), w4, w1f, posf, b1, w2, b2)


def _row_q(shape):
    return lax.broadcasted_iota(jnp.int32, shape, 0) & 7


def _diag_heads(o_all):
    g_row = (lax.broadcasted_iota(jnp.int32, (LANES, HEAD_DIM), 0) // 8) & (KV_HEADS - 1)
    out = jnp.zeros((LANES, HEAD_DIM), F32)
    for g in range(KV_HEADS):
        out = out + jnp.where(g_row == g, o_all[:, g * HEAD_DIM:(g + 1) * HEAD_DIM], 0.0)
    return out


def _attn_s1_kernel(ck_ref, cv_ref, qbr_ref, at_ref, ocmp_ref, pen_ref, score_sc, *, past, n_blocks, n_tiles):
    n_rows = ck_ref.shape[1]
    nb_rows = at_ref.shape[0]
    mrow = lax.broadcasted_iota(jnp.int32, (1, n_rows), 1)
    cend = jnp.where(mrow >= 1, mrow * CMP_STRIDE + (CMP_BLOCK - CMP_STRIDE - 1), jnp.int32(2 ** 30))
    tq = past + _row_q((LANES, 1))
    j = lax.broadcasted_iota(jnp.int32, (nb_rows, LANES), 0)
    lane = lax.broadcasted_iota(jnp.int32, (nb_rows, LANES), 1)
    grp = lane // (LANES // Q_PER_KV)
    tl = past + (lane & 7)
    cur = tl // SEL_BLOCK
    valid = (j * SEL_BLOCK <= tl) & (j < n_blocks)
    forced = valid & ((j == 0) | (j == cur) | (j == cur - 1))
    rows = LANES // Q_PER_KV
    score = None
    for k in range(S1_BATCH):
        s = lax.dot_general(qbr_ref[k], ck_ref[k], NT_DIMS, preferred_element_type=F32)
        s = jnp.where(cend <= tq, s, NEG)
        e = jnp.exp(s - jnp.max(s, axis=1, keepdims=True))
        inv = jnp.where(tq >= CMP_BLOCK - 1, 1.0 / jnp.sum(e, axis=1, keepdims=True), 0.0)
        p = e * inv
        ocmp_ref[k] = _diag_heads(jnp.dot(p.astype(BF16), cv_ref[k], preferred_element_type=F32))
        imp = p[0:rows]
        for r in range(1, Q_PER_KV):
            imp = imp + p[r * rows:(r + 1) * rows]
        imp = jnp.concatenate([imp] * Q_PER_KV, axis=0)
        blk = lax.dot_general(at_ref[...], imp, NT_DIMS, precision=lax.Precision.HIGHEST,
                              preferred_element_type=F32)
        sk = jnp.where(forced, BIG, jnp.where(valid, blk, -1.0))
        score = sk if score is None else jnp.where(grp == k, sk, score)
    score_sc[...] = score

    def body(i, cnt):
        ri = score_sc[pl.ds(i, 1), :]
        beats = (ri > score) | ((ri == score) & (j > i))
        return cnt + jnp.where(beats, 1.0, 0.0)

    cnt = lax.fori_loop(0, n_blocks, body, jnp.zeros((nb_rows, LANES), F32))
    pen4 = jnp.where((cnt < N_SEL) & valid, 0.0, NEG)
    rolled = [pen4] + [pltpu.roll(pen4, sft * rows, axis=1) for sft in range(1, S1_BATCH)]
    n_chunks = -(-n_tiles * PEN_TILE // LANES)
    for k in range(S1_BATCH):
        pen_t = rolled[(0 - k) % S1_BATCH]
        for m in range(1, S1_BATCH):
            pen_t = jnp.where(grp == m, rolled[(m - k) % S1_BATCH], pen_t)
        pen_t = jnp.concatenate([pen_t, jnp.full((n_chunks * LANES - nb_rows, LANES), NEG, F32)], axis=0)
        pen_r = jnp.concatenate([pen_t[c * LANES:(c + 1) * LANES, :].T for c in range(n_chunks)], axis=1)
        for t in range(n_tiles):
            tile = pen_r[:, t * PEN_TILE:(t + 1) * PEN_TILE]
            if PEN_TILE < LANES:
                tile = jnp.concatenate([tile, jnp.zeros((LANES, LANES - PEN_TILE), F32)], axis=1)
            pen_ref[k, t] = tile


def _attn_s2_kernel(ptab_ref, *refs, n_pg):
    pg_refs = refs[:n_pg]
    qbr_ref, pen_ref, e_ref, m_ref, l_ref, acc_ref = refs[n_pg:]
    kw = KV_W // 2
    kt = jnp.concatenate([r[0, 0:kw, :].astype(BF16) for r in pg_refs], axis=1)
    s = jnp.dot(qbr_ref[0], kt, preferred_element_type=F32)
    s = s + jnp.dot(pen_ref[0, 0].astype(BF16), e_ref[...], preferred_element_type=F32)
    m = jnp.max(s, axis=1, keepdims=True)
    pe = jnp.exp(s - m)
    l = jnp.sum(pe, axis=1, keepdims=True)
    vt = jnp.concatenate([r[0, kw:KV_W, :].astype(BF16) for r in pg_refs], axis=1)
    acc_ref[0, 0] = lax.dot_general(pe.astype(BF16), vt, NT_DIMS, preferred_element_type=F32)
    m_ref[0, 0] = jnp.broadcast_to(m, (LANES, LANES))
    l_ref[0, 0] = jnp.broadcast_to(l, (LANES, LANES))


def _attn_s3_kernel(m_ref, l_ref, acc_ref, pen_ref, ocmp_ref, qbr_ref, snew_ref, wst_ref, wnew_ref, gt_ref,
                    o_ref, *, n_pt, n_new):
    qbr = qbr_ref[0]
    kw = KV_W // 2
    nrow = snew_ref.shape[1]
    qr = _row_q((LANES, 1))
    icol = lax.broadcasted_iota(jnp.int32, (1, nrow), 1)
    new_ok = (icol <= qr) & (icol < n_new)

    xs = snew_ref[0]
    st = lax.dot_general(qbr, xs[:, 0:kw].astype(BF16), NT_DIMS, preferred_element_type=F32)
    st = jnp.where(new_ok, st + pen_ref[0, 0][:, 0:1], NEG)
    m_tot = jnp.max(st, axis=1, keepdims=True)
    for s in range(n_pt):
        m_tot = jnp.maximum(m_tot, m_ref[0, s][:, 0:1])
    pt_ = jnp.exp(st - m_tot)
    l_tot = jnp.sum(pt_, axis=1, keepdims=True)
    acc = jnp.dot(pt_.astype(BF16), xs[:, kw:KV_W].astype(BF16), preferred_element_type=F32)
    for s in range(n_pt):
        a = jnp.exp(m_ref[0, s][:, 0:1] - m_tot)
        l_tot = l_tot + a * l_ref[0, s][:, 0:1]
        acc = acc + a * acc_ref[0, s]
    o_slc = _diag_heads(acc / l_tot)

    wn = wnew_ref[0]
    nw = wst_ref.shape[2]
    s1 = jnp.dot(qbr, wst_ref[0, 0:kw, :].astype(BF16), preferred_element_type=F32)
    s2 = lax.dot_general(qbr, wn[:, 0:kw].astype(BF16), NT_DIMS, preferred_element_type=F32)
    dist1 = nw + qr - lax.broadcasted_iota(jnp.int32, (1, nw), 1)
    mask1 = (dist1 >= 0) & (dist1 < WINDOW)
    mask2 = new_ok & (qr - icol < WINDOW)
    s1 = jnp.where(mask1, s1, NEG)
    s2 = jnp.where(mask2, s2, NEG)
    mw = jnp.maximum(jnp.max(s1, axis=1, keepdims=True), jnp.max(s2, axis=1, keepdims=True))
    p1 = jnp.where(mask1, jnp.exp(s1 - mw), 0.0)
    p2 = jnp.where(mask2, jnp.exp(s2 - mw), 0.0)
    inv = 1.0 / (jnp.sum(p1, axis=1, keepdims=True) + jnp.sum(p2, axis=1, keepdims=True))
    ow = (lax.dot_general((p1 * inv).astype(BF16), wst_ref[0, kw:KV_W, :].astype(BF16), NT_DIMS,
                          preferred_element_type=F32)
          + jnp.dot((p2 * inv).astype(BF16), wn[:, kw:KV_W].astype(BF16), preferred_element_type=F32))
    o_win = _diag_heads(ow)

    gv = gt_ref[0]
    o_ref[0] = gv[:, 0:1] * ocmp_ref[0] + gv[:, 1:2] * o_slc + gv[:, 2:3] * o_win


def _attn_sample(q_flat, gates, ck, cv, slc_pool_t, page_table, slc_new, win_state_t, win_new, past):
    B, Tn, _ = q_flat.shape
    n_pg = PAGES_PER_STEP
    assert Tn == 8 and past % (PAGE_SIZE * n_pg) == 0
    n_pages = past // PAGE_SIZE
    n_pt = n_pages // n_pg
    n_past_blk = past // SEL_BLOCK
    n_blocks = n_past_blk + 1
    nb_rows = -(-n_blocks // SUBLANES) * SUBLANES
    n_cmp = past // CMP_STRIDE - 1
    Mc = ck.shape[1]
    at = jnp.asarray(_block_sum_matrix(nb_rows, Mc, n_blocks, n_cmp))
    kw = KV_W // 2

    q5 = q_flat.reshape(B, Tn, KV_HEADS, Q_PER_KV, HEAD_DIM)
    base = jnp.transpose(q5, (0, 3, 2, 1, 4))
    eye = jnp.eye(KV_HEADS, dtype=q_flat.dtype)
    qbr = (base[:, :, :, :, None, :] * eye[None, None, :, None, :, None]).reshape(B, LANES, kw)
    g5 = gates[:, :, :N_HEADS * 3].reshape(B, Tn, KV_HEADS, Q_PER_KV, 3)
    gt = jnp.pad(jnp.transpose(g5, (0, 3, 2, 1, 4)).reshape(B, LANES, 3), ((0, 0), (0, 0), (0, LANES - 3)))

    per_b = lambda *shape: pl.BlockSpec((1,) + shape, lambda bi, *_: (bi,) + (0,) * len(shape))
    assert B % S1_BATCH == 0 and S1_BATCH == Q_PER_KV
    per_sb = lambda *shape: pl.BlockSpec((S1_BATCH,) + shape, lambda bi: (bi,) + (0,) * len(shape))
    ocmp, pen = pl.pallas_call(
        functools.partial(_attn_s1_kernel, past=past, n_blocks=n_blocks, n_tiles=n_pt + 1),
        grid=(B // S1_BATCH,),
        in_specs=[per_sb(Mc, kw), per_sb(Mc, kw), per_sb(LANES, kw),
                  pl.BlockSpec((nb_rows, Mc), lambda bi: (0, 0))],
        out_specs=[per_sb(LANES, HEAD_DIM), per_sb(n_pt + 1, LANES, LANES)],
        out_shape=[jax.ShapeDtypeStruct((B, LANES, HEAD_DIM), F32),
                   jax.ShapeDtypeStruct((B, n_pt + 1, LANES, LANES), F32)],
        scratch_shapes=[pltpu.VMEM((nb_rows, LANES), F32)],
        compiler_params=_cparams(("parallel",)),
        name="attn_s1",
    )(ck, cv, qbr, at)

    expand = np.zeros((LANES, n_pg * PAGE_SIZE), np.float32)
    for jb in range(PEN_TILE):
        expand[jb, jb * SEL_BLOCK:(jb + 1) * SEL_BLOCK] = 1.0
    part = lambda *shape: pl.BlockSpec((1, 1) + shape, lambda bi, pi, ptab: (bi, pi) + (0,) * len(shape))
    m_p, l_p, acc_p = pl.pallas_call(
        functools.partial(_attn_s2_kernel, n_pg=n_pg),
        grid_spec=pltpu.PrefetchScalarGridSpec(
            num_scalar_prefetch=1, grid=(B, n_pt),
            in_specs=_page_specs(n_pg) + [pl.BlockSpec((1, LANES, kw), lambda bi, pi, ptab: (bi, 0, 0)),
                                          part(LANES, LANES),
                                          pl.BlockSpec(expand.shape, lambda bi, pi, ptab: (0, 0))],
            out_specs=[part(LANES, LANES), part(LANES, LANES), part(LANES, kw)],
        ),
        out_shape=[jax.ShapeDtypeStruct((B, n_pt, LANES, LANES), F32)] * 2
        + [jax.ShapeDtypeStruct((B, n_pt, LANES, kw), F32)],
        compiler_params=_cparams(("parallel", "parallel")),
        name="attn_s2",
    )(page_table, *([slc_pool_t] * n_pg), qbr, pen, jnp.asarray(expand, BF16))

    pad_rows = 16 - Tn
    snew = jnp.pad(slc_new, ((0, 0), (0, pad_rows), (0, 0)))
    wnew = jnp.pad(win_new, ((0, 0), (0, pad_rows), (0, 0)))
    nw = win_state_t.shape[2]
    o = pl.pallas_call(
        functools.partial(_attn_s3_kernel, n_pt=n_pt, n_new=Tn),
        grid=(B,),
        in_specs=[per_b(n_pt, LANES, LANES), per_b(n_pt, LANES, LANES), per_b(n_pt, LANES, kw),
                  pl.BlockSpec((1, 1, LANES, LANES), lambda bi: (bi, n_pt, 0, 0)),
                  per_b(LANES, HEAD_DIM), per_b(LANES, kw),
                  per_b(16, KV_W), per_b(KV_W, nw), per_b(16, KV_W), per_b(LANES, LANES)],
        out_specs=per_b(LANES, HEAD_DIM),
        out_shape=jax.ShapeDtypeStruct((B, LANES, HEAD_DIM), F32),
        compiler_params=_cparams(("parallel",)),
        name="attn_s3",
    )(m_p, l_p, acc_p, pen, ocmp, qbr, snew, win_state_t, wnew, gt)
    o = jnp.transpose(o.reshape(B, Q_PER_KV, KV_HEADS, Tn, HEAD_DIM), (0, 3, 2, 1, 4))
    return o.reshape(B, Tn, D_MODEL).astype(BF16)


def _nsa_prompt_layer(h, w_in, w_o, cw, g, b, tT):
    B, T, _ = h.shape
    cmp_kv, slc_kv, win_kv, qh, skp, sv, wk, wv, gates = _nsa_proj(h, jnp.arange(T), w_in, tT, True)
    ck, cv = _compress_prompt(cmp_kv, cw)
    o = _attn_prompt(qh, ck, cv, skp, sv, wk, wv, gates)
    h2 = _oproj_ln(h, o, w_o, g, b, tT)
    win_len = min(WINDOW, T)
    return h2, cmp_kv, slc_kv, win_kv[:, T - win_len:]


def _pages_device_layout(pool):
    return jnp.transpose(pool, (0, 2, 3, 4, 1)).reshape(pool.shape[0], KV_W, pool.shape[1])


def _nsa_sample_layer(h, cmp_pool, slc_pool, win_buf, page_table, w_in, w_o, cw, g, b):
    B, Tn, _ = h.shape
    n_pages = page_table.shape[1]
    past = n_pages * PAGE_SIZE
    cmp_kv, slc_kv, win_kv, q_flat, gates = _nsa_proj(h, past + jnp.arange(Tn), w_in, Tn, False)
    assert Tn < CMP_STRIDE
    ck, cv = _compress_pages(_pages_device_layout(cmp_pool), page_table, cw)
    o = _attn_sample(q_flat, gates, ck, cv, _pages_device_layout(slc_pool), page_table, slc_kv,
                     _pages_device_layout(win_buf), win_kv, past)
    h2 = _oproj_ln(h, o, w_o, g, b, Tn)
    buf_len = win_buf.shape[1]
    new_win = jnp.concatenate([win_buf, win_kv.reshape(B, Tn, 2, KV_HEADS, HEAD_DIM)], axis=1)[:, -buf_len:]
    return h2, cmp_kv, slc_kv, new_win.reshape(B, buf_len, KV_W)


def _row_tile(T):
    return 512 if T % 512 == 0 else T


def kernel(x_prompt, x_sample, state_pool, cache_cmp_kv, cache_slc_kv, state_win_kv, state_ffn, page_table, ln_g, ln_b, pool_w, pool_scale, nsa_w_in, nsa_w_o, cmp_w1, cmp_b1, cmp_pos, cmp_w2, cmp_b2, ffn_w_up, ffn_conv_w, ffn_conv_b, ffn_w_down):
    Bp, T, D = x_prompt.shape
    Bs, Tn, _ = x_sample.shape
    past = page_table.shape[1] * PAGE_SIZE
    kv_shape = (2, KV_HEADS, HEAD_DIM)
    tp, ts = _row_tile(T), Tn

    hp = _pool_ln(x_prompt, jnp.zeros((Bp, POOL_BUF, D), F32), 0, pool_w[0], pool_scale[0], ln_g[0, 0], ln_b[0, 0], tp)
    hs = _pool_ln(x_sample, state_pool[0], past, pool_w[0], pool_scale[0], ln_g[0, 0], ln_b[0, 0], ts)
    pool_p = jnp.concatenate([jnp.zeros((Bp, POOL_BUF, D), F32), x_prompt], axis=1)[:, -POOL_BUF:][None]
    pool_s = jnp.concatenate([state_pool[0], x_sample], axis=1)[:, -POOL_BUF:][None]
    ffn = lambda h, pre, i, t: _ffn_ln(h, pre, ffn_w_up[i], ffn_conv_w[i], ffn_conv_b[i], ffn_w_down[i],
                                       ln_g[i, 1], ln_b[i, 1], t)
    zero_pre = jnp.zeros((Bp, CONV_W - 1, 2 * D_FF), F32)
    hp, ffn_p0 = ffn(hp, zero_pre, 0, tp)
    ffn_s = lambda h, pre, i: _ffn_ln_short(h, pre, ffn_w_up[i], ffn_conv_w[i], ffn_conv_b[i], ffn_w_down[i],
                                            ln_g[i, 1], ln_b[i, 1])
    hs, ffn_s0 = ffn_s(hs, state_ffn[0], 0)

    cw = (cmp_w1[0], cmp_b1[0], cmp_pos[0], cmp_w2[0], cmp_b2[0])
    hp, c_p, s_p, w_p = _nsa_prompt_layer(hp, nsa_w_in[0], nsa_w_o[0], cw, ln_g[1, 0], ln_b[1, 0], tp)
    hs, c_s, s_s, w_s = _nsa_sample_layer(hs, cache_cmp_kv[0], cache_slc_kv[0], state_win_kv[0], page_table,
                                          nsa_w_in[0], nsa_w_o[0], cw, ln_g[1, 0], ln_b[1, 0])
    hp, ffn_p1 = ffn(hp, zero_pre, 1, tp)
    hs, ffn_s1 = ffn_s(hs, state_ffn[1], 1)

    kv5 = lambda a: a.reshape(a.shape[0], a.shape[1], *kv_shape)[None]
    return (hp, hs, pool_p, pool_s, kv5(c_p), kv5(c_s), kv5(s_p), kv5(s_s), kv5(w_p), kv5(w_s),
            jnp.stack([ffn_p0, ffn_p1]), jnp.stack([ffn_s0, ffn_s1]))
```

```python
import functools

import numpy as np
import jax
import jax.numpy as jnp
from jax import lax
from jax.experimental import pallas as pl
from jax.experimental.pallas import tpu as pltpu

D_MODEL = 1024
DEPTH = 2
ALPHA = (2.0 * DEPTH) ** 0.25
LN_EPS = 1e-5
POOL_WINDOWS = (2, 4, 8, 16)
POOL_CH = D_MODEL // len(POOL_WINDOWS)
POOL_BUF = max(POOL_WINDOWS) - 1
N_HEADS = 16
KV_HEADS = 4
HEAD_DIM = 64
Q_PER_KV = N_HEADS // KV_HEADS
ROT_DIM = HEAD_DIM // 4
ROPE_THETA = 500000.0
CMP_BLOCK = 32
CMP_STRIDE = 16
CMP_HID = 128
SEL_BLOCK = 64
N_SEL = 16
WINDOW = 512
KV_W = 2 * KV_HEADS * HEAD_DIM
D_FF = 2816
CONV_W = 3
PAGE_SIZE = 128
NEG = -1e30
BIG = 1e9
LOG2E = 1.4426950408889634

LANES = 128
SUBLANES = 8
Q_TILE = 128
SLC_KT = 512
ATTN_GROUPS = 4
PAGES_PER_STEP = 64
CMP_PAGES_PER_STEP = 64
CMP_TGROUP = 4
VMEM_LIMIT = 60 * 1024 * 1024

F32 = jnp.float32
BF16 = jnp.bfloat16
NT_DIMS = (((1,), (1,)), ((), ()))


def _cparams(sem):
    return pltpu.CompilerParams(dimension_semantics=sem, vmem_limit_bytes=VMEM_LIMIT)


def _layer_norm(h, g, b):
    mu = jnp.mean(h, axis=-1, keepdims=True)
    hc = h - mu
    var = jnp.mean(hc * hc, axis=-1, keepdims=True)
    return hc * lax.rsqrt(var + LN_EPS) * g + b


def _const_spec(shape):
    nd = len(shape)
    return pl.BlockSpec(shape, lambda *_: (0,) * nd, pipeline_mode=pl.Buffered(1))


def _pool_ln_kernel(x_ref, halo_ref, pre_ref, w_ref, sc_ref, g_ref, b_ref, o_ref, buf, *, tT, pos0):
    t = pl.program_id(1)
    x = x_ref[0]
    buf[0:16, :] = jnp.where(t == 0, pre_ref[0], halo_ref[0])
    buf[16:16 + tT, :] = x
    pos = pos0 + t * tT + lax.broadcasted_iota(jnp.int32, (tT, 1), 0)
    ys = []
    for gi, win in enumerate(POOL_WINDOWS):
        c0 = gi * POOL_CH
        xg = x[:, c0:c0 + POOL_CH]
        s = xg
        for i in range(1, win):
            s = s + buf[pl.ds(16 - i, tT), c0:c0 + POOL_CH]
        cnt = jnp.minimum(pos + 1, win).astype(F32)
        d = s / cnt - xg
        ys.append(jnp.dot(d.astype(BF16), w_ref[gi], preferred_element_type=F32))
    y = jnp.concatenate(ys, axis=1) * sc_ref[...]
    o_ref[0] = _layer_norm(ALPHA * x + y, g_ref[...], b_ref[...])


def _pool_ln(x, prefix, pos0, w, scale, g, b, tT):
    B, T, D = x.shape
    nT = T // tT
    pre16 = jnp.concatenate([jnp.zeros((B, 1, D), x.dtype), prefix], axis=1)
    if T >= 16:
        halo_src = x
        per = tT // 16
        halo_spec = pl.BlockSpec((1, 16, D), lambda bi, ti: (bi, jnp.maximum(ti * per - 1, 0), 0))
    else:
        halo_src = pre16
        halo_spec = pl.BlockSpec((1, 16, D), lambda bi, ti: (bi, 0, 0))
    return pl.pallas_call(
        functools.partial(_pool_ln_kernel, tT=tT, pos0=pos0),
        grid=(B, nT),
        in_specs=[
            pl.BlockSpec((1, tT, D), lambda bi, ti: (bi, ti, 0)),
            halo_spec,
            pl.BlockSpec((1, 16, D), lambda bi, ti: (bi, 0, 0)),
            _const_spec((len(POOL_WINDOWS), POOL_CH, POOL_CH)),
            _const_spec((1, D)), _const_spec((1, D)), _const_spec((1, D)),
        ],
        out_specs=pl.BlockSpec((1, tT, D), lambda bi, ti: (bi, ti, 0)),
        out_shape=jax.ShapeDtypeStruct((B, T, D), F32),
        scratch_shapes=[pltpu.VMEM((16 + tT, D), F32)],
        compiler_params=_cparams(("parallel", "parallel")),
        name="pool_ln",
    )(x, halo_src, pre16, w.astype(BF16), scale.reshape(1, D), g.reshape(1, D), b.reshape(1, D))


FFN_CHUNK = 256


def _ffn_ln_kernel(h_ref, pre_ref, wup_ref, cw_ref, cb_ref, wdn_ref, g_ref, b_ref, o_ref, st_ref,
                   ubuf, act, *, tT):
    t = pl.program_id(1)

    @pl.when(t == 0)
    def _():
        st_ref[0] = pre_ref[0]

    h = h_ref[0]
    hb = h.astype(BF16)
    fc = FFN_CHUNK
    for j in range(D_FF // fc):
        cs = []
        for half in range(2):
            c0 = half * D_FF + j * fc
            up = jnp.dot(hb, wup_ref[:, c0:c0 + fc], preferred_element_type=F32)
            ubuf[half, 0:8, :] = st_ref[0, :, c0:c0 + fc]
            ubuf[half, 8:8 + tT, :] = up
            st_ref[0, :, c0:c0 + fc] = ubuf[half, tT:tT + 8, :]
            c = (cb_ref[:, c0:c0 + fc]
                 + cw_ref[0:1, c0:c0 + fc] * ubuf[half, pl.ds(6, tT), :]
                 + cw_ref[1:2, c0:c0 + fc] * ubuf[half, pl.ds(7, tT), :]
                 + cw_ref[2:3, c0:c0 + fc] * up)
            cs.append(c)
        a = cs[0] * jax.nn.sigmoid(cs[0]) * cs[1]
        act[:, j * fc:(j + 1) * fc] = a.astype(BF16)
    y = jnp.dot(act[...], wdn_ref[...], preferred_element_type=F32)
    o_ref[0] = _layer_norm(ALPHA * h + y, g_ref[...], b_ref[...])


def _ffn_ln(h, prefix, w_up, conv_w, conv_b, w_down, g, b, tT):
    B, T, D = h.shape
    nT = T // tT
    F2 = 2 * D_FF
    pre8 = jnp.concatenate([jnp.zeros((B, 8 - (CONV_W - 1), F2), F32), prefix], axis=1)
    out, st = pl.pallas_call(
        functools.partial(_ffn_ln_kernel, tT=tT),
        grid=(B, nT),
        in_specs=[
            pl.BlockSpec((1, tT, D), lambda bi, ti: (bi, ti, 0)),
            pl.BlockSpec((1, 8, F2), lambda bi, ti: (bi, 0, 0)),
            _const_spec((D, F2)),
            _const_spec((CONV_W, F2)),
            _const_spec((1, F2)),
            _const_spec((D_FF, D)),
            _const_spec((1, D)), _const_spec((1, D)),
        ],
        out_specs=[pl.BlockSpec((1, tT, D), lambda bi, ti: (bi, ti, 0)),
                   pl.BlockSpec((1, 8, F2), lambda bi, ti: (bi, 0, 0))],
        out_shape=[jax.ShapeDtypeStruct((B, T, D), F32), jax.ShapeDtypeStruct((B, 8, F2), F32)],
        scratch_shapes=[pltpu.VMEM((2, tT + 8, FFN_CHUNK), F32), pltpu.VMEM((tT, D_FF), BF16)],
        compiler_params=_cparams(("parallel", "arbitrary")),
        name="ffn_ln",
    )(h, pre8, w_up.astype(BF16), conv_w, conv_b.reshape(1, F2), w_down.astype(BF16),
      g.reshape(1, D), b.reshape(1, D))
    return out, st[:, 8 - (CONV_W - 1):, :]


def _ffn_ln_short_kernel(h_ref, pre_ref, wup_ref, cw_ref, cb_ref, wdn_ref, g_ref, b_ref, o_ref, st_ref, act,
                         *, seq):
    h = h_ref[...]
    hb = h.astype(BF16)
    rows = h.shape[0]
    fc = FFN_CHUNK
    t = lax.broadcasted_iota(jnp.int32, (rows, fc), 0) & (seq - 1)
    for j in range(D_FF // fc):
        cs = []
        for half in range(2):
            c0 = half * D_FF + j * fc
            up = jnp.dot(hb, wup_ref[:, c0:c0 + fc], preferred_element_type=F32)
            st_ref[:, c0:c0 + fc] = up
            pre = pre_ref[:, c0:c0 + fc]
            s1 = jnp.where(t >= 1, pltpu.roll(up, 1, axis=0), pltpu.roll(pre, 1, axis=0))
            s2 = jnp.where(t >= 2, pltpu.roll(up, 2, axis=0), pltpu.roll(pre, 2, axis=0))
            cs.append(cb_ref[:, c0:c0 + fc] + cw_ref[0:1, c0:c0 + fc] * s2 + cw_ref[1:2, c0:c0 + fc] * s1
                      + cw_ref[2:3, c0:c0 + fc] * up)
        a = cs[0] * jax.nn.sigmoid(cs[0]) * cs[1]
        act[:, j * fc:(j + 1) * fc] = a.astype(BF16)
    y = jnp.dot(act[...], wdn_ref[...], preferred_element_type=F32)
    o_ref[...] = _layer_norm(ALPHA * h + y, g_ref[...], b_ref[...])


def _ffn_ln_short(h, prefix, w_up, conv_w, conv_b, w_down, g, b):
    B, T, D = h.shape
    assert T == SUBLANES and CONV_W - 1 <= T
    F2 = 2 * D_FF
    rows = B * T
    pre8 = jnp.concatenate([jnp.zeros((B, T - (CONV_W - 1), F2), F32), prefix], axis=1)
    pre_rows = jnp.roll(pre8, -1, axis=0).reshape(rows, F2)
    full = lambda *shape: pl.BlockSpec(shape, lambda i: (0,) * len(shape))
    out, st = pl.pallas_call(
        functools.partial(_ffn_ln_short_kernel, seq=T),
        grid=(1,),
        in_specs=[_const_spec(sh) for sh in ((rows, D), (rows, F2), (D, F2), (CONV_W, F2), (1, F2), (D_FF, D),
                                             (1, D), (1, D))],
        out_specs=[full(rows, D), full(rows, F2)],
        out_shape=[jax.ShapeDtypeStruct((rows, D), F32), jax.ShapeDtypeStruct((rows, F2), F32)],
        scratch_shapes=[pltpu.VMEM((rows, D_FF), BF16)],
        compiler_params=_cparams(("arbitrary",)),
        name="ffn_ln_short",
    )(h.reshape(rows, D), pre_rows, w_up.astype(BF16), conv_w, conv_b.reshape(1, F2), w_down.astype(BF16),
      g.reshape(1, D), b.reshape(1, D))
    return out.reshape(B, T, D), st.reshape(B, T, F2)[:, T - (CONV_W - 1):, :]


def _rope_tables(pos):
    half = ROT_DIM // 2
    inv = jnp.power(ROPE_THETA, -2.0 * jnp.arange(half, dtype=F32) / ROT_DIM)
    ang = pos.astype(F32)[:, None] * inv[None, :]
    cos, sin = jnp.cos(ang), jnp.sin(ang)
    T = pos.shape[0]
    one, zero = jnp.ones((T, HEAD_DIM - ROT_DIM), F32), jnp.zeros((T, HEAD_DIM - half), F32)
    c64 = jnp.concatenate([cos, cos, one], axis=1)
    lo64 = jnp.concatenate([-sin, zero], axis=1)
    hi64 = jnp.concatenate([jnp.zeros((T, half), F32), sin, jnp.zeros((T, HEAD_DIM - ROT_DIM), F32)], axis=1)
    tile = lambda a: jnp.concatenate([a, a], axis=1)
    return tile(c64), tile(lo64), tile(hi64)


def _rope_chunk(x, c, lo, hi):
    return x * c + pltpu.roll(x, LANES - ROT_DIM // 2, axis=1) * lo + pltpu.roll(x, ROT_DIM // 2, axis=1) * hi


def _nsa_proj_kernel(h_ref, wq_ref, wkv_ref, wg_ref, c_ref, lo_ref, hi_ref, *outs, tT, head_major):
    t = pl.program_id(1)
    hb = h_ref[0].astype(BF16)
    c, lo, hi = c_ref[...], lo_ref[...], hi_ref[...]
    q = jnp.dot(hb, wq_ref[...], preferred_element_type=F32)
    kv = jnp.dot(hb, wkv_ref[...], preferred_element_type=F32)
    gl = jnp.dot(hb, wg_ref[...], preferred_element_type=F32)
    gates = jax.nn.sigmoid(gl)
    lane = lax.broadcasted_iota(jnp.int32, (tT, LANES), 1)
    low = lane < HEAD_DIM
    scale = HEAD_DIM ** -0.5 * (LOG2E if head_major else 1.0)
    qc = [_rope_chunk(q[:, i * LANES:(i + 1) * LANES], c, lo, hi) * scale for i in range(D_MODEL // LANES)]
    kvc = []
    for br in range(3):
        for i in range(KV_W // LANES):
            x = kv[:, br * KV_W + i * LANES: br * KV_W + (i + 1) * LANES]
            kvc.append(_rope_chunk(x, c, lo, hi) if i < KV_W // (2 * LANES) else x)
    if head_major:
        cmp_ref, slc_ref, win_ref, qh_ref, skp_ref, sv_ref, wk_ref, wv_ref, g_ref = outs
    else:
        cmp_ref, slc_ref, win_ref, qf_ref, g_ref = outs
    for br, ref in enumerate((cmp_ref, slc_ref, win_ref)):
        ref[0] = jnp.concatenate(kvc[br * 4:(br + 1) * 4], axis=1)
    g_ref[0] = gates
    if not head_major:
        qf_ref[0] = jnp.concatenate(qc, axis=1).astype(BF16)
        return

    def split(x):
        return jnp.where(low, x, 0.0), jnp.where(low, pltpu.roll(x, HEAD_DIM, axis=1), 0.0)

    for i in range(D_MODEL // LANES):
        a, b = split(qc[i])
        qh_ref[0, 2 * i] = a.astype(BF16)
        qh_ref[0, 2 * i + 1] = b.astype(BF16)
    blk = (t * tT + lax.broadcasted_iota(jnp.int32, (tT, LANES), 0)) // SEL_BLOCK
    onehot = jnp.where(lane - HEAD_DIM == blk, 1.0, 0.0)
    ones_col = jnp.where(lane == HEAD_DIM, 1.0, 0.0)
    for br, (kref, vref) in ((1, (skp_ref, sv_ref)), (2, (wk_ref, wv_ref))):
        for i in range(2):
            ka, kb = split(kvc[br * 4 + i])
            va, vb = split(kvc[br * 4 + 2 + i])
            if br == 1:
                ka, kb = ka + onehot, kb + onehot
            va, vb = va + ones_col, vb + ones_col
            kref[0, 2 * i] = ka.astype(BF16)
            kref[0, 2 * i + 1] = kb.astype(BF16)
            vref[0, 2 * i] = va.astype(BF16)
            vref[0, 2 * i + 1] = vb.astype(BF16)


def _nsa_proj(h, pos, w_in, tT, head_major):
    B, T, D = h.shape
    nT = T // tT
    nq = N_HEADS * HEAD_DIM
    wq = w_in[:, :nq].astype(BF16)
    wkv = w_in[:, nq:nq + 3 * KV_W].astype(BF16)
    wg = w_in[:, nq + 3 * KV_W:]
    if head_major:
        GW = KV_HEADS * LANES
        wg = jnp.pad(wg.reshape(D, KV_HEADS, Q_PER_KV * 3), ((0, 0), (0, 0), (0, LANES - Q_PER_KV * 3))).reshape(D, GW)
    else:
        GW = LANES
        wg = jnp.pad(wg, ((0, 0), (0, LANES - wg.shape[1])))
    wg = wg.astype(BF16)
    c, lo, hi = _rope_tables(pos)
    row = lambda w: pl.BlockSpec((1, tT, w), lambda bi, ti: (bi, ti, 0))
    hm = lambda n: pl.BlockSpec((1, n, tT, LANES), lambda bi, ti: (bi, 0, ti, 0))
    tab = pl.BlockSpec((tT, LANES), lambda bi, ti: (ti, 0))
    out_specs = [row(KV_W), row(KV_W), row(KV_W)]
    out_shape = [jax.ShapeDtypeStruct((B, T, KV_W), F32)] * 3
    if head_major:
        out_specs += [hm(N_HEADS)] + [hm(KV_HEADS)] * 4 + [row(GW)]
        out_shape += ([jax.ShapeDtypeStruct((B, N_HEADS, T, LANES), BF16)]
                      + [jax.ShapeDtypeStruct((B, KV_HEADS, T, LANES), BF16)] * 4
                      + [jax.ShapeDtypeStruct((B, T, GW), F32)])
    else:
        out_specs += [row(D), row(GW)]
        out_shape += [jax.ShapeDtypeStruct((B, T, D), BF16), jax.ShapeDtypeStruct((B, T, GW), F32)]
    return pl.pallas_call(
        functools.partial(_nsa_proj_kernel, tT=tT, head_major=head_major),
        grid=(B, nT),
        in_specs=[row(D), _const_spec((D, nq)), _const_spec((D, 3 * KV_W)), _const_spec((D, GW)), tab, tab, tab],
        out_specs=out_specs,
        out_shape=out_shape,
        compiler_params=_cparams(("parallel", "parallel")),
        name="nsa_proj",
    )(h, wq, wkv, wg, c, lo, hi)


def _gelu_tanh(x):
    return 0.5 * x * (1.0 + jnp.tanh(0.7978845608028654 * (x + 0.044715 * x * x * x)))


def _compress_kernel(x_ref, w1_ref, w1f_ref, pos_ref, b1_ref, w2_ref, b2_ref, ck_ref, cv_ref):
    for kvi, out_ref in enumerate((ck_ref, cv_ref)):
        b1 = b1_ref[kvi:kvi + 1, :] + jnp.dot(pos_ref[kvi], w1f_ref[kvi], preferred_element_type=F32)[0:1, :]
        for g in range(KV_HEADS):
            acc = None
            for t in range(CMP_STRIDE):
                off = t * KV_W + kvi * (KV_W // 2) + g * HEAD_DIM
                d = jnp.dot(x_ref[0, :, off:off + HEAD_DIM].astype(BF16), w1_ref[kvi, t],
                            preferred_element_type=F32)
                acc = d if acc is None else acc + d
            pr0, pr1 = acc[:, :CMP_HID], acc[:, CMP_HID:]
            hid = _gelu_tanh(pltpu.roll(pr0, 1, axis=0) + pr1 + b1)
            o = jnp.dot(hid.astype(BF16), w2_ref[kvi], preferred_element_type=F32) + b2_ref[kvi:kvi + 1, :]
            out_ref[0, g] = jnp.concatenate([o, jnp.zeros_like(o)], axis=1).astype(BF16)


def _compress_weights(cw):
    w1, b1, pos_emb, w2, b2 = cw
    assert CMP_BLOCK == 2 * CMP_STRIDE
    w1cat = jnp.concatenate([w1[:, :CMP_STRIDE], w1[:, CMP_STRIDE:]], axis=-1).astype(BF16)
    w1f = w1.reshape(2, CMP_BLOCK * HEAD_DIM, CMP_HID).astype(BF16)
    posf = jnp.broadcast_to(pos_emb.reshape(2, 1, CMP_BLOCK * HEAD_DIM), (2, SUBLANES, CMP_BLOCK * HEAD_DIM)).astype(BF16)
    return w1cat, w1f, posf, b1, w2.astype(BF16), b2


def _compress_prompt(cmp_kv, cw):
    B, T, _ = cmp_kv.shape
    n_half = T // CMP_STRIDE
    x = cmp_kv.reshape(B, n_half, CMP_STRIDE * KV_W)
    ws = _compress_weights(cw)
    o_spec = pl.BlockSpec((1, KV_HEADS, n_half, LANES), lambda bi: (bi, 0, 0, 0))
    o_shape = jax.ShapeDtypeStruct((B, KV_HEADS, n_half, LANES), BF16)
    return pl.pallas_call(
        _compress_kernel,
        grid=(B,),
        in_specs=[pl.BlockSpec((1, n_half, CMP_STRIDE * KV_W), lambda bi: (bi, 0, 0))]
        + [_const_spec(w.shape) for w in ws],
        out_specs=[o_spec, o_spec],
        out_shape=[o_shape, o_shape],
        compiler_params=_cparams(("parallel",)),
        name="compress_hm",
    )(x, *ws)


def _block_sum_matrix(n_out, n_in, n_blocks, n_cmp):
    a = np.zeros((n_out, n_in), np.float32)
    for j in range(n_blocks):
        for m in range(4 * j, 4 * j + 5):
            if 1 <= m <= n_cmp:
                a[j, m] = 1.0
    return a


def _attn_prompt_kernel(q_ref, ck_ref, cv_ref, kp_ref, vs_ref, wk_ref, wv_ref, gt_ref, at_ref, o_ref,
                        s_buf, m_run, acc_sc, oc_sc, pen_sc, *, n_cmp_rows, n_blk_rows):
    qt = pl.program_id(2)
    s0 = qt * Q_TILE
    R = Q_PER_KV * Q_TILE
    GS = ATTN_GROUPS
    Qs = [q_ref[0, gi].reshape(R, LANES) for gi in range(GS)]
    tq = s0 + (lax.broadcasted_iota(jnp.int32, (R, 1), 0) & (Q_TILE - 1))

    mrow = lax.broadcasted_iota(jnp.int32, (1, n_cmp_rows), 1)
    cend = jnp.where(mrow >= 1, mrow * CMP_STRIDE + (CMP_BLOCK - CMP_STRIDE - 1), jnp.int32(2 ** 30))
    j = lax.broadcasted_iota(jnp.int32, (n_blk_rows, Q_TILE), 0)
    tl = s0 + lax.broadcasted_iota(jnp.int32, (n_blk_rows, Q_TILE), 1)
    cur = tl // SEL_BLOCK
    valid = j * SEL_BLOCK <= tl
    forced = valid & ((j == 0) | (j == cur) | (j == cur - 1))
    scores = []
    for gi in range(GS):
        sc = lax.dot_general(Qs[gi], ck_ref[0, gi], NT_DIMS, preferred_element_type=F32)
        sc = jnp.where(cend <= tq, sc, NEG)
        e = jnp.exp2(sc - jnp.max(sc, axis=1, keepdims=True))
        inv = jnp.where(tq >= CMP_BLOCK - 1, 1.0 / jnp.sum(e, axis=1, keepdims=True), 0.0)
        p = e * inv
        oc_sc[gi] = jnp.dot(p.astype(BF16), cv_ref[0, gi], preferred_element_type=F32)
        imp = p[0:Q_TILE]
        for r in range(1, Q_PER_KV):
            imp = imp + p[r * Q_TILE:(r + 1) * Q_TILE]
        blk = lax.dot_general(at_ref[...], imp, NT_DIMS, precision=lax.Precision.HIGHEST,
                              preferred_element_type=F32)
        scores.append(jnp.where(forced, BIG, jnp.where(valid, blk, -1.0)))

    wlen = WINDOW + Q_TILE
    n_sub = wlen // Q_TILE
    w0 = pl.multiple_of(jnp.maximum(s0 - WINDOW, 0), Q_TILE)
    tri = (lax.broadcasted_iota(jnp.int32, (Q_TILE, Q_TILE), 1)
           <= lax.broadcasted_iota(jnp.int32, (Q_TILE, Q_TILE), 0))
    steady = s0 >= WINDOW
    wbias = []
    for i in range(n_sub):
        lower_ok = jnp.where(steady, i > 0, i <= qt)
        upper_ok = jnp.where(steady, i < n_sub - 1, i < qt)
        b = jnp.where(tri, jnp.where(lower_ok, 0.0, NEG), jnp.where(upper_ok, 0.0, NEG))
        wbias.append(jnp.concatenate([b] * Q_PER_KV, axis=0))
    wbias = jnp.concatenate(wbias, axis=1)
    gv = gt_ref[0]
    gcol = lambda gi, jb: jnp.concatenate(
        [gv[:, gi * LANES + 3 * r + jb:gi * LANES + 3 * r + jb + 1] for r in range(Q_PER_KV)], axis=0)
    for gi in range(GS):
        s = lax.dot_general(Qs[gi], wk_ref[0, gi, pl.ds(w0, wlen), :], NT_DIMS, preferred_element_type=F32)
        s = s + wbias
        pw = jnp.exp2(s - jnp.max(s, axis=1, keepdims=True)).astype(BF16)
        acc_w = jnp.dot(pw, wv_ref[0, gi, pl.ds(w0, wlen), :], preferred_element_type=F32)
        oc_sc[gi] = gcol(gi, 0) * oc_sc[gi] + (gcol(gi, 2) / acc_w[:, HEAD_DIM:HEAD_DIM + 1]) * acc_w

    jv = lax.broadcasted_iota(jnp.int32, (SUBLANES, Q_TILE), 0)

    def rank_rows(n_rows):
        nv = n_rows // SUBLANES
        for gi in range(GS):
            if n_rows <= N_SEL:
                pen_sc[gi] = jnp.where(valid, 0.0, NEG)
                continue
            sv = [scores[gi][v * SUBLANES:(v + 1) * SUBLANES, :] for v in range(nv)]
            cnt = [jnp.zeros((SUBLANES, Q_TILE), F32) for _ in range(nv)]
            for i in range(n_rows):
                vi, si = divmod(i, SUBLANES)
                ri = sv[vi][si:si + 1, :]
                for v in range(nv):
                    if v < vi:
                        beats = ri > sv[v]
                    elif v > vi:
                        beats = ri >= sv[v]
                    else:
                        beats = (ri > sv[v]) | ((jv > si) & (ri >= sv[v]))
                    cnt[v] = cnt[v] + jnp.where(beats, 1.0, 0.0)
            pen = jnp.where(jnp.concatenate(cnt, axis=0) < N_SEL, 0.0, NEG)
            if n_rows < n_blk_rows:
                pen = jnp.concatenate([pen, jnp.full((n_blk_rows - n_rows, Q_TILE), NEG, F32)], axis=0)
            pen_sc[gi] = jnp.where(valid, pen, NEG)

    n_poss = (s0 + Q_TILE - 1) // SEL_BLOCK + 1
    bounds = list(range(N_SEL, n_blk_rows, N_SEL)) + [n_blk_rows]
    for lo, hi in zip([0] + bounds[:-1], bounds):
        pl.when((n_poss > lo) & (n_poss <= hi))(functools.partial(rank_rows, hi))

    Qa = []
    for gi in range(GS):
        pen_rows = [jnp.zeros((HEAD_DIM, Q_TILE), F32), pen_sc[gi]]
        if n_blk_rows < HEAD_DIM:
            pen_rows.append(jnp.zeros((HEAD_DIM - n_blk_rows, Q_TILE), F32))
        pen = jnp.concatenate(pen_rows, axis=0).T
        Qa.append(Qs[gi] + jnp.concatenate([pen.astype(BF16)] * Q_PER_KV, axis=0))

    n_full = s0 // SLC_KT
    m_run[...] = jnp.full(m_run.shape, NEG, F32)
    acc_sc[...] = jnp.zeros(acc_sc.shape, F32)

    def scores_tile(kt, causal):
        k0 = pl.multiple_of(kt * SLC_KT, SLC_KT)
        for gi in range(GS):
            s = lax.dot_general(Qa[gi], kp_ref[0, gi, pl.ds(k0, SLC_KT), :], NT_DIMS, preferred_element_type=F32)
            if causal:
                kp_ = k0 + lax.broadcasted_iota(jnp.int32, (1, SLC_KT), 1)
                s = jnp.where(kp_ <= tq, s, NEG)
            s_buf[gi, kt] = s
            mx = m_run[gi]
            for c in range(SLC_KT // LANES):
                mx = jnp.maximum(mx, s[:, c * LANES:(c + 1) * LANES])
            m_run[gi] = mx

    def p1(i, carry):
        scores_tile(2 * i, False)
        scores_tile(2 * i + 1, False)
        return carry

    lax.fori_loop(0, n_full // 2, p1, 0)
    pl.when(n_full % 2 == 1)(lambda: scores_tile(n_full - 1, False))
    scores_tile(n_full, True)
    ms = [jnp.max(m_run[gi], axis=1, keepdims=True) for gi in range(GS)]

    def pv_tile(kt):
        k0 = pl.multiple_of(kt * SLC_KT, SLC_KT)
        for gi in range(GS):
            pe = jnp.exp2(s_buf[gi, kt] - ms[gi]).astype(BF16)
            acc_sc[gi] += jnp.dot(pe, vs_ref[0, gi, pl.ds(k0, SLC_KT), :], preferred_element_type=F32)

    def p2(i, carry):
        pv_tile(2 * i)
        pv_tile(2 * i + 1)
        return carry

    lax.fori_loop(0, (n_full + 1) // 2, p2, 0)
    pl.when(n_full % 2 == 0)(lambda: pv_tile(n_full))

    first_head = lax.broadcasted_iota(jnp.int32, (Q_TILE, LANES), 1) < HEAD_DIM
    for gi in range(GS):
        acc_s = acc_sc[gi]
        o = oc_sc[gi] + (gcol(gi, 1) / acc_s[:, HEAD_DIM:HEAD_DIM + 1]) * acc_s
        for pr in range(Q_PER_KV // 2):
            a = o[(2 * pr) * Q_TILE:(2 * pr + 1) * Q_TILE]
            b = pltpu.roll(o[(2 * pr + 1) * Q_TILE:(2 * pr + 2) * Q_TILE], HEAD_DIM, axis=1)
            c0 = (gi * Q_PER_KV // 2 + pr) * LANES
            o_ref[0, :, c0:c0 + LANES] = jnp.where(first_head, a, b).astype(BF16)


def _attn_prompt(qh, ck, cv, skp, sv, wk, wv, gates):
    B, _, T, _ = qh.shape
    GS = ATTN_GROUPS
    n_cmp_rows = ck.shape[2]
    n_blocks = T // SEL_BLOCK
    n_blk_rows = max(SUBLANES, n_blocks)
    assert n_blocks <= HEAD_DIM and T >= WINDOW + Q_TILE and T % SLC_KT == 0 and KV_HEADS % GS == 0
    n_cmp = T // CMP_STRIDE - CMP_BLOCK // CMP_STRIDE + 1
    at = jnp.asarray(_block_sum_matrix(n_blk_rows, n_cmp_rows, n_blocks, n_cmp))
    q5 = qh.reshape(B, KV_HEADS, Q_PER_KV, T, LANES)
    R = Q_PER_KV * Q_TILE
    full = lambda n: pl.BlockSpec((1, GS, n, LANES), lambda bi, gi, qi: (bi, gi, 0, 0),
                                  pipeline_mode=pl.Buffered(1))
    return pl.pallas_call(
        functools.partial(_attn_prompt_kernel, n_cmp_rows=n_cmp_rows, n_blk_rows=n_blk_rows),
        grid=(B, KV_HEADS // GS, T // Q_TILE),
        in_specs=[
            pl.BlockSpec((1, GS, Q_PER_KV, Q_TILE, LANES), lambda bi, gi, qi: (bi, gi, 0, qi, 0)),
            full(n_cmp_rows), full(n_cmp_rows), full(T), full(T), full(T), full(T),
            pl.BlockSpec((1, Q_TILE, GS * LANES), lambda bi, gi, qi: (bi, qi, gi)),
            pl.BlockSpec((n_blk_rows, n_cmp_rows), lambda bi, gi, qi: (0, 0)),
        ],
        out_specs=pl.BlockSpec((1, Q_TILE, GS * Q_PER_KV * HEAD_DIM), lambda bi, gi, qi: (bi, qi, gi)),
        out_shape=jax.ShapeDtypeStruct((B, T, D_MODEL), BF16),
        scratch_shapes=[pltpu.VMEM((GS, T // SLC_KT, R, SLC_KT), F32), pltpu.VMEM((GS, R, LANES), F32),
                        pltpu.VMEM((GS, R, LANES), F32), pltpu.VMEM((GS, R, LANES), F32),
                        pltpu.VMEM((GS, n_blk_rows, Q_TILE), F32)],
        compiler_params=_cparams(("parallel", "parallel", "arbitrary")),
        name="attn_prompt",
    )(q5, ck, cv, skp, sv, wk, wv, gates, at)


def _oproj_ln_kernel(h_ref, o_ref, w_ref, g_ref, b_ref, out_ref):
    y = jnp.dot(o_ref[0], w_ref[...], preferred_element_type=F32)
    out_ref[0] = _layer_norm(ALPHA * h_ref[0] + y, g_ref[...], b_ref[...])


def _oproj_ln(h, o, w_o, g, b, tT):
    B, T, D = h.shape
    row = pl.BlockSpec((1, tT, D), lambda bi, ti: (bi, ti, 0))
    return pl.pallas_call(
        _oproj_ln_kernel,
        grid=(B, T // tT),
        in_specs=[row, row, _const_spec((D, D)), _const_spec((1, D)), _const_spec((1, D))],
        out_specs=row,
        out_shape=jax.ShapeDtypeStruct((B, T, D), F32),
        compiler_params=_cparams(("parallel", "parallel")),
        name="oproj_ln",
    )(h, o, w_o.astype(BF16), g.reshape(1, D), b.reshape(1, D))


PEN_TILE = PAGES_PER_STEP * (PAGE_SIZE // SEL_BLOCK)
S1_BATCH = 4


def _compress_pg_kernel(ptab_ref, *refs, n_pg):
    pg_refs = refs[:n_pg]
    w4_ref, w1f_ref, pos_ref, b1_ref, w2_ref, b2_ref, ck_ref, cv_ref = refs[n_pg:n_pg + 8]
    xs, carry = refs[n_pg + 8:-1], refs[-1]
    pt = pl.program_id(1)

    @pl.when(pt == 0)
    def _():
        carry[...] = jnp.zeros_like(carry)

    hp = PAGE_SIZE // CMP_STRIDE
    M = n_pg * hp
    first = lax.broadcasted_iota(jnp.int32, (M, CMP_HID), 0) == 0
    for kvi, out_ref in enumerate((ck_ref, cv_ref)):
        b1 = b1_ref[kvi:kvi + 1, :] + jnp.dot(pos_ref[kvi], w1f_ref[kvi], preferred_element_type=F32)[0:1, :]
        for pair in range(KV_HEADS // 2):
            c = kvi * (KV_HEADS // 2) + pair
            for i, r in enumerate(pg_refs):
                xs[c][i * PAGE_SIZE:(i + 1) * PAGE_SIZE, :] = r[0, c * LANES:(c + 1) * LANES, :].T
            acc = None
            for tq in range(CMP_STRIDE // CMP_TGROUP):
                lhs = jnp.concatenate(
                    [xs[c][pl.ds(tq * CMP_TGROUP + k, M, stride=CMP_STRIDE), :].astype(BF16)
                     for k in range(CMP_TGROUP)], axis=1)
                d = jnp.dot(lhs, w4_ref[kvi, tq], preferred_element_type=F32)
                acc = d if acc is None else acc + d
            for gl in range(2):
                g = pair * 2 + gl
                pr0 = acc[:, gl * 2 * CMP_HID:gl * 2 * CMP_HID + CMP_HID]
                pr1 = acc[:, gl * 2 * CMP_HID + CMP_HID:(gl + 1) * 2 * CMP_HID]
                ci = kvi * KV_HEADS + g
                prev = jnp.where(first, carry[ci, 0:1, :], pltpu.roll(pr0, 1, axis=0))
                carry[ci, 0:1, :] = pr0[M - 1:M, :]
                hid = _gelu_tanh(prev + pr1 + b1)
                o = jnp.dot(hid.astype(BF16), w2_ref[kvi], preferred_element_type=F32) + b2_ref[kvi:kvi + 1, :]
                out_ref[0, :, g * HEAD_DIM:(g + 1) * HEAD_DIM] = o.astype(BF16)


def _page_specs(n_pg):
    return [pl.BlockSpec((1, KV_W, PAGE_SIZE),
                         functools.partial(lambda bi, pi, ptab, i: (ptab[bi, pi * n_pg + i], 0, 0), i=i))
            for i in range(n_pg)]


def _compress_pages(pool_t, page_table, cw):
    B, n_pages = page_table.shape
    n_pg = CMP_PAGES_PER_STEP
    assert n_pages % n_pg == 0
    hp = PAGE_SIZE // CMP_STRIDE
    w1cat, w1f, posf, b1, w2, b2 = _compress_weights(cw)
    ntq = CMP_STRIDE // CMP_TGROUP
    w1q = w1cat.reshape(2, ntq, CMP_TGROUP, 1, HEAD_DIM, 1, 2 * CMP_HID)
    eye = jnp.eye(2, dtype=w1cat.dtype).reshape(1, 1, 1, 2, 1, 2, 1)
    w4 = (w1q * eye).reshape(2, ntq, CMP_TGROUP * 2 * HEAD_DIM, 2 * 2 * CMP_HID)
    o_spec = pl.BlockSpec((1, n_pg * hp, KV_W // 2), lambda bi, pi, ptab: (bi, pi, 0))
    o_shape = jax.ShapeDtypeStruct((B, n_pages * hp, KV_W // 2), BF16)
    return pl.pallas_call(
        functools.partial(_compress_pg_kernel, n_pg=n_pg),
        grid_spec=pltpu.PrefetchScalarGridSpec(
            num_scalar_prefetch=1, grid=(B, n_pages // n_pg),
            in_specs=_page_specs(n_pg) + [_const_spec(w.shape) for w in (w4, w1f, posf, b1, w2, b2)],
            out_specs=[o_spec, o_spec],
            scratch_shapes=[pltpu.VMEM((n_pg * PAGE_SIZE, LANES), F32)] * (KV_W // LANES)
            + [pltpu.VMEM((2 * KV_HEADS, SUBLANES, CMP_HID), F32)],
        ),
        out_shape=[o_shape, o_shape],
        compiler_params=_cparams(("parallel", "arbitrary")),
        name="compress_pg",
    )(page_table, *([pool_t] * n_pg), w4, w1f, posf, b1, w2, b2)


def _row_q(shape):
    return lax.broadcasted_iota(jnp.int32, shape, 0) & 7


def _diag_heads(o_all):
    g_row = (lax.broadcasted_iota(jnp.int32, (LANES, HEAD_DIM), 0) // 8) & (KV_HEADS - 1)
    out = jnp.zeros((LANES, HEAD_DIM), F32)
    for g in range(KV_HEADS):
        out = out + jnp.where(g_row == g, o_all[:, g * HEAD_DIM:(g + 1) * HEAD_DIM], 0.0)
    return out


def _attn_s1_kernel(ck_ref, cv_ref, qbr_ref, at_ref, ocmp_ref, pen_ref, score_sc, *, past, n_blocks, n_tiles):
    n_rows = ck_ref.shape[1]
    nb_rows = at_ref.shape[0]
    mrow = lax.broadcasted_iota(jnp.int32, (1, n_rows), 1)
    cend = jnp.where(mrow >= 1, mrow * CMP_STRIDE + (CMP_BLOCK - CMP_STRIDE - 1), jnp.int32(2 ** 30))
    tq = past + _row_q((LANES, 1))
    j = lax.broadcasted_iota(jnp.int32, (nb_rows, LANES), 0)
    lane = lax.broadcasted_iota(jnp.int32, (nb_rows, LANES), 1)
    grp = lane // (LANES // Q_PER_KV)
    tl = past + (lane & 7)
    cur = tl // SEL_BLOCK
    valid = (j * SEL_BLOCK <= tl) & (j < n_blocks)
    forced = valid & ((j == 0) | (j == cur) | (j == cur - 1))
    rows = LANES // Q_PER_KV
    score = None
    for k in range(S1_BATCH):
        s = lax.dot_general(qbr_ref[k], ck_ref[k], NT_DIMS, preferred_element_type=F32)
        s = jnp.where(cend <= tq, s, NEG)
        e = jnp.exp(s - jnp.max(s, axis=1, keepdims=True))
        inv = jnp.where(tq >= CMP_BLOCK - 1, 1.0 / jnp.sum(e, axis=1, keepdims=True), 0.0)
        p = e * inv
        ocmp_ref[k] = _diag_heads(jnp.dot(p.astype(BF16), cv_ref[k], preferred_element_type=F32))
        imp = p[0:rows]
        for r in range(1, Q_PER_KV):
            imp = imp + p[r * rows:(r + 1) * rows]
        imp = jnp.concatenate([imp] * Q_PER_KV, axis=0)
        blk = lax.dot_general(at_ref[...], imp, NT_DIMS, precision=lax.Precision.HIGHEST,
                              preferred_element_type=F32)
        sk = jnp.where(forced, BIG, jnp.where(valid, blk, -1.0))
        score = sk if score is None else jnp.where(grp == k, sk, score)
    score_sc[...] = score

    def body(i, cnt):
        ri = score_sc[pl.ds(i, 1), :]
        beats = (ri > score) | ((ri == score) & (j > i))
        return cnt + jnp.where(beats, 1.0, 0.0)

    cnt = lax.fori_loop(0, n_blocks, body, jnp.zeros((nb_rows, LANES), F32))
    pen4 = jnp.where((cnt < N_SEL) & valid, 0.0, NEG)
    rolled = [pen4] + [pltpu.roll(pen4, sft * rows, axis=1) for sft in range(1, S1_BATCH)]
    n_chunks = -(-n_tiles * PEN_TILE // LANES)
    for k in range(S1_BATCH):
        pen_t = rolled[(0 - k) % S1_BATCH]
        for m in range(1, S1_BATCH):
            pen_t = jnp.where(grp == m, rolled[(m - k) % S1_BATCH], pen_t)
        pen_t = jnp.concatenate([pen_t, jnp.full((n_chunks * LANES - nb_rows, LANES), NEG, F32)], axis=0)
        pen_r = jnp.concatenate([pen_t[c * LANES:(c + 1) * LANES, :].T for c in range(n_chunks)], axis=1)
        for t in range(n_tiles):
            tile = pen_r[:, t * PEN_TILE:(t + 1) * PEN_TILE]
            if PEN_TILE < LANES:
                tile = jnp.concatenate([tile, jnp.zeros((LANES, LANES - PEN_TILE), F32)], axis=1)
            pen_ref[k, t] = tile


def _attn_s2_kernel(ptab_ref, *refs, n_pg):
    pg_refs = refs[:n_pg]
    qbr_ref, pen_ref, e_ref, m_ref, l_ref, acc_ref = refs[n_pg:]
    kw = KV_W // 2
    kt = jnp.concatenate([r[0, 0:kw, :].astype(BF16) for r in pg_refs], axis=1)
    s = jnp.dot(qbr_ref[0], kt, preferred_element_type=F32)
    s = s + jnp.dot(pen_ref[0, 0].astype(BF16), e_ref[...], preferred_element_type=F32)
    m = jnp.max(s, axis=1, keepdims=True)
    pe = jnp.exp(s - m)
    l = jnp.sum(pe, axis=1, keepdims=True)
    vt = jnp.concatenate([r[0, kw:KV_W, :].astype(BF16) for r in pg_refs], axis=1)
    acc_ref[0, 0] = lax.dot_general(pe.astype(BF16), vt, NT_DIMS, preferred_element_type=F32)
    m_ref[0, 0] = jnp.broadcast_to(m, (LANES, LANES))
    l_ref[0, 0] = jnp.broadcast_to(l, (LANES, LANES))


def _attn_s3_kernel(m_ref, l_ref, acc_ref, pen_ref, ocmp_ref, qbr_ref, snew_ref, wst_ref, wnew_ref, gt_ref,
                    o_ref, *, n_pt, n_new):
    qbr = qbr_ref[0]
    kw = KV_W // 2
    nrow = snew_ref.shape[1]
    qr = _row_q((LANES, 1))
    icol = lax.broadcasted_iota(jnp.int32, (1, nrow), 1)
    new_ok = (icol <= qr) & (icol < n_new)

    xs = snew_ref[0]
    st = lax.dot_general(qbr, xs[:, 0:kw].astype(BF16), NT_DIMS, preferred_element_type=F32)
    st = jnp.where(new_ok, st + pen_ref[0, 0][:, 0:1], NEG)
    m_tot = jnp.max(st, axis=1, keepdims=True)
    for s in range(n_pt):
        m_tot = jnp.maximum(m_tot, m_ref[0, s][:, 0:1])
    pt_ = jnp.exp(st - m_tot)
    l_tot = jnp.sum(pt_, axis=1, keepdims=True)
    acc = jnp.dot(pt_.astype(BF16), xs[:, kw:KV_W].astype(BF16), preferred_element_type=F32)
    for s in range(n_pt):
        a = jnp.exp(m_ref[0, s][:, 0:1] - m_tot)
        l_tot = l_tot + a * l_ref[0, s][:, 0:1]
        acc = acc + a * acc_ref[0, s]
    o_slc = _diag_heads(acc / l_tot)

    wn = wnew_ref[0]
    nw = wst_ref.shape[2]
    s1 = jnp.dot(qbr, wst_ref[0, 0:kw, :].astype(BF16), preferred_element_type=F32)
    s2 = lax.dot_general(qbr, wn[:, 0:kw].astype(BF16), NT_DIMS, preferred_element_type=F32)
    dist1 = nw + qr - lax.broadcasted_iota(jnp.int32, (1, nw), 1)
    mask1 = (dist1 >= 0) & (dist1 < WINDOW)
    mask2 = new_ok & (qr - icol < WINDOW)
    s1 = jnp.where(mask1, s1, NEG)
    s2 = jnp.where(mask2, s2, NEG)
    mw = jnp.maximum(jnp.max(s1, axis=1, keepdims=True), jnp.max(s2, axis=1, keepdims=True))
    p1 = jnp.where(mask1, jnp.exp(s1 - mw), 0.0)
    p2 = jnp.where(mask2, jnp.exp(s2 - mw), 0.0)
    inv = 1.0 / (jnp.sum(p1, axis=1, keepdims=True) + jnp.sum(p2, axis=1, keepdims=True))
    ow = (lax.dot_general((p1 * inv).astype(BF16), wst_ref[0, kw:KV_W, :].astype(BF16), NT_DIMS,
                          preferred_element_type=F32)
          + jnp.dot((p2 * inv).astype(BF16), wn[:, kw:KV_W].astype(BF16), preferred_element_type=F32))
    o_win = _diag_heads(ow)

    gv = gt_ref[0]
    o_ref[0] = gv[:, 0:1] * ocmp_ref[0] + gv[:, 1:2] * o_slc + gv[:, 2:3] * o_win


def _attn_sample(q_flat, gates, ck, cv, slc_pool_t, page_table, slc_new, win_state_t, win_new, past):
    B, Tn, _ = q_flat.shape
    n_pg = PAGES_PER_STEP
    assert Tn == 8 and past % (PAGE_SIZE * n_pg) == 0
    n_pages = past // PAGE_SIZE
    n_pt = n_pages // n_pg
    n_past_blk = past // SEL_BLOCK
    n_blocks = n_past_blk + 1
    nb_rows = -(-n_blocks // SUBLANES) * SUBLANES
    n_cmp = past // CMP_STRIDE - 1
    Mc = ck.shape[1]
    at = jnp.asarray(_block_sum_matrix(nb_rows, Mc, n_blocks, n_cmp))
    kw = KV_W // 2

    q5 = q_flat.reshape(B, Tn, KV_HEADS, Q_PER_KV, HEAD_DIM)
    base = jnp.transpose(q5, (0, 3, 2, 1, 4))
    eye = jnp.eye(KV_HEADS, dtype=q_flat.dtype)
    qbr = (base[:, :, :, :, None, :] * eye[None, None, :, None, :, None]).reshape(B, LANES, kw)
    g5 = gates[:, :, :N_HEADS * 3].reshape(B, Tn, KV_HEADS, Q_PER_KV, 3)
    gt = jnp.pad(jnp.transpose(g5, (0, 3, 2, 1, 4)).reshape(B, LANES, 3), ((0, 0), (0, 0), (0, LANES - 3)))

    per_b = lambda *shape: pl.BlockSpec((1,) + shape, lambda bi, *_: (bi,) + (0,) * len(shape))
    assert B % S1_BATCH == 0 and S1_BATCH == Q_PER_KV
    per_sb = lambda *shape: pl.BlockSpec((S1_BATCH,) + shape, lambda bi: (bi,) + (0,) * len(shape))
    ocmp, pen = pl.pallas_call(
        functools.partial(_attn_s1_kernel, past=past, n_blocks=n_blocks, n_tiles=n_pt + 1),
        grid=(B // S1_BATCH,),
        in_specs=[per_sb(Mc, kw), per_sb(Mc, kw), per_sb(LANES, kw),
                  pl.BlockSpec((nb_rows, Mc), lambda bi: (0, 0))],
        out_specs=[per_sb(LANES, HEAD_DIM), per_sb(n_pt + 1, LANES, LANES)],
        out_shape=[jax.ShapeDtypeStruct((B, LANES, HEAD_DIM), F32),
                   jax.ShapeDtypeStruct((B, n_pt + 1, LANES, LANES), F32)],
        scratch_shapes=[pltpu.VMEM((nb_rows, LANES), F32)],
        compiler_params=_cparams(("parallel",)),
        name="attn_s1",
    )(ck, cv, qbr, at)

    expand = np.zeros((LANES, n_pg * PAGE_SIZE), np.float32)
    for jb in range(PEN_TILE):
        expand[jb, jb * SEL_BLOCK:(jb + 1) * SEL_BLOCK] = 1.0
    part = lambda *shape: pl.BlockSpec((1, 1) + shape, lambda bi, pi, ptab: (bi, pi) + (0,) * len(shape))
    m_p, l_p, acc_p = pl.pallas_call(
        functools.partial(_attn_s2_kernel, n_pg=n_pg),
        grid_spec=pltpu.PrefetchScalarGridSpec(
            num_scalar_prefetch=1, grid=(B, n_pt),
            in_specs=_page_specs(n_pg) + [pl.BlockSpec((1, LANES, kw), lambda bi, pi, ptab: (bi, 0, 0)),
                                          part(LANES, LANES),
                                          pl.BlockSpec(expand.shape, lambda bi, pi, ptab: (0, 0))],
            out_specs=[part(LANES, LANES), part(LANES, LANES), part(LANES, kw)],
        ),
        out_shape=[jax.ShapeDtypeStruct((B, n_pt, LANES, LANES), F32)] * 2
        + [jax.ShapeDtypeStruct((B, n_pt, LANES, kw), F32)],
        compiler_params=_cparams(("parallel", "parallel")),
        name="attn_s2",
    )(page_table, *([slc_pool_t] * n_pg), qbr, pen, jnp.asarray(expand, BF16))

    pad_rows = 16 - Tn
    snew = jnp.pad(slc_new, ((0, 0), (0, pad_rows), (0, 0)))
    wnew = jnp.pad(win_new, ((0, 0), (0, pad_rows), (0, 0)))
    nw = win_state_t.shape[2]
    o = pl.pallas_call(
        functools.partial(_attn_s3_kernel, n_pt=n_pt, n_new=Tn),
        grid=(B,),
        in_specs=[per_b(n_pt, LANES, LANES), per_b(n_pt, LANES, LANES), per_b(n_pt, LANES, kw),
                  pl.BlockSpec((1, 1, LANES, LANES), lambda bi: (bi, n_pt, 0, 0)),
                  per_b(LANES, HEAD_DIM), per_b(LANES, kw),
                  per_b(16, KV_W), per_b(KV_W, nw), per_b(16, KV_W), per_b(LANES, LANES)],
        out_specs=per_b(LANES, HEAD_DIM),
        out_shape=jax.ShapeDtypeStruct((B, LANES, HEAD_DIM), F32),
        compiler_params=_cparams(("parallel",)),
        name="attn_s3",
    )(m_p, l_p, acc_p, pen, ocmp, qbr, snew, win_state_t, wnew, gt)
    o = jnp.transpose(o.reshape(B, Q_PER_KV, KV_HEADS, Tn, HEAD_DIM), (0, 3, 2, 1, 4))
    return o.reshape(B, Tn, D_MODEL).astype(BF16)


def _nsa_prompt_layer(h, w_in, w_o, cw, g, b, tT):
    B, T, _ = h.shape
    cmp_kv, slc_kv, win_kv, qh, skp, sv, wk, wv, gates = _nsa_proj(h, jnp.arange(T), w_in, tT, True)
    ck, cv = _compress_prompt(cmp_kv, cw)
    o = _attn_prompt(qh, ck, cv, skp, sv, wk, wv, gates)
    h2 = _oproj_ln(h, o, w_o, g, b, tT)
    win_len = min(WINDOW, T)
    return h2, cmp_kv, slc_kv, win_kv[:, T - win_len:]


def _pages_device_layout(pool):
    return jnp.transpose(pool, (0, 2, 3, 4, 1)).reshape(pool.shape[0], KV_W, pool.shape[1])


def _nsa_sample_layer(h, cmp_pool, slc_pool, win_buf, page_table, w_in, w_o, cw, g, b):
    B, Tn, _ = h.shape
    n_pages = page_table.shape[1]
    past = n_pages * PAGE_SIZE
    rows = B * Tn
    flat = lambda a: a.reshape(1, rows, a.shape[-1])
    cmp_kv, slc_kv, win_kv, q_flat, gates = [
        a.reshape(B, Tn, a.shape[-1])
        for a in _nsa_proj(flat(h), jnp.tile(past + jnp.arange(Tn), B), w_in, rows, False)]
    assert Tn < CMP_STRIDE
    ck, cv = _compress_pages(_pages_device_layout(cmp_pool), page_table, cw)
    o = _attn_sample(q_flat, gates, ck, cv, _pages_device_layout(slc_pool), page_table, slc_kv,
                     _pages_device_layout(win_buf), win_kv, past)
    h2 = _oproj_ln(flat(h), flat(o), w_o, g, b, rows).reshape(B, Tn, -1)
    buf_len = win_buf.shape[1]
    new_win = jnp.concatenate([win_buf, win_kv.reshape(B, Tn, 2, KV_HEADS, HEAD_DIM)], axis=1)[:, -buf_len:]
    return h2, cmp_kv, slc_kv, new_win.reshape(B, buf_len, KV_W)


def _row_tile(T):
    return 512 if T % 512 == 0 else T


def kernel(x_prompt, x_sample, state_pool, cache_cmp_kv, cache_slc_kv, state_win_kv, state_ffn, page_table, ln_g, ln_b, pool_w, pool_scale, nsa_w_in, nsa_w_o, cmp_w1, cmp_b1, cmp_pos, cmp_w2, cmp_b2, ffn_w_up, ffn_conv_w, ffn_conv_b, ffn_w_down):
    Bp, T, D = x_prompt.shape
    Bs, Tn, _ = x_sample.shape
    past = page_table.shape[1] * PAGE_SIZE
    kv_shape = (2, KV_HEADS, HEAD_DIM)
    tp, ts = _row_tile(T), Tn

    hp = _pool_ln(x_prompt, jnp.zeros((Bp, POOL_BUF, D), F32), 0, pool_w[0], pool_scale[0], ln_g[0, 0], ln_b[0, 0], tp)
    hs = _pool_ln(x_sample, state_pool[0], past, pool_w[0], pool_scale[0], ln_g[0, 0], ln_b[0, 0], ts)
    pool_p = jnp.concatenate([jnp.zeros((Bp, POOL_BUF, D), F32), x_prompt], axis=1)[:, -POOL_BUF:][None]
    pool_s = jnp.concatenate([state_pool[0], x_sample], axis=1)[:, -POOL_BUF:][None]
    ffn = lambda h, pre, i, t: _ffn_ln(h, pre, ffn_w_up[i], ffn_conv_w[i], ffn_conv_b[i], ffn_w_down[i],
                                       ln_g[i, 1], ln_b[i, 1], t)
    zero_pre = jnp.zeros((Bp, CONV_W - 1, 2 * D_FF), F32)
    hp, ffn_p0 = ffn(hp, zero_pre, 0, tp)
    ffn_s = lambda h, pre, i: _ffn_ln_short(h, pre, ffn_w_up[i], ffn_conv_w[i], ffn_conv_b[i], ffn_w_down[i],
                                            ln_g[i, 1], ln_b[i, 1])
    hs, ffn_s0 = ffn_s(hs, state_ffn[0], 0)

    cw = (cmp_w1[0], cmp_b1[0], cmp_pos[0], cmp_w2[0], cmp_b2[0])
    hp, c_p, s_p, w_p = _nsa_prompt_layer(hp, nsa_w_in[0], nsa_w_o[0], cw, ln_g[1, 0], ln_b[1, 0], tp)
    hs, c_s, s_s, w_s = _nsa_sample_layer(hs, cache_cmp_kv[0], cache_slc_kv[0], state_win_kv[0], page_table,
                                          nsa_w_in[0], nsa_w_o[0], cw, ln_g[1, 0], ln_b[1, 0])
    hp, ffn_p1 = ffn(hp, zero_pre, 1, tp)
    hs, ffn_s1 = ffn_s(hs, state_ffn[1], 1)

    kv5 = lambda a: a.reshape(a.shape[0], a.shape[1], *kv_shape)[None]
    return (hp, hs, pool_p, pool_s, kv5(c_p), kv5(c_s), kv5(s_p), kv5(s_s), kv5(w_p), kv5(w_s),
            jnp.stack([ffn_p0, ffn_p1]), jnp.stack([ffn_s0, ffn_s1]))
```
